```python
import math
import jax, jax.numpy as jnp
from jax import lax
import numpy as np

D_MODEL = 1024
BATCH = 2
SEQ = 8192
DEPTH = 1

CHUNK = 64
Q_BLOCK = 128
MEM_LEN = 256

N_DIFF_HEADS = 4
DIFF_DK = 64
DIFF_DV = 128
DIFF_WIDTH = N_DIFF_HEADS * DIFF_DV

POOL_WINDOWS = (2, 4, 8, 16)
POOL_GROUPS = len(POOL_WINDOWS)
POOL_DIM = 128
POOL_WIDTH = POOL_GROUPS * POOL_DIM

MIX_WIDTH = DIFF_WIDTH + POOL_WIDTH
IN_WIDTH = 3 * DIFF_WIDTH + POOL_WIDTH

REL_BUCKETS = 32
REL_MAX_DISTANCE = 128

N_CROSS_HEADS = 4
CROSS_HEAD_DIM = D_MODEL // N_CROSS_HEADS

N_EXPERT_GROUPS = 4
EXPERTS_PER_GROUP = 4
N_EXPERTS = N_EXPERT_GROUPS * EXPERTS_PER_GROUP
TOP_K_INNER = 2
EXPERT_FF = D_MODEL // 4

EPS = 1e-6

kernel_name = "hybrid_diffattn_pool_hiermoe_block"


def rmsnorm(x, g):
    xf = x.astype(jnp.float32)
    y = xf * lax.rsqrt(jnp.mean(xf * xf, axis=-1, keepdims=True) + EPS)
    return (y * g.astype(jnp.float32)).astype(x.dtype)


def rel_bucket(rel):
    nb = REL_BUCKETS // 2
    max_exact = nb // 2
    ret = (rel > 0).astype(jnp.int32) * nb
    n = jnp.abs(rel)
    nf = jnp.maximum(n, 1).astype(jnp.float32)
    large = max_exact + (jnp.log(nf / max_exact) / math.log(REL_MAX_DISTANCE / max_exact)
                         * (nb - max_exact)).astype(jnp.int32)
    large = jnp.minimum(large, nb - 1)
    return ret + jnp.where(n < max_exact, n, large)


def diff_attention(q, k, v, lam, rel_bias):
    B, S = q.shape[0], q.shape[1]
    nb = S // Q_BLOCK
    scale = DIFF_DK ** -0.5
    qb_all = q.reshape(B, nb, Q_BLOCK, N_DIFF_HEADS, 2, DIFF_DK).transpose(1, 0, 3, 4, 2, 5)
    kt = k.transpose(0, 2, 3, 1, 4)
    vt = v.transpose(0, 2, 1, 3)
    kpos = jnp.arange(S, dtype=jnp.int32)
    kchunk = kpos // CHUNK
    neg = jnp.finfo(jnp.float32).min

    def block(args):
        qb, bi = args
        qpos = bi * Q_BLOCK + jnp.arange(Q_BLOCK, dtype=jnp.int32)
        bias = rel_bias[rel_bucket(kpos[None, :] - qpos[:, None])]
        bias = bias.transpose(2, 0, 1).astype(jnp.float32)
        allowed = kchunk[None, :] <= (qpos // CHUNK)[:, None]
        s = jnp.einsum("bhmqd,bhmkd->bhmqk", qb, kt).astype(jnp.float32) * scale
        s = s + bias[None, :, None]
        s = jnp.where(allowed[None, None, None], s, neg)
        p = jax.nn.softmax(s, axis=-1)
        a = p[:, :, 0] - lam.astype(jnp.float32) * p[:, :, 1]
        return jnp.einsum("bhqk,bhkd->bhqd", a.astype(vt.dtype), vt)

    o = lax.map(block, (qb_all, jnp.arange(nb, dtype=jnp.int32)))
    return o.transpose(1, 0, 3, 2, 4).reshape(B, S, N_DIFF_HEADS, DIFF_DV)


def multiscale_pool(u, w_pool, scale):
    B, S, _ = u.shape
    uf = u.astype(jnp.float32).reshape(B, S, POOL_GROUPS, POOL_DIM)
    cum = jnp.cumsum(uf, axis=1)
    t1 = jnp.arange(1, S + 1, dtype=jnp.float32)
    outs = []
    for g, w in enumerate(POOL_WINDOWS):
        c = cum[:, :, g]
        lag = jnp.pad(c, ((0, 0), (w, 0), (0, 0)))[:, :S]
        mean = (c - lag) / jnp.minimum(t1, float(w))[None, :, None]
        outs.append(mean - uf[:, :, g])
    d = jnp.stack(outs, axis=2).astype(u.dtype)
    y = jnp.einsum("bsgc,gcd->bsgd", d, w_pool) * scale.reshape(POOL_GROUPS, POOL_DIM)
    return y.reshape(B, S, POOL_WIDTH)


def cross_attention(h, m, wq, wkv, wo):
    B, S, _ = h.shape
    q = (h @ wq).reshape(B, S, N_CROSS_HEADS, CROSS_HEAD_DIM)
    kv = (m @ wkv).reshape(B, m.shape[1], 2, N_CROSS_HEADS, CROSS_HEAD_DIM)
    k, v = kv[:, :, 0], kv[:, :, 1]
    s = jnp.einsum("bqhd,bkhd->bhqk", q, k).astype(jnp.float32) * (CROSS_HEAD_DIM ** -0.5)
    p = jax.nn.softmax(s, axis=-1).astype(v.dtype)
    o = jnp.einsum("bhqk,bkhd->bqhd", p, v).reshape(B, S, D_MODEL)
    return o @ wo


def hier_moe(h, wr_g, br_g, wr_e, br_e, w_gate, w_up, w_down):
    B, S, D = h.shape
    t = h.reshape(B * S, D)
    g_logits = (t @ wr_g + br_g).astype(jnp.float32)
    g_prob = jax.nn.softmax(g_logits, axis=-1)
    g_sel = jnp.argmax(g_logits, axis=-1)
    g_w = jnp.take_along_axis(g_prob, g_sel[:, None], axis=-1)[:, 0]
    e_logits = (jnp.einsum("td,gde->tge", t, wr_e) + br_e).astype(jnp.float32)
    e_sel_logits = jnp.take_along_axis(e_logits, g_sel[:, None, None], axis=1)[:, 0]
    top_v, top_i = lax.top_k(e_sel_logits, TOP_K_INNER)
    top_w = jax.nn.softmax(top_v, axis=-1)
    inner = jnp.einsum("tk,tke->te", top_w, jax.nn.one_hot(top_i, EXPERTS_PER_GROUP, dtype=jnp.float32))
    combine = (g_w[:, None, None] * jax.nn.one_hot(g_sel, N_EXPERT_GROUPS, dtype=jnp.float32)[:, :, None]
               * inner[:, None, :]).reshape(B * S, N_EXPERTS)
    hg = jnp.einsum("td,edf->tef", t, w_gate)
    hu = jnp.einsum("td,edf->tef", t, w_up)
    act = jax.nn.silu(hg) * hu * combine.astype(t.dtype)[:, :, None]
    out = jnp.einsum("tef,efd->td", act, w_down)
    return out.reshape(B, S, D)


def setup_inputs(seed: int = 0) -> dict:
    key = jax.random.key(seed)
    ks = jax.random.split(key, 32)
    f32 = jnp.float32
    nrm = lambda k, shape, s: jax.random.normal(k, shape, f32) * s
    gain = lambda k, shape: 1.0 + 0.05 * jax.random.normal(k, shape, f32)
    L, D = DEPTH, D_MODEL
    return {
        "x": jax.random.normal(ks[0], (BATCH, SEQ, D), f32),
        "mem": jax.random.normal(ks[1], (BATCH, MEM_LEN, D), f32),
        "rel_bias": nrm(ks[2], (REL_BUCKETS, N_DIFF_HEADS), 0.5),
        "attn_norm": gain(ks[3], (L, D)),
        "w_in": nrm(ks[4], (L, D, IN_WIDTH), D ** -0.5),
        "lambda_q1": nrm(ks[5], (L, DIFF_DK), 0.1),
        "lambda_k1": nrm(ks[6], (L, DIFF_DK), 0.1),
        "lambda_q2": nrm(ks[7], (L, DIFF_DK), 0.1),
        "lambda_k2": nrm(ks[8], (L, DIFF_DK), 0.1),
        "diff_subln": gain(ks[9], (L, DIFF_DV)),
        "pool_w": nrm(ks[10], (L, POOL_GROUPS, POOL_DIM, POOL_DIM), POOL_DIM ** -0.5),
        "pool_scale": gain(ks[11], (L, POOL_WIDTH)),
        "w_out": nrm(ks[12], (L, MIX_WIDTH, D), MIX_WIDTH ** -0.5),
        "cross_norm": gain(ks[13], (L, D)),
        "mem_norm": gain(ks[14], (L, D)),
        "wq_cross": nrm(ks[15], (L, D, D), D ** -0.5),
        "wkv_cross": nrm(ks[16], (L, D, 2 * D), D ** -0.5),
        "wo_cross": nrm(ks[17], (L, D, D), D ** -0.5),
        "ffn_norm": gain(ks[18], (L, D)),
        "router_group": nrm(ks[19], (L, D, N_EXPERT_GROUPS), D ** -0.5),
        "router_group_bias": nrm(ks[20], (L, N_EXPERT_GROUPS), 0.01),
        "router_expert": nrm(ks[21], (L, N_EXPERT_GROUPS, D, EXPERTS_PER_GROUP), D ** -0.5),
        "router_expert_bias": nrm(ks[22], (L, N_EXPERT_GROUPS, EXPERTS_PER_GROUP), 0.01),
        "w_gate": nrm(ks[23], (L, N_EXPERTS, D, EXPERT_FF), D ** -0.5),
        "w_up": nrm(ks[24], (L, N_EXPERTS, D, EXPERT_FF), D ** -0.5),
        "w_down": nrm(ks[25], (L, N_EXPERTS, EXPERT_FF, D), EXPERT_FF ** -0.5),
        "final_norm": gain(ks[26], (D,)),
    }


def reference(x, mem, rel_bias, attn_norm, w_in, lambda_q1, lambda_k1, lambda_q2, lambda_k2,
              diff_subln, pool_w, pool_scale, w_out, cross_norm, mem_norm, wq_cross, wkv_cross,
              wo_cross, ffn_norm, router_group, router_group_bias, router_expert,
              router_expert_bias, w_gate, w_up, w_down, final_norm):
    B, S, _ = x.shape
    h = x
    for layer in range(DEPTH):
        lambda_init = 0.8 - 0.6 * math.exp(-0.3 * layer)
        hn = rmsnorm(h, attn_norm[layer])
        z = hn @ w_in[layer]
        q = z[..., :DIFF_WIDTH].reshape(B, S, N_DIFF_HEADS, 2, DIFF_DK)
        k = z[..., DIFF_WIDTH:2 * DIFF_WIDTH].reshape(B, S, N_DIFF_HEADS, 2, DIFF_DK)
        v = z[..., 2 * DIFF_WIDTH:3 * DIFF_WIDTH].reshape(B, S, N_DIFF_HEADS, DIFF_DV)
        u = z[..., 3 * DIFF_WIDTH:]
        lam = (jnp.exp(jnp.sum(lambda_q1[layer] * lambda_k1[layer]))
               - jnp.exp(jnp.sum(lambda_q2[layer] * lambda_k2[layer])) + lambda_init)
        a = diff_attention(q, k, v, lam, rel_bias)
        a = rmsnorm(a, diff_subln[layer]) * (1.0 - lambda_init)
        a = a.reshape(B, S, DIFF_WIDTH)
        p = multiscale_pool(u, pool_w[layer], pool_scale[layer])
        h = h + jnp.concatenate([a, p], axis=-1) @ w_out[layer]
        h = h + cross_attention(rmsnorm(h, cross_norm[layer]), rmsnorm(mem, mem_norm[layer]),
                                wq_cross[layer], wkv_cross[layer], wo_cross[layer])
        h = h + hier_moe(rmsnorm(h, ffn_norm[layer]), router_group[layer], router_group_bias[layer],
                         router_expert[layer], router_expert_bias[layer],
                         w_gate[layer], w_up[layer], w_down[layer])
    return rmsnorm(h, final_norm)
```

```python
import functools
import math

import jax
import jax.numpy as jnp
from jax import lax
from jax.experimental import pallas as pl
from jax.experimental.pallas import tpu as pltpu

F32 = jnp.float32
BF16 = jnp.bfloat16

EPS = 1e-6
CHUNK = 64
N_DIFF_HEADS = 4
DIFF_DK = 64
DIFF_DV = 128
DIFF_WIDTH = N_DIFF_HEADS * DIFF_DV
POOL_WINDOWS = (2, 4, 8, 16)
POOL_DIM = 128
POOL_WIDTH = len(POOL_WINDOWS) * POOL_DIM
MAX_WINDOW = max(POOL_WINDOWS)
REL_BUCKETS = 32
REL_MAX_DISTANCE = 128
N_CROSS_HEADS = 4
N_EXPERT_GROUPS = 4
EXPERTS_PER_GROUP = 4
N_EXPERTS = N_EXPERT_GROUPS * EXPERTS_PER_GROUP
LAMBDA_INIT = 0.8 - 0.6 * math.exp(-0.3 * 0)

LANES = 128
SUBLANES = 8
MXU_DIM = 256
VMEM_LIMIT_BYTES = 56 * 1024 * 1024

ATTN_BLOCK = MXU_DIM
FAR_BUCKET = REL_BUCKETS // 2 - 1
MASK_VALUE = -1e30
M_INIT = float(jnp.finfo(jnp.float32).min)

ROUTER_ROWS = LANES
GROUP_ROW_STRIDE = SUBLANES
COMB_GSEL_LANE = GROUP_ROW_STRIDE * (N_EXPERT_GROUPS + 1)


def _rms(x, g):
    return x * lax.rsqrt(jnp.mean(x * x, axis=-1, keepdims=True) + EPS) * g


def _dot(a, b):
    return jnp.dot(a, b, preferred_element_type=F32)


def _dot_nt(a, b):
    return lax.dot_general(a, b, (((1,), (1,)), ((), ())), preferred_element_type=F32)


def _in_proj_kernel(x_ref, g_ref, wqv_t_ref, wku_ref, qt_ref, vt_ref, k_ref, u_ref):
    hn = _rms(x_ref[...], g_ref[...]).astype(BF16)
    zt = _dot_nt(wqv_t_ref[...], hn)
    qt_ref[...] = (zt[:DIFF_WIDTH] * (DIFF_DK ** -0.5)).astype(BF16)
    for c in range(vt_ref.shape[0]):
        vt_ref[c] = zt[DIFF_WIDTH:, c * ATTN_BLOCK:(c + 1) * ATTN_BLOCK].astype(BF16)
    z = _dot(hn, wku_ref[...])
    k_ref[...] = z[:, :DIFF_WIDTH].astype(BF16)
    u_ref[...] = z[:, DIFF_WIDTH:]


def _in_proj(x, attn_norm, wqv_t, wku, tm):
    B, S, D = x.shape
    nkb = tm // ATTN_BLOCK
    return pl.pallas_call(
        _in_proj_kernel,
        grid=(B, S // tm),
        in_specs=[
            pl.BlockSpec((None, tm, D), lambda b, i: (b, i, 0)),
            pl.BlockSpec((1, D), lambda b, i: (0, 0)),
            pl.BlockSpec(wqv_t.shape, lambda b, i: (0, 0)),
            pl.BlockSpec(wku.shape, lambda b, i: (0, 0)),
        ],
        out_specs=[
            pl.BlockSpec((None, DIFF_WIDTH, tm), lambda b, i: (b, 0, i)),
            pl.BlockSpec((None, nkb, DIFF_WIDTH, ATTN_BLOCK), lambda b, i: (b, i, 0, 0)),
            pl.BlockSpec((None, tm, DIFF_WIDTH), lambda b, i: (b, i, 0)),
            pl.BlockSpec((None, tm, POOL_WIDTH), lambda b, i: (b, i, 0)),
        ],
        out_shape=[
            jax.ShapeDtypeStruct((B, DIFF_WIDTH, S), BF16),
            jax.ShapeDtypeStruct((B, S // ATTN_BLOCK, DIFF_WIDTH, ATTN_BLOCK), BF16),
            jax.ShapeDtypeStruct((B, S, DIFF_WIDTH), BF16),
            jax.ShapeDtypeStruct((B, S, POOL_WIDTH), F32),
        ],
        compiler_params=pltpu.CompilerParams(
            dimension_semantics=("arbitrary", "arbitrary"), vmem_limit_bytes=VMEM_LIMIT_BYTES),
        name="in_proj",
    )(x, attn_norm, wqv_t, wku)


def _rel_bucket(rel):
    nb = REL_BUCKETS // 2
    max_exact = nb // 2
    ret = (rel > 0).astype(jnp.int32) * nb
    n = jnp.abs(rel)
    nf = jnp.maximum(n, 1).astype(jnp.float32)
    large = max_exact + (jnp.log(nf / max_exact) / math.log(REL_MAX_DISTANCE / max_exact)
                         * (nb - max_exact)).astype(jnp.int32)
    large = jnp.minimum(large, nb - 1)
    return ret + jnp.where(n < max_exact, n, large)


def _near_bucket_tiles():
    kk = jnp.arange(ATTN_BLOCK, dtype=jnp.int32)[:, None]
    qq = jnp.arange(ATTN_BLOCK, dtype=jnp.int32)[None, :]
    diag = jnp.where(kk // CHUNK <= qq // CHUNK, _rel_bucket(kk - qq), -1)
    prev = _rel_bucket(kk - ATTN_BLOCK - qq)
    return jnp.stack([diag, prev])


def _bias_tile_kernel(rb_ref, bucket_ref, out_ref):
    h = pl.program_id(0)
    bk = bucket_ref[...]
    val = jnp.zeros(bk.shape, F32)
    for b in range(REL_BUCKETS):
        val = jnp.where(bk == b, rb_ref[b, h], val)
    out_ref[...] = jnp.where(bk < 0, MASK_VALUE, val - rb_ref[FAR_BUCKET, h])


def _bias_tiles(rel_bias):
    buckets = _near_bucket_tiles()
    return pl.pallas_call(
        _bias_tile_kernel,
        grid=(N_DIFF_HEADS,),
        in_specs=[
            pl.BlockSpec(memory_space=pltpu.SMEM),
            pl.BlockSpec(buckets.shape, lambda h: (0, 0, 0)),
        ],
        out_specs=pl.BlockSpec((None,) + buckets.shape, lambda h: (h, 0, 0, 0)),
        out_shape=jax.ShapeDtypeStruct((N_DIFF_HEADS,) + buckets.shape, F32),
        name="bias_tiles",
    )(rel_bias, buckets)


def _attn_kernel(far_ref, qt_ref, k_ref, vt_ref, dt_ref, lam_ref, g_ref, o_ref,
                 kaug, qaug, m_s, l_s, acc_s):
    h = pl.program_id(1)
    i = pl.program_id(2)
    S = k_ref.shape[0]
    QB = ATTN_BLOCK
    KB = ATTN_BLOCK
    pad_rows = 512

    @pl.when(i == 0)
    def _():
        lane = lax.broadcasted_iota(jnp.int32, (pad_rows, LANES), 1)
        pad = jnp.where(lane == 0, far_ref[h, 0],
                        jnp.where(lane == 1, far_ref[h, 1],
                                  jnp.where(lane == 2, far_ref[h, 2], 0.0))).astype(BF16)

        def fill(c, carry):
            r = pl.multiple_of(c * pad_rows, pad_rows)
            kaug[pl.ds(r, pad_rows), :LANES] = k_ref[pl.ds(r, pad_rows), :]
            kaug[pl.ds(r, pad_rows), LANES:] = pad
            return carry

        lax.fori_loop(0, S // pad_rows, fill, 0)

    q = qt_ref[...]
    zeros = jnp.zeros((DIFF_DK, QB), BF16)
    row = lax.broadcasted_iota(jnp.int32, (LANES, QB), 0)
    ones_blk = jnp.where(row < 3, 1.0, 0.0).astype(BF16)
    qaug[0, :DIFF_DK] = q[:DIFF_DK]
    qaug[0, DIFF_DK:2 * DIFF_DK] = zeros
    qaug[0, 2 * DIFF_DK:] = ones_blk
    qaug[1, :DIFF_DK] = zeros
    qaug[1, DIFF_DK:2 * DIFF_DK] = q[DIFF_DK:]
    qaug[1, 2 * DIFF_DK:] = ones_blk

    m_s[...] = jnp.full(m_s.shape, M_INIT, F32)
    l_s[...] = jnp.zeros(l_s.shape, F32)
    acc_s[...] = jnp.zeros(acc_s.shape, F32)

    def step(j, bias):
        r = pl.multiple_of(j * KB, KB)
        kblk = kaug[pl.ds(r, KB), :]
        vblk = vt_ref[j]
        for half in range(2):
            s = _dot(kblk, qaug[half])
            if bias is not None:
                s = s + bias
            m_old = m_s[half]
            m_new = jnp.maximum(m_old, jnp.max(s, axis=0, keepdims=True))
            alpha = jnp.exp(m_old - m_new)
            p = jnp.exp(s - m_new)
            l_s[half] = alpha * l_s[half] + jnp.sum(p, axis=0, keepdims=True)
            acc_s[half] = alpha * acc_s[half] + _dot(vblk, p.astype(BF16))
            m_s[half] = m_new

    def far_body(j, carry):
        step(j, None)
        return carry

    lax.fori_loop(0, jnp.maximum(i - 1, 0), far_body, 0)

    @pl.when(i >= 1)
    def _():
        step(i - 1, dt_ref[1])

    step(i, dt_ref[0])

    lam = (jnp.exp(jnp.sum(lam_ref[0:1] * lam_ref[1:2], axis=-1, keepdims=True))
           - jnp.exp(jnp.sum(lam_ref[2:3] * lam_ref[3:4], axis=-1, keepdims=True)) + LAMBDA_INIT)
    ot = acc_s[0] / l_s[0] - lam * (acc_s[1] / l_s[1])
    o = ot.T
    o_ref[...] = (_rms(o, g_ref[...]) * (1.0 - LAMBDA_INIT)).astype(BF16)


def _diff_attn(far_split, qt, k, vt, dtiles, lam_params, subln):
    B, S, _ = k.shape
    nq = S // ATTN_BLOCK
    return pl.pallas_call(
        _attn_kernel,
        grid=(B, N_DIFF_HEADS, nq),
        in_specs=[
            pl.BlockSpec(memory_space=pltpu.SMEM),
            pl.BlockSpec((None, 2 * DIFF_DK, ATTN_BLOCK), lambda b, h, i: (b, h, i)),
            pl.BlockSpec((None, S, 2 * DIFF_DK), lambda b, h, i: (b, 0, h)),
            pl.BlockSpec((None, nq, DIFF_DV, ATTN_BLOCK), lambda b, h, i: (b, 0, h, 0)),
            pl.BlockSpec((None, 2, ATTN_BLOCK, ATTN_BLOCK), lambda b, h, i: (h, 0, 0, 0)),
            pl.BlockSpec(lam_params.shape, lambda b, h, i: (0, 0)),
            pl.BlockSpec((1, DIFF_DV), lambda b, h, i: (0, 0)),
        ],
        out_specs=pl.BlockSpec((None, ATTN_BLOCK, DIFF_DV), lambda b, h, i: (b, i, h)),
        out_shape=jax.ShapeDtypeStruct((B, S, DIFF_WIDTH), BF16),
        scratch_shapes=[
            pltpu.VMEM((S, 2 * LANES), BF16),
            pltpu.VMEM((2, 2 * LANES, ATTN_BLOCK), BF16),
            pltpu.VMEM((2, 1, ATTN_BLOCK), F32),
            pltpu.VMEM((2, 1, ATTN_BLOCK), F32),
            pltpu.VMEM((2, DIFF_DV, ATTN_BLOCK), F32),
        ],
        compiler_params=pltpu.CompilerParams(
            dimension_semantics=("arbitrary", "arbitrary", "arbitrary"), vmem_limit_bytes=VMEM_LIMIT_BYTES),
        name="diff_attn",
    )(far_split, qt, k, vt, dtiles, lam_params, subln)


def _mem_kv_kernel(mem_ref, g_ref, wk_t_ref, wv_ref, kt_ref, v_ref):
    mn = _rms(mem_ref[...], g_ref[...]).astype(BF16)
    kt_ref[...] = _dot_nt(wk_t_ref[...], mn).astype(BF16)
    v_ref[...] = _dot(mn, wv_ref[...]).astype(BF16)


def _mem_kv(mem, mem_norm, wk_t, wv):
    B, M, D = mem.shape
    return pl.pallas_call(
        _mem_kv_kernel,
        grid=(B,),
        in_specs=[
            pl.BlockSpec((None, M, D), lambda b: (b, 0, 0)),
            pl.BlockSpec((1, D), lambda b: (0, 0)),
            pl.BlockSpec((D, D), lambda b: (0, 0)),
            pl.BlockSpec((D, D), lambda b: (0, 0)),
        ],
        out_specs=[
            pl.BlockSpec((None, D, M), lambda b: (b, 0, 0)),
            pl.BlockSpec((None, M, D), lambda b: (b, 0, 0)),
        ],
        out_shape=[jax.ShapeDtypeStruct((B, D, M), BF16), jax.ShapeDtypeStruct((B, M, D), BF16)],
        compiler_params=pltpu.CompilerParams(
            dimension_semantics=("arbitrary",), vmem_limit_bytes=VMEM_LIMIT_BYTES),
        name="mem_kv",
    )(mem, mem_norm, wk_t, wv)


def _first_argmax_rows(v, vmax):
    row = lax.broadcasted_iota(jnp.int32, v.shape, 0)
    return jnp.min(jnp.where(v == vmax, row, v.shape[0]), axis=0, keepdims=True)


def _mix_cross_kernel(a_ref, u_ref, uprev_ref, x_ref, wout_ref, pw_ref, ps_ref, cn_ref, wq_ref, kt_ref, v_ref,
                      wo_ref, fn_ref, wrh_ref, wrl_ref, rb_ref, h2_ref, comb_ref):
    i = pl.program_id(1)
    tm = u_ref.shape[0]
    hd = kt_ref.shape[0] // N_CROSS_HEADS

    u = u_ref[...]
    prev = jnp.where(i > 0, uprev_ref[...], 0.0)
    pos1 = (i * tm + 1 + lax.broadcasted_iota(jnp.int32, (tm, 1), 0)).astype(F32)
    ys = []
    for g, w in enumerate(POOL_WINDOWS):
        sl = slice(g * POOL_DIM, (g + 1) * POOL_DIM)
        acc = jnp.concatenate([prev[:, sl], u[:, sl]], axis=0)
        span = 1
        while span < w:
            acc = acc + pltpu.roll(acc, span, axis=0)
            span *= 2
        mean = acc[MAX_WINDOW:] / jnp.minimum(pos1, float(w))
        d = (mean - u[:, sl]).astype(BF16)
        ys.append(_dot(d, pw_ref[g]) * ps_ref[:, sl])
    p = jnp.concatenate(ys, axis=-1).astype(BF16)

    h1 = x_ref[...] + _dot(jnp.concatenate([a_ref[...], p], axis=-1), wout_ref[...])

    q = _dot(_rms(h1, cn_ref[...]).astype(BF16), wq_ref[...]).astype(BF16)
    outs = []
    for c in range(N_CROSS_HEADS):
        sl = slice(c * hd, (c + 1) * hd)
        s = _dot(q[:, sl], kt_ref[sl, :]) * (hd ** -0.5)
        e = jnp.exp(s - jnp.max(s, axis=-1, keepdims=True))
        pr = (e / jnp.sum(e, axis=-1, keepdims=True)).astype(BF16)
        outs.append(_dot(pr, v_ref[:, sl]))
    o = jnp.concatenate(outs, axis=-1).astype(BF16)
    h2 = h1 + _dot(o, wo_ref[...])
    h2_ref[...] = h2

    hf = _rms(h2, fn_ref[...])
    hf_hi = hf.astype(BF16)
    hf_lo = (hf - hf_hi.astype(F32)).astype(BF16)
    lg = (_dot_nt(wrh_ref[...], hf_hi) + _dot_nt(wrl_ref[...], hf_hi) + _dot_nt(wrh_ref[...], hf_lo)
          + rb_ref[...])
    R = GROUP_ROW_STRIDE
    gl = lg[:R]
    gmax = jnp.max(gl, axis=0, keepdims=True)
    gsel = _first_argmax_rows(gl, gmax)
    g_w = 1.0 / jnp.sum(jnp.exp(gl - gmax), axis=0, keepdims=True)
    el = lg[R:2 * R]
    for g in range(1, N_EXPERT_GROUPS):
        el = jnp.where(gsel == g, lg[R * (g + 1):R * (g + 2)], el)
    row = lax.broadcasted_iota(jnp.int32, el.shape, 0)
    v0 = jnp.max(el, axis=0, keepdims=True)
    i0 = _first_argmax_rows(el, v0)
    el1 = jnp.where(row == i0, -jnp.inf, el)
    v1 = jnp.max(el1, axis=0, keepdims=True)
    i1 = _first_argmax_rows(el1, v1)
    t = jnp.exp(v1 - v0)
    w0 = 1.0 / (1.0 + t)
    w1 = t / (1.0 + t)
    inner = jnp.where(row == i0, w0, 0.0) + jnp.where(row == i1, w1, 0.0)
    blocks = [jnp.where(gsel == g, g_w * inner, 0.0) for g in range(N_EXPERT_GROUPS)]
    blocks.append(jnp.where(row == 0, gsel.astype(F32), 0.0))
    blocks.append(jnp.zeros((ROUTER_ROWS - R * (N_EXPERT_GROUPS + 1), tm), F32))
    comb_ref[...] = jnp.concatenate(blocks, axis=0).T


def _mix_cross(a, u, x, wout, pool_w, pool_scale, cross_norm, wq, kt, v, wo, ffn_norm, wr_hi, wr_lo, rbias, tm):
    B, S, D = x.shape
    M = v.shape[1]
    prev_blocks = tm // MAX_WINDOW
    const2 = lambda b, i: (0, 0)
    return pl.pallas_call(
        _mix_cross_kernel,
        grid=(B, S // tm),
        in_specs=[
            pl.BlockSpec((None, tm, DIFF_WIDTH), lambda b, i: (b, i, 0)),
            pl.BlockSpec((None, tm, POOL_WIDTH), lambda b, i: (b, i, 0)),
            pl.BlockSpec((None, MAX_WINDOW, POOL_WIDTH), lambda b, i: (b, jnp.maximum(i * prev_blocks - 1, 0), 0)),
            pl.BlockSpec((None, tm, D), lambda b, i: (b, i, 0)),
            pl.BlockSpec(wout.shape, const2),
            pl.BlockSpec(pool_w.shape, lambda b, i: (0, 0, 0)),
            pl.BlockSpec((1, POOL_WIDTH), const2),
            pl.BlockSpec((1, D), const2),
            pl.BlockSpec(wq.shape, const2),
            pl.BlockSpec((None, D, M), lambda b, i: (b, 0, 0)),
            pl.BlockSpec((None, M, D), lambda b, i: (b, 0, 0)),
            pl.BlockSpec(wo.shape, const2),
            pl.BlockSpec((1, D), const2),
            pl.BlockSpec(wr_hi.shape, const2),
            pl.BlockSpec(wr_lo.shape, const2),
            pl.BlockSpec(rbias.shape, const2),
        ],
        out_specs=[
            pl.BlockSpec((None, tm, D), lambda b, i: (b, i, 0)),
            pl.BlockSpec((None, tm, LANES), lambda b, i: (b, i, 0)),
        ],
        out_shape=[jax.ShapeDtypeStruct((B, S, D), F32), jax.ShapeDtypeStruct((B, S, LANES), F32)],
        compiler_params=pltpu.CompilerParams(
            dimension_semantics=("arbitrary", "arbitrary"), vmem_limit_bytes=VMEM_LIMIT_BYTES),
        name="mix_cross",
    )(a, u, u, x, wout, pool_w, pool_scale, cross_norm, wq, kt, v, wo, ffn_norm, wr_hi, wr_lo, rbias)


def _moe_kernel(h2_ref, comb_ref, fn_ref, wg_ref, wu_ref, wd_ref, gn_ref, y_ref, hf_s, acc_s):
    e = pl.program_id(1)

    @pl.when(e == 0)
    def _():
        hf_s[...] = _rms(h2_ref[...], fn_ref[...]).astype(BF16)
        acc_s[...] = jnp.zeros(acc_s.shape, F32)

    hf = hf_s[...]
    hg = _dot(hf, wg_ref[...])
    hu = _dot(hf, wu_ref[...])
    lane = lax.broadcasted_iota(jnp.int32, comb_ref.shape, 1)
    comb_lane = (e // EXPERTS_PER_GROUP) * GROUP_ROW_STRIDE + e % EXPERTS_PER_GROUP
    cw = jnp.sum(jnp.where(lane == comb_lane, comb_ref[...], 0.0), axis=-1, keepdims=True)
    act = (hg * (1.0 / (1.0 + jnp.exp(-hg))) * hu * cw).astype(BF16)
    acc_s[...] += _dot(act, wd_ref[...])

    @pl.when(e == pl.num_programs(1) - 1)
    def _():
        y_ref[...] = _rms(h2_ref[...] + acc_s[...], gn_ref[...])


def _moe(h2, comb, ffn_norm, wg, wu, wd, final_norm, tm):
    T, D = h2.shape
    FF = wg.shape[-1]
    return pl.pallas_call(
        _moe_kernel,
        grid=(T // tm, N_EXPERTS),
        in_specs=[
            pl.BlockSpec((tm, D), lambda t, e: (t, 0)),
            pl.BlockSpec((tm, LANES), lambda t, e: (t, 0)),
            pl.BlockSpec((1, D), lambda t, e: (0, 0)),
            pl.BlockSpec((None, D, FF), lambda t, e: (e, 0, 0)),
            pl.BlockSpec((None, D, FF), lambda t, e: (e, 0, 0)),
            pl.BlockSpec((None, FF, D), lambda t, e: (e, 0, 0)),
            pl.BlockSpec((1, D), lambda t, e: (0, 0)),
        ],
        out_specs=pl.BlockSpec((tm, D), lambda t, e: (t, 0)),
        out_shape=jax.ShapeDtypeStruct((T, D), F32),
        scratch_shapes=[pltpu.VMEM((tm, D), BF16), pltpu.VMEM((tm, D), F32)],
        compiler_params=pltpu.CompilerParams(
            dimension_semantics=("arbitrary", "arbitrary"), vmem_limit_bytes=VMEM_LIMIT_BYTES),
        name="moe",
    )(h2, comb, ffn_norm, wg, wu, wd, final_norm)


def _split3_bf16(c):
    hi = c.astype(BF16).astype(F32)
    mid = (c - hi).astype(BF16).astype(F32)
    lo = (c - hi - mid).astype(BF16).astype(F32)
    return jnp.stack([hi, mid, lo], axis=-1)


def _router_operands(router_group, router_group_bias, router_expert, router_expert_bias):
    D = router_group.shape[0]
    R = GROUP_ROW_STRIDE
    w = jnp.zeros((ROUTER_ROWS, D), F32)
    bias = jnp.zeros((ROUTER_ROWS,), F32)
    slab_pad = jnp.full((R - N_EXPERT_GROUPS,), MASK_VALUE, F32)
    w = w.at[:N_EXPERT_GROUPS].set(router_group.T)
    bias = bias.at[:R].set(jnp.concatenate([router_group_bias, slab_pad]))
    for g in range(N_EXPERT_GROUPS):
        w = w.at[R * (g + 1):R * (g + 1) + EXPERTS_PER_GROUP].set(router_expert[g].T)
        bias = bias.at[R * (g + 1):R * (g + 2)].set(jnp.concatenate([router_expert_bias[g], slab_pad]))
    w_hi = w.astype(BF16)
    w_lo = (w - w_hi.astype(F32)).astype(BF16)
    return w_hi, w_lo, bias[:, None]


def kernel(x, mem, rel_bias, attn_norm, w_in, lambda_q1, lambda_k1, lambda_q2, lambda_k2, diff_subln, pool_w,
           pool_scale, w_out, cross_norm, mem_norm, wq_cross, wkv_cross, wo_cross, ffn_norm, router_group,
           router_group_bias, router_expert, router_expert_bias, w_gate, w_up, w_down, final_norm):
    B, S, D = x.shape
    layer = 0
    w_in_l = w_in[layer]
    wqv_t = jnp.concatenate([w_in_l[:, :DIFF_WIDTH], w_in_l[:, 2 * DIFF_WIDTH:3 * DIFF_WIDTH]], axis=1).T.astype(BF16)
    wku = jnp.concatenate([w_in_l[:, DIFF_WIDTH:2 * DIFF_WIDTH], w_in_l[:, 3 * DIFF_WIDTH:]], axis=1).astype(BF16)
    qt, vt, k, u = _in_proj(x, attn_norm[layer][None], wqv_t, wku, tm=512)

    dtiles = _bias_tiles(rel_bias)
    far_split = _split3_bf16(rel_bias[FAR_BUCKET])
    lam_params = jnp.stack([lambda_q1[layer], lambda_k1[layer], lambda_q2[layer], lambda_k2[layer]])
    a = _diff_attn(far_split, qt, k, vt, dtiles, lam_params, diff_subln[layer][None])

    wkv = wkv_cross[layer]
    kt, v = _mem_kv(mem, mem_norm[layer][None], wkv[:, :D].T.astype(BF16), wkv[:, D:].astype(BF16))

    wr_hi, wr_lo, rbias = _router_operands(router_group[layer], router_group_bias[layer],
                                           router_expert[layer], router_expert_bias[layer])
    h2, comb = _mix_cross(a, u, x, w_out[layer].astype(BF16), pool_w[layer].astype(BF16), pool_scale[layer][None],
                          cross_norm[layer][None], wq_cross[layer].astype(BF16), kt, v,
                          wo_cross[layer].astype(BF16), ffn_norm[layer][None], wr_hi, wr_lo, rbias, tm=512)

    y = _moe(h2.reshape(B * S, D), comb.reshape(B * S, LANES), ffn_norm[layer][None],
             w_gate[layer].astype(BF16), w_up[layer].astype(BF16), w_down[layer].astype(BF16),
             final_norm[None], tm=1024)
    return y.reshape(B, S, D)
```

```python
import functools
import math

import jax
import jax.numpy as jnp
from jax import lax
from jax.experimental import pallas as pl
from jax.experimental.pallas import tpu as pltpu

F32 = jnp.float32
BF16 = jnp.bfloat16

EPS = 1e-6
CHUNK = 64
N_DIFF_HEADS = 4
DIFF_DK = 64
DIFF_DV = 128
DIFF_WIDTH = N_DIFF_HEADS * DIFF_DV
POOL_WINDOWS = (2, 4, 8, 16)
POOL_DIM = 128
POOL_WIDTH = len(POOL_WINDOWS) * POOL_DIM
MAX_WINDOW = max(POOL_WINDOWS)
REL_BUCKETS = 32
REL_MAX_DISTANCE = 128
N_CROSS_HEADS = 4
N_EXPERT_GROUPS = 4
EXPERTS_PER_GROUP = 4
N_EXPERTS = N_EXPERT_GROUPS * EXPERTS_PER_GROUP
LAMBDA_INIT = 0.8 - 0.6 * math.exp(-0.3 * 0)

LANES = 128
SUBLANES = 8
MXU_DIM = 256
VMEM_LIMIT_BYTES = 56 * 1024 * 1024

ATTN_BLOCK = 2 * MXU_DIM
FAR_BUCKET = REL_BUCKETS // 2 - 1
MASK_VALUE = -1e30
M_INIT = float(jnp.finfo(jnp.float32).min)
LOG2E = math.log2(math.e)
V_AUG_ROWS = DIFF_DV + 16

ROUTER_ROWS = LANES
GROUP_ROW_STRIDE = SUBLANES
COMB_GSEL_LANE = GROUP_ROW_STRIDE * (N_EXPERT_GROUPS + 1)


def _rms(x, g):
    return x * lax.rsqrt(jnp.mean(x * x, axis=-1, keepdims=True) + EPS) * g


def _dot(a, b):
    return jnp.dot(a, b, preferred_element_type=F32)


def _dot_nt(a, b):
    return lax.dot_general(a, b, (((1,), (1,)), ((), ())), preferred_element_type=F32)


def _in_proj_kernel(x_ref, g_ref, wqv_t_ref, wku_ref, qt_ref, vt_ref, k_ref, u_ref):
    hn = _rms(x_ref[...], g_ref[...]).astype(BF16)
    zt = _dot_nt(wqv_t_ref[...], hn)
    qt_ref[...] = (zt[:DIFF_WIDTH] * (DIFF_DK ** -0.5 * LOG2E)).astype(BF16)
    for c in range(vt_ref.shape[0]):
        vt_ref[c] = zt[DIFF_WIDTH:, c * ATTN_BLOCK:(c + 1) * ATTN_BLOCK].astype(BF16)
    z = _dot(hn, wku_ref[...])
    k_ref[...] = z[:, :DIFF_WIDTH].astype(BF16)
    u_ref[...] = z[:, DIFF_WIDTH:]


def _in_proj(x, attn_norm, wqv_t, wku, tm):
    B, S, D = x.shape
    nkb = tm // ATTN_BLOCK
    return pl.pallas_call(
        _in_proj_kernel,
        grid=(B, S // tm),
        in_specs=[
            pl.BlockSpec((None, tm, D), lambda b, i: (b, i, 0)),
            pl.BlockSpec((1, D), lambda b, i: (0, 0)),
            pl.BlockSpec(wqv_t.shape, lambda b, i: (0, 0)),
            pl.BlockSpec(wku.shape, lambda b, i: (0, 0)),
        ],
        out_specs=[
            pl.BlockSpec((None, DIFF_WIDTH, tm), lambda b, i: (b, 0, i)),
            pl.BlockSpec((None, nkb, DIFF_WIDTH, ATTN_BLOCK), lambda b, i: (b, i, 0, 0)),
            pl.BlockSpec((None, tm, DIFF_WIDTH), lambda b, i: (b, i, 0)),
            pl.BlockSpec((None, tm, POOL_WIDTH), lambda b, i: (b, i, 0)),
        ],
        out_shape=[
            jax.ShapeDtypeStruct((B, DIFF_WIDTH, S), BF16),
            jax.ShapeDtypeStruct((B, S // ATTN_BLOCK, DIFF_WIDTH, ATTN_BLOCK), BF16),
            jax.ShapeDtypeStruct((B, S, DIFF_WIDTH), BF16),
            jax.ShapeDtypeStruct((B, S, POOL_WIDTH), F32),
        ],
        compiler_params=pltpu.CompilerParams(
            dimension_semantics=("arbitrary", "arbitrary"), vmem_limit_bytes=VMEM_LIMIT_BYTES),
        name="in_proj",
    )(x, attn_norm, wqv_t, wku)


def _rel_bucket(rel):
    nb = REL_BUCKETS // 2
    max_exact = nb // 2
    ret = (rel > 0).astype(jnp.int32) * nb
    n = jnp.abs(rel)
    nf = jnp.maximum(n, 1).astype(jnp.float32)
    large = max_exact + (jnp.log(nf / max_exact) / math.log(REL_MAX_DISTANCE / max_exact)
                         * (nb - max_exact)).astype(jnp.int32)
    large = jnp.minimum(large, nb - 1)
    return ret + jnp.where(n < max_exact, n, large)


def _near_bucket_tiles():
    kk = jnp.arange(ATTN_BLOCK, dtype=jnp.int32)[:, None]
    qq = jnp.arange(ATTN_BLOCK, dtype=jnp.int32)[None, :]
    diag = jnp.where(kk // CHUNK <= qq // CHUNK, _rel_bucket(kk - qq), -1)
    prev = _rel_bucket(kk - ATTN_BLOCK - qq)
    return jnp.stack([diag, prev])


def _bias_tile_kernel(rb_ref, bucket_ref, out_ref):
    h = pl.program_id(0)
    bk = bucket_ref[...]
    val = jnp.zeros(bk.shape, F32)
    for b in range(REL_BUCKETS):
        val = jnp.where(bk == b, rb_ref[b, h], val)
    out_ref[...] = jnp.where(bk < 0, MASK_VALUE, (val - rb_ref[FAR_BUCKET, h]) * LOG2E)


def _bias_tiles(rel_bias):
    buckets = _near_bucket_tiles()
    return pl.pallas_call(
        _bias_tile_kernel,
        grid=(N_DIFF_HEADS,),
        in_specs=[
            pl.BlockSpec(memory_space=pltpu.SMEM),
            pl.BlockSpec(buckets.shape, lambda h: (0, 0, 0)),
        ],
        out_specs=pl.BlockSpec((None,) + buckets.shape, lambda h: (h, 0, 0, 0)),
        out_shape=jax.ShapeDtypeStruct((N_DIFF_HEADS,) + buckets.shape, F32),
        name="bias_tiles",
    )(rel_bias, buckets)


def _attn_kernel(far_ref, qt_ref, k_ref, vt_ref, dt_ref, lam_ref, g_ref, o_ref,
                 kaug, vaug, qaug, m_s, acc_s):
    h = pl.program_id(1)
    i = pl.program_id(2)
    nkb = vt_ref.shape[0]
    QB = ATTN_BLOCK
    KB = ATTN_BLOCK

    @pl.when(i == 0)
    def _():
        lane = lax.broadcasted_iota(jnp.int32, (KB, LANES), 1)
        pad = jnp.where(lane == 0, far_ref[h, 0],
                        jnp.where(lane == 1, far_ref[h, 1],
                                  jnp.where(lane == 2, far_ref[h, 2], 0.0))).astype(BF16)
        vrow = lax.broadcasted_iota(jnp.int32, (V_AUG_ROWS - DIFF_DV, KB), 0)
        ones_row = jnp.where(vrow == 0, 1.0, 0.0).astype(BF16)

        def fill(c, carry):
            r = pl.multiple_of(c * KB, KB)
            kaug[pl.ds(r, KB), :LANES] = k_ref[pl.ds(r, KB), :]
            kaug[pl.ds(r, KB), LANES:] = pad
            vaug[c, :DIFF_DV] = vt_ref[c]
            vaug[c, DIFF_DV:] = ones_row
            return carry

        lax.fori_loop(0, nkb, fill, 0)

    q = qt_ref[...]
    zeros = jnp.zeros((DIFF_DK, QB), BF16)
    row = lax.broadcasted_iota(jnp.int32, (LANES, QB), 0)
    ones_blk = jnp.where(row < 3, 1.0, 0.0).astype(BF16)
    qaug[0, :DIFF_DK] = q[:DIFF_DK]
    qaug[0, DIFF_DK:2 * DIFF_DK] = zeros
    qaug[0, 2 * DIFF_DK:] = ones_blk
    qaug[1, :DIFF_DK] = zeros
    qaug[1, DIFF_DK:2 * DIFF_DK] = q[DIFF_DK:]
    qaug[1, 2 * DIFF_DK:] = ones_blk

    m_s[...] = jnp.full(m_s.shape, M_INIT, F32)
    acc_s[...] = jnp.zeros(acc_s.shape, F32)

    def step(j, bias):
        r = pl.multiple_of(j * KB, KB)
        kblk = kaug[pl.ds(r, KB), :]
        vblk = vaug[j]
        for half in range(2):
            s = _dot(kblk, qaug[half])
            if bias is not None:
                s = s + bias
            m_old = m_s[half]
            m_new = jnp.maximum(m_old, jnp.max(s, axis=0, keepdims=True))
            p = jnp.exp2(s - m_new).astype(BF16)
            acc_s[half] = jnp.exp2(m_old - m_new) * acc_s[half] + _dot(vblk, p)
            m_s[half] = m_new

    def far_body(j, carry):
        step(j, None)
        return carry

    lax.fori_loop(0, jnp.maximum(i - 1, 0), far_body, 0)

    @pl.when(i >= 1)
    def _():
        step(i - 1, dt_ref[1])

    step(i, dt_ref[0])

    lam = (jnp.exp(jnp.sum(lam_ref[0:1] * lam_ref[1:2], axis=-1, keepdims=True))
           - jnp.exp(jnp.sum(lam_ref[2:3] * lam_ref[3:4], axis=-1, keepdims=True)) + LAMBDA_INIT)
    o0 = acc_s[0, :DIFF_DV] / acc_s[0, DIFF_DV:DIFF_DV + 1]
    o1 = acc_s[1, :DIFF_DV] / acc_s[1, DIFF_DV:DIFF_DV + 1]
    o = (o0 - lam * o1).T
    o_ref[...] = (_rms(o, g_ref[...]) * (1.0 - LAMBDA_INIT)).astype(BF16)


def _diff_attn(far_split, qt, k, vt, dtiles, lam_params, subln):
    B, S, _ = k.shape
    nq = S // ATTN_BLOCK
    return pl.pallas_call(
        _attn_kernel,
        grid=(B, N_DIFF_HEADS, nq),
        in_specs=[
            pl.BlockSpec(memory_space=pltpu.SMEM),
            pl.BlockSpec((None, 2 * DIFF_DK, ATTN_BLOCK), lambda b, h, i: (b, h, i)),
            pl.BlockSpec((None, S, 2 * DIFF_DK), lambda b, h, i: (b, 0, h)),
            pl.BlockSpec((None, nq, DIFF_DV, ATTN_BLOCK), lambda b, h, i: (b, 0, h, 0)),
            pl.BlockSpec((None, 2, ATTN_BLOCK, ATTN_BLOCK), lambda b, h, i: (h, 0, 0, 0)),
            pl.BlockSpec(lam_params.shape, lambda b, h, i: (0, 0)),
            pl.BlockSpec((1, DIFF_DV), lambda b, h, i: (0, 0)),
        ],
        out_specs=pl.BlockSpec((None, ATTN_BLOCK, DIFF_DV), lambda b, h, i: (b, i, h)),
        out_shape=jax.ShapeDtypeStruct((B, S, DIFF_WIDTH), BF16),
        scratch_shapes=[
            pltpu.VMEM((S, 2 * LANES), BF16),
            pltpu.VMEM((nq, V_AUG_ROWS, ATTN_BLOCK), BF16),
            pltpu.VMEM((2, 2 * LANES, ATTN_BLOCK), BF16),
            pltpu.VMEM((2, 1, ATTN_BLOCK), F32),
            pltpu.VMEM((2, V_AUG_ROWS, ATTN_BLOCK), F32),
        ],
        compiler_params=pltpu.CompilerParams(
            dimension_semantics=("arbitrary", "arbitrary", "arbitrary"), vmem_limit_bytes=VMEM_LIMIT_BYTES),
        name="diff_attn",
    )(far_split, qt, k, vt, dtiles, lam_params, subln)


def _mem_kv_kernel(mem_ref, g_ref, wk_t_ref, wv_ref, kt_ref, v_ref):
    mn = _rms(mem_ref[...], g_ref[...]).astype(BF16)
    kt_ref[...] = _dot_nt(wk_t_ref[...], mn).astype(BF16)
    v_ref[...] = _dot(mn, wv_ref[...]).astype(BF16)


def _mem_kv(mem, mem_norm, wk_t, wv):
    B, M, D = mem.shape
    return pl.pallas_call(
        _mem_kv_kernel,
        grid=(B,),
        in_specs=[
            pl.BlockSpec((None, M, D), lambda b: (b, 0, 0)),
            pl.BlockSpec((1, D), lambda b: (0, 0)),
            pl.BlockSpec((D, D), lambda b: (0, 0)),
            pl.BlockSpec((D, D), lambda b: (0, 0)),
        ],
        out_specs=[
            pl.BlockSpec((None, D, M), lambda b: (b, 0, 0)),
            pl.BlockSpec((None, M, D), lambda b: (b, 0, 0)),
        ],
        out_shape=[jax.ShapeDtypeStruct((B, D, M), BF16), jax.ShapeDtypeStruct((B, M, D), BF16)],
        compiler_params=pltpu.CompilerParams(
            dimension_semantics=("arbitrary",), vmem_limit_bytes=VMEM_LIMIT_BYTES),
        name="mem_kv",
    )(mem, mem_norm, wk_t, wv)


def _first_argmax_rows(v, vmax):
    row = lax.broadcasted_iota(jnp.int32, v.shape, 0)
    return jnp.min(jnp.where(v == vmax, row, v.shape[0]), axis=0, keepdims=True)


def _mix_cross_kernel(a_ref, u_ref, uprev_ref, x_ref, wout_ref, pw_ref, ps_ref, cn_ref, wq_ref, kt_ref, v_ref,
                      wo_ref, fn_ref, wrh_ref, wrl_ref, rb_ref, h2_ref, comb_ref):
    i = pl.program_id(1)
    tm = u_ref.shape[0]
    hd = kt_ref.shape[0] // N_CROSS_HEADS

    u = u_ref[...]
    prev = jnp.where(i > 0, uprev_ref[...], 0.0)
    pos1 = (i * tm + 1 + lax.broadcasted_iota(jnp.int32, (tm, 1), 0)).astype(F32)
    ys = []
    for g, w in enumerate(POOL_WINDOWS):
        sl = slice(g * POOL_DIM, (g + 1) * POOL_DIM)
        acc = jnp.concatenate([prev[:, sl], u[:, sl]], axis=0)
        span = 1
        while span < w:
            acc = acc + pltpu.roll(acc, span, axis=0)
            span *= 2
        mean = acc[MAX_WINDOW:] / jnp.minimum(pos1, float(w))
        d = (mean - u[:, sl]).astype(BF16)
        ys.append(_dot(d, pw_ref[g]) * ps_ref[:, sl])
    p = jnp.concatenate(ys, axis=-1).astype(BF16)

    h1 = x_ref[...] + _dot(jnp.concatenate([a_ref[...], p], axis=-1), wout_ref[...])

    q = _dot(_rms(h1, cn_ref[...]).astype(BF16), wq_ref[...]).astype(BF16)
    outs = []
    for c in range(N_CROSS_HEADS):
        sl = slice(c * hd, (c + 1) * hd)
        s = _dot(q[:, sl], kt_ref[sl, :]) * (hd ** -0.5)
        e = jnp.exp(s - jnp.max(s, axis=-1, keepdims=True))
        pr = (e / jnp.sum(e, axis=-1, keepdims=True)).astype(BF16)
        outs.append(_dot(pr, v_ref[:, sl]))
    o = jnp.concatenate(outs, axis=-1).astype(BF16)
    h2 = h1 + _dot(o, wo_ref[...])
    h2_ref[...] = h2

    hf = _rms(h2, fn_ref[...])
    hf_hi = hf.astype(BF16)
    hf_lo = (hf - hf_hi.astype(F32)).astype(BF16)
    lg = (_dot_nt(wrh_ref[...], hf_hi) + _dot_nt(wrl_ref[...], hf_hi) + _dot_nt(wrh_ref[...], hf_lo)
          + rb_ref[...])
    R = GROUP_ROW_STRIDE
    gl = lg[:R]
    gmax = jnp.max(gl, axis=0, keepdims=True)
    gsel = _first_argmax_rows(gl, gmax)
    g_w = 1.0 / jnp.sum(jnp.exp(gl - gmax), axis=0, keepdims=True)
    el = lg[R:2 * R]
    for g in range(1, N_EXPERT_GROUPS):
        el = jnp.where(gsel == g, lg[R * (g + 1):R * (g + 2)], el)
    row = lax.broadcasted_iota(jnp.int32, el.shape, 0)
    v0 = jnp.max(el, axis=0, keepdims=True)
    i0 = _first_argmax_rows(el, v0)
    el1 = jnp.where(row == i0, -jnp.inf, el)
    v1 = jnp.max(el1, axis=0, keepdims=True)
    i1 = _first_argmax_rows(el1, v1)
    t = jnp.exp(v1 - v0)
    w0 = 1.0 / (1.0 + t)
    w1 = t / (1.0 + t)
    inner = jnp.where(row == i0, w0, 0.0) + jnp.where(row == i1, w1, 0.0)
    blocks = [jnp.where(gsel == g, g_w * inner, 0.0) for g in range(N_EXPERT_GROUPS)]
    blocks.append(jnp.where(row == 0, gsel.astype(F32), 0.0))
    blocks.append(jnp.zeros((ROUTER_ROWS - R * (N_EXPERT_GROUPS + 1), tm), F32))
    comb_ref[...] = jnp.concatenate(blocks, axis=0).T


def _mix_cross(a, u, x, wout, pool_w, pool_scale, cross_norm, wq, kt, v, wo, ffn_norm, wr_hi, wr_lo, rbias, tm):
    B, S, D = x.shape
    M = v.shape[1]
    prev_blocks = tm // MAX_WINDOW
    const2 = lambda b, i: (0, 0)
    return pl.pallas_call(
        _mix_cross_kernel,
        grid=(B, S // tm),
        in_specs=[
            pl.BlockSpec((None, tm, DIFF_WIDTH), lambda b, i: (b, i, 0)),
            pl.BlockSpec((None, tm, POOL_WIDTH), lambda b, i: (b, i, 0)),
            pl.BlockSpec((None, MAX_WINDOW, POOL_WIDTH), lambda b, i: (b, jnp.maximum(i * prev_blocks - 1, 0), 0)),
            pl.BlockSpec((None, tm, D), lambda b, i: (b, i, 0)),
            pl.BlockSpec(wout.shape, const2),
            pl.BlockSpec(pool_w.shape, lambda b, i: (0, 0, 0)),
            pl.BlockSpec((1, POOL_WIDTH), const2),
            pl.BlockSpec((1, D), const2),
            pl.BlockSpec(wq.shape, const2),
            pl.BlockSpec((None, D, M), lambda b, i: (b, 0, 0)),
            pl.BlockSpec((None, M, D), lambda b, i: (b, 0, 0)),
            pl.BlockSpec(wo.shape, const2),
            pl.BlockSpec((1, D), const2),
            pl.BlockSpec(wr_hi.shape, const2),
            pl.BlockSpec(wr_lo.shape, const2),
            pl.BlockSpec(rbias.shape, const2),
        ],
        out_specs=[
            pl.BlockSpec((None, tm, D), lambda b, i: (b, i, 0)),
            pl.BlockSpec((None, tm, LANES), lambda b, i: (b, i, 0)),
        ],
        out_shape=[jax.ShapeDtypeStruct((B, S, D), F32), jax.ShapeDtypeStruct((B, S, LANES), F32)],
        compiler_params=pltpu.CompilerParams(
            dimension_semantics=("arbitrary", "arbitrary"), vmem_limit_bytes=VMEM_LIMIT_BYTES),
        name="mix_cross",
    )(a, u, u, x, wout, pool_w, pool_scale, cross_norm, wq, kt, v, wo, ffn_norm, wr_hi, wr_lo, rbias)


def _moe_kernel(h2_ref, comb_ref, fn_ref, wg_ref, wu_ref, wd_ref, gn_ref, y_ref, hf_s, acc_s):
    e = pl.program_id(1)

    @pl.when(e == 0)
    def _():
        hf_s[...] = _rms(h2_ref[...], fn_ref[...]).astype(BF16)
        acc_s[...] = jnp.zeros(acc_s.shape, F32)

    hf = hf_s[...]
    hg = _dot(hf, wg_ref[...])
    hu = _dot(hf, wu_ref[...])
    lane = lax.broadcasted_iota(jnp.int32, comb_ref.shape, 1)
    comb_lane = (e // EXPERTS_PER_GROUP) * GROUP_ROW_STRIDE + e % EXPERTS_PER_GROUP
    cw = jnp.sum(jnp.where(lane == comb_lane, comb_ref[...], 0.0), axis=-1, keepdims=True)
    act = (hg * (1.0 / (1.0 + jnp.exp(-hg))) * hu * cw).astype(BF16)
    acc_s[...] += _dot(act, wd_ref[...])

    @pl.when(e == pl.num_programs(1) - 1)
    def _():
        y_ref[...] = _rms(h2_ref[...] + acc_s[...], gn_ref[...])


def _moe(h2, comb, ffn_norm, wg, wu, wd, final_norm, tm):
    T, D = h2.shape
    FF = wg.shape[-1]
    return pl.pallas_call(
        _moe_kernel,
        grid=(T // tm, N_EXPERTS),
        in_specs=[
            pl.BlockSpec((tm, D), lambda t, e: (t, 0)),
            pl.BlockSpec((tm, LANES), lambda t, e: (t, 0)),
            pl.BlockSpec((1, D), lambda t, e: (0, 0)),
            pl.BlockSpec((None, D, FF), lambda t, e: (e, 0, 0)),
            pl.BlockSpec((None, D, FF), lambda t, e: (e, 0, 0)),
            pl.BlockSpec((None, FF, D), lambda t, e: (e, 0, 0)),
            pl.BlockSpec((1, D), lambda t, e: (0, 0)),
        ],
        out_specs=pl.BlockSpec((tm, D), lambda t, e: (t, 0)),
        out_shape=jax.ShapeDtypeStruct((T, D), F32),
        scratch_shapes=[pltpu.VMEM((tm, D), BF16), pltpu.VMEM((tm, D), F32)],
        compiler_params=pltpu.CompilerParams(
            dimension_semantics=("arbitrary", "arbitrary"), vmem_limit_bytes=VMEM_LIMIT_BYTES),
        name="moe",
    )(h2, comb, ffn_norm, wg, wu, wd, final_norm)


def _split3_bf16(c):
    hi = c.astype(BF16).astype(F32)
    mid = (c - hi).astype(BF16).astype(F32)
    lo = (c - hi - mid).astype(BF16).astype(F32)
    return jnp.stack([hi, mid, lo], axis=-1)


def _router_operands(router_group, router_group_bias, router_expert, router_expert_bias):
    D = router_group.shape[0]
    R = GROUP_ROW_STRIDE
    w = jnp.zeros((ROUTER_ROWS, D), F32)
    bias = jnp.zeros((ROUTER_ROWS,), F32)
    slab_pad = jnp.full((R - N_EXPERT_GROUPS,), MASK_VALUE, F32)
    w = w.at[:N_EXPERT_GROUPS].set(router_group.T)
    bias = bias.at[:R].set(jnp.concatenate([router_group_bias, slab_pad]))
    for g in range(N_EXPERT_GROUPS):
        w = w.at[R * (g + 1):R * (g + 1) + EXPERTS_PER_GROUP].set(router_expert[g].T)
        bias = bias.at[R * (g + 1):R * (g + 2)].set(jnp.concatenate([router_expert_bias[g], slab_pad]))
    w_hi = w.astype(BF16)
    w_lo = (w - w_hi.astype(F32)).astype(BF16)
    return w_hi, w_lo, bias[:, None]


def kernel(x, mem, rel_bias, attn_norm, w_in, lambda_q1, lambda_k1, lambda_q2, lambda_k2, diff_subln, pool_w,
           pool_scale, w_out, cross_norm, mem_norm, wq_cross, wkv_cross, wo_cross, ffn_norm, router_group,
           router_group_bias, router_expert, router_expert_bias, w_gate, w_up, w_down, final_norm):
    B, S, D = x.shape
    layer = 0
    w_in_l = w_in[layer]
    wqv_t = jnp.concatenate([w_in_l[:, :DIFF_WIDTH], w_in_l[:, 2 * DIFF_WIDTH:3 * DIFF_WIDTH]], axis=1).T.astype(BF16)
    wku = jnp.concatenate([w_in_l[:, DIFF_WIDTH:2 * DIFF_WIDTH], w_in_l[:, 3 * DIFF_WIDTH:]], axis=1).astype(BF16)
    qt, vt, k, u = _in_proj(x, attn_norm[layer][None], wqv_t, wku, tm=512)

    dtiles = _bias_tiles(rel_bias)
    far_split = _split3_bf16(rel_bias[FAR_BUCKET] * LOG2E)
    lam_params = jnp.stack([lambda_q1[layer], lambda_k1[layer], lambda_q2[layer], lambda_k2[layer]])
    a = _diff_attn(far_split, qt, k, vt, dtiles, lam_params, diff_subln[layer][None])

    wkv = wkv_cross[layer]
    kt, v = _mem_kv(mem, mem_norm[layer][None], wkv[:, :D].T.astype(BF16), wkv[:, D:].astype(BF16))

    wr_hi, wr_lo, rbias = _router_operands(router_group[layer], router_group_bias[layer],
                                           router_expert[layer], router_expert_bias[layer])
    h2, comb = _mix_cross(a, u, x, w_out[layer].astype(BF16), pool_w[layer].astype(BF16), pool_scale[layer][None],
                          cross_norm[layer][None], wq_cross[layer].astype(BF16), kt, v,
                          wo_cross[layer].astype(BF16), ffn_norm[layer][None], wr_hi, wr_lo, rbias, tm=512)

    y = _moe(h2.reshape(B * S, D), comb.reshape(B * S, LANES), ffn_norm[layer][None],
             w_gate[layer].astype(BF16), w_up[layer].astype(BF16), w_down[layer].astype(BF16),
             final_norm[None], tm=1024)
    return y.reshape(B, S, D)
```

```python
import functools
import math

import jax
import jax.numpy as jnp
from jax import lax
from jax.experimental import pallas as pl
from jax.experimental.pallas import tpu as pltpu

F32 = jnp.float32
BF16 = jnp.bfloat16

EPS = 1e-6
CHUNK = 64
N_DIFF_HEADS = 4
DIFF_DK = 64
DIFF_DV = 128
DIFF_WIDTH = N_DIFF_HEADS * DIFF_DV
POOL_WINDOWS = (2, 4, 8, 16)
POOL_DIM = 128
POOL_WIDTH = len(POOL_WINDOWS) * POOL_DIM
MAX_WINDOW = max(POOL_WINDOWS)
REL_BUCKETS = 32
REL_MAX_DISTANCE = 128
N_CROSS_HEADS = 4
N_EXPERT_GROUPS = 4
EXPERTS_PER_GROUP = 4
N_EXPERTS = N_EXPERT_GROUPS * EXPERTS_PER_GROUP
LAMBDA_INIT = 0.8 - 0.6 * math.exp(-0.3 * 0)

LANES = 128
SUBLANES = 8
MXU_DIM = 256
VMEM_LIMIT_BYTES = 56 * 1024 * 1024

ATTN_BLOCK = 2 * MXU_DIM
FAR_BUCKET = REL_BUCKETS // 2 - 1
MASK_VALUE = -1e30
M_FINITE = float(jnp.finfo(jnp.float32).max)
LOG2E = math.log2(math.e)
V_AUG_ROWS = DIFF_DV + 16

ROUTER_ROWS = LANES
GROUP_ROW_STRIDE = SUBLANES
COMB_GSEL_LANE = GROUP_ROW_STRIDE * (N_EXPERT_GROUPS + 1)


def _rms(x, g):
    return x * lax.rsqrt(jnp.mean(x * x, axis=-1, keepdims=True) + EPS) * g


def _dot(a, b):
    return jnp.dot(a, b, preferred_element_type=F32)


def _dot_nt(a, b):
    return lax.dot_general(a, b, (((1,), (1,)), ((), ())), preferred_element_type=F32)


def _in_proj_kernel(x_ref, g_ref, wqv_t_ref, wku_ref, qt_ref, vt_ref, k_ref, u_ref):
    hn = _rms(x_ref[...], g_ref[...]).astype(BF16)
    zt = _dot_nt(wqv_t_ref[...], hn)
    qt_ref[...] = (zt[:DIFF_WIDTH] * (DIFF_DK ** -0.5 * LOG2E)).astype(BF16)
    for c in range(vt_ref.shape[0]):
        vt_ref[c] = zt[DIFF_WIDTH:, c * ATTN_BLOCK:(c + 1) * ATTN_BLOCK].astype(BF16)
    z = _dot(hn, wku_ref[...])
    k_ref[...] = z[:, :DIFF_WIDTH].astype(BF16)
    u_ref[...] = z[:, DIFF_WIDTH:]


def _in_proj(x, attn_norm, wqv_t, wku, tm):
    B, S, D = x.shape
    nkb = tm // ATTN_BLOCK
    return pl.pallas_call(
        _in_proj_kernel,
        grid=(B, S // tm),
        in_specs=[
            pl.BlockSpec((None, tm, D), lambda b, i: (b, i, 0)),
            pl.BlockSpec((1, D), lambda b, i: (0, 0)),
            pl.BlockSpec(wqv_t.shape, lambda b, i: (0, 0)),
            pl.BlockSpec(wku.shape, lambda b, i: (0, 0)),
        ],
        out_specs=[
            pl.BlockSpec((None, DIFF_WIDTH, tm), lambda b, i: (b, 0, i)),
            pl.BlockSpec((None, nkb, DIFF_WIDTH, ATTN_BLOCK), lambda b, i: (b, i, 0, 0)),
            pl.BlockSpec((None, tm, DIFF_WIDTH), lambda b, i: (b, i, 0)),
            pl.BlockSpec((None, tm, POOL_WIDTH), lambda b, i: (b, i, 0)),
        ],
        out_shape=[
            jax.ShapeDtypeStruct((B, DIFF_WIDTH, S), BF16),
            jax.ShapeDtypeStruct((B, S // ATTN_BLOCK, DIFF_WIDTH, ATTN_BLOCK), BF16),
            jax.ShapeDtypeStruct((B, S, DIFF_WIDTH), BF16),
            jax.ShapeDtypeStruct((B, S, POOL_WIDTH), F32),
        ],
        compiler_params=pltpu.CompilerParams(
            dimension_semantics=("arbitrary", "arbitrary"), vmem_limit_bytes=VMEM_LIMIT_BYTES),
        name="in_proj",
    )(x, attn_norm, wqv_t, wku)


def _rel_bucket(rel):
    nb = REL_BUCKETS // 2
    max_exact = nb // 2
    ret = (rel > 0).astype(jnp.int32) * nb
    n = jnp.abs(rel)
    nf = jnp.maximum(n, 1).astype(jnp.float32)
    large = max_exact + (jnp.log(nf / max_exact) / math.log(REL_MAX_DISTANCE / max_exact)
                         * (nb - max_exact)).astype(jnp.int32)
    large = jnp.minimum(large, nb - 1)
    return ret + jnp.where(n < max_exact, n, large)


def _near_bucket_tiles():
    kk = jnp.arange(ATTN_BLOCK, dtype=jnp.int32)[:, None]
    qq = jnp.arange(ATTN_BLOCK, dtype=jnp.int32)[None, :]
    diag = jnp.where(kk // CHUNK <= qq // CHUNK, _rel_bucket(kk - qq), -1)
    prev = _rel_bucket(kk - ATTN_BLOCK - qq)
    return jnp.stack([diag, prev])


def _bias_tile_kernel(rb_ref, bucket_ref, out_ref):
    h = pl.program_id(0)
    bk = bucket_ref[...]
    val = jnp.zeros(bk.shape, F32)
    for b in range(REL_BUCKETS):
        val = jnp.where(bk == b, rb_ref[b, h], val)
    out_ref[...] = jnp.where(bk < 0, MASK_VALUE, (val - rb_ref[FAR_BUCKET, h]) * LOG2E)


def _bias_tiles(rel_bias):
    buckets = _near_bucket_tiles()
    return pl.pallas_call(
        _bias_tile_kernel,
        grid=(N_DIFF_HEADS,),
        in_specs=[
            pl.BlockSpec(memory_space=pltpu.SMEM),
            pl.BlockSpec(buckets.shape, lambda h: (0, 0, 0)),
        ],
        out_specs=pl.BlockSpec((None,) + buckets.shape, lambda h: (h, 0, 0, 0)),
        out_shape=jax.ShapeDtypeStruct((N_DIFF_HEADS,) + buckets.shape, F32),
        name="bias_tiles",
    )(rel_bias, buckets)


def _attn_kernel(far_ref, qt_ref, k_ref, vt_ref, dt_ref, lam_ref, g_ref, o_ref,
                 kaug, vaug, qaug, qref, m_s, acc_s, acc0_s, s_buf, p_buf):
    h = pl.program_id(1)
    i = pl.program_id(2)
    nkb = vt_ref.shape[0]
    QB = ATTN_BLOCK
    KB = ATTN_BLOCK

    @pl.when(i == 0)
    def _():
        lane = lax.broadcasted_iota(jnp.int32, (KB, LANES), 1)
        pad = jnp.where(lane == 0, far_ref[h, 0],
                        jnp.where(lane == 1, far_ref[h, 1],
                                  jnp.where(lane == 2, far_ref[h, 2],
                                            jnp.where(lane < 6, 1.0, 0.0)))).astype(BF16)
        vrow = lax.broadcasted_iota(jnp.int32, (V_AUG_ROWS - DIFF_DV, KB), 0)
        ones_row = jnp.where(vrow == 0, 1.0, 0.0).astype(BF16)

        def fill(c, carry):
            r = pl.multiple_of(c * KB, KB)
            kaug[pl.ds(r, KB), :LANES] = k_ref[pl.ds(r, KB), :]
            kaug[pl.ds(r, KB), LANES:] = pad
            vaug[c, :DIFF_DV] = vt_ref[c]
            vaug[c, DIFF_DV:] = ones_row
            return carry

        lax.fori_loop(0, nkb, fill, 0)

    q = qt_ref[...]
    zeros = jnp.zeros((DIFF_DK, QB), BF16)
    row = lax.broadcasted_iota(jnp.int32, (LANES, QB), 0)
    ones_blk = jnp.where(row < 3, 1.0, 0.0)
    for buf in (qaug, qref):
        buf[0, :DIFF_DK] = q[:DIFF_DK]
        buf[0, DIFF_DK:2 * DIFF_DK] = zeros
        buf[1, :DIFF_DK] = zeros
        buf[1, DIFF_DK:2 * DIFF_DK] = q[DIFF_DK:]
    qaug[0, 2 * DIFF_DK:] = ones_blk.astype(BF16)
    qaug[1, 2 * DIFF_DK:] = ones_blk.astype(BF16)

    rd = pl.multiple_of(i * KB, KB)
    kd = kaug[pl.ds(rd, KB), :]
    vd = vaug[i]
    for half in range(2):
        s_buf[...] = _dot(kd, qaug[half]) + dt_ref[0]
        m = jnp.max(s_buf[...], axis=0, keepdims=True)
        m_s[half] = m
        acc = _dot(vd, jnp.exp2((s_buf[...] - m).astype(BF16)))
        acc_s[half] = acc
        acc0_s[half] = acc
        nm = -m
        hi = nm.astype(BF16).astype(F32)
        mid = (nm - hi).astype(BF16).astype(F32)
        lo = (nm - hi - mid).astype(BF16).astype(F32)
        qref[half, 2 * DIFF_DK:] = jnp.where(
            row == 3, hi, jnp.where(row == 4, mid, jnp.where(row == 5, lo, ones_blk))).astype(BF16)

    def stream_block(t):
        return jnp.where(t == 0, i - 1, t - 1)

    def probabilities(par, j, bias):
        r = pl.multiple_of(j * KB, KB)
        kblk = kaug[pl.ds(r, KB), :]
        for half in range(2):
            s = _dot(kblk, qref[half])
            if bias is not None:
                s = s + bias
            p_buf[par, half] = jnp.exp2(s.astype(BF16))

    def weighted_values(par, j):
        vblk = vaug[j]
        for half in range(2):
            acc_s[half] += _dot(vblk, p_buf[par, half])

    def stream_step(par, t):
        weighted_values(1 - par, stream_block(t - 1))
        probabilities(par, stream_block(t), None)

    @pl.when(i >= 1)
    def _():
        probabilities(0, i - 1, dt_ref[1])

        def stream_pair(g, carry):
            stream_step(1, 2 * g + 1)
            stream_step(0, 2 * g + 2)
            return carry

        lax.fori_loop(0, (i - 1) // 2, stream_pair, 0)

        @pl.when((i - 1) % 2 == 1)
        def _():
            stream_step(1, i - 1)

        for par in range(2):
            @pl.when((i - 1) % 2 == par)
            def _():
                weighted_values(par, stream_block(i - 1))

    n_far = jnp.maximum(i - 1, 0)

    bad = jnp.where(jnp.abs(acc_s[...]) <= M_FINITE, 0.0, 1.0)
    overflowed = jnp.max(bad) > 0.0

    @pl.when(overflowed)
    def _():
        acc_s[...] = acc0_s[...]

        def online_step(j, bias):
            r = pl.multiple_of(j * KB, KB)
            kblk = kaug[pl.ds(r, KB), :]
            vblk = vaug[j]
            for half in range(2):
                s_buf[...] = _dot(kblk, qaug[half])
                if bias is not None:
                    s_buf[...] += bias
                m_old = m_s[half]
                m_new = jnp.maximum(m_old, jnp.max(s_buf[...], axis=0, keepdims=True))
                p = jnp.exp2((s_buf[...] - m_new).astype(BF16))
                acc_s[half] = jnp.exp2(m_old - m_new) * acc_s[half] + _dot(vblk, p)
                m_s[half] = m_new

        @pl.when(i >= 1)
        def _():
            online_step(i - 1, dt_ref[1])

        def far_online(j, carry):
            online_step(j, None)
            return carry

        lax.fori_loop(0, n_far, far_online, 0)

    lam = (jnp.exp(jnp.sum(lam_ref[0:1] * lam_ref[1:2], axis=-1, keepdims=True))
           - jnp.exp(jnp.sum(lam_ref[2:3] * lam_ref[3:4], axis=-1, keepdims=True)) + LAMBDA_INIT)
    o0 = acc_s[0, :DIFF_DV] / acc_s[0, DIFF_DV:DIFF_DV + 1]
    o1 = acc_s[1, :DIFF_DV] / acc_s[1, DIFF_DV:DIFF_DV + 1]
    o = (o0 - lam * o1).T
    o_ref[...] = (_rms(o, g_ref[...]) * (1.0 - LAMBDA_INIT)).astype(BF16)


def _diff_attn(far_split, qt, k, vt, dtiles, lam_params, subln):
    B, S, _ = k.shape
    nq = S // ATTN_BLOCK
    return pl.pallas_call(
        _attn_kernel,
        grid=(B, N_DIFF_HEADS, nq),
        in_specs=[
            pl.BlockSpec(memory_space=pltpu.SMEM),
            pl.BlockSpec((None, 2 * DIFF_DK, ATTN_BLOCK), lambda b, h, i: (b, h, i)),
            pl.BlockSpec((None, S, 2 * DIFF_DK), lambda b, h, i: (b, 0, h)),
            pl.BlockSpec((None, nq, DIFF_DV, ATTN_BLOCK), lambda b, h, i: (b, 0, h, 0)),
            pl.BlockSpec((None, 2, ATTN_BLOCK, ATTN_BLOCK), lambda b, h, i: (h, 0, 0, 0)),
            pl.BlockSpec(lam_params.shape, lambda b, h, i: (0, 0)),
            pl.BlockSpec((1, DIFF_DV), lambda b, h, i: (0, 0)),
        ],
        out_specs=pl.BlockSpec((None, ATTN_BLOCK, DIFF_DV), lambda b, h, i: (b, i, h)),
        out_shape=jax.ShapeDtypeStruct((B, S, DIFF_WIDTH), BF16),
        scratch_shapes=[
            pltpu.VMEM((S, 2 * LANES), BF16),
            pltpu.VMEM((nq, V_AUG_ROWS, ATTN_BLOCK), BF16),
            pltpu.VMEM((2, 2 * LANES, ATTN_BLOCK), BF16),
            pltpu.VMEM((2, 2 * LANES, ATTN_BLOCK), BF16),
            pltpu.VMEM((2, 1, ATTN_BLOCK), F32),
            pltpu.VMEM((2, V_AUG_ROWS, ATTN_BLOCK), F32),
            pltpu.VMEM((2, V_AUG_ROWS, ATTN_BLOCK), F32),
            pltpu.VMEM((ATTN_BLOCK, ATTN_BLOCK), F32),
            pltpu.VMEM((2, 2, ATTN_BLOCK, ATTN_BLOCK), BF16),
        ],
        compiler_params=pltpu.CompilerParams(
            dimension_semantics=("arbitrary", "arbitrary", "arbitrary"), vmem_limit_bytes=VMEM_LIMIT_BYTES),
        name="diff_attn",
    )(far_split, qt, k, vt, dtiles, lam_params, subln)


def _mem_kv_kernel(mem_ref, g_ref, wk_t_ref, wv_ref, kt_ref, v_ref):
    mn = _rms(mem_ref[...], g_ref[...]).astype(BF16)
    kt_ref[...] = _dot_nt(wk_t_ref[...], mn).astype(BF16)
    v_ref[...] = _dot(mn, wv_ref[...]).astype(BF16)


def _mem_kv(mem, mem_norm, wk_t, wv):
    B, M, D = mem.shape
    return pl.pallas_call(
        _mem_kv_kernel,
        grid=(B,),
        in_specs=[
            pl.BlockSpec((None, M, D), lambda b: (b, 0, 0)),
            pl.BlockSpec((1, D), lambda b: (0, 0)),
            pl.BlockSpec((D, D), lambda b: (0, 0)),
            pl.BlockSpec((D, D), lambda b: (0, 0)),
        ],
        out_specs=[
            pl.BlockSpec((None, D, M), lambda b: (b, 0, 0)),
            pl.BlockSpec((None, M, D), lambda b: (b, 0, 0)),
        ],
        out_shape=[jax.ShapeDtypeStruct((B, D, M), BF16), jax.ShapeDtypeStruct((B, M, D), BF16)],
        compiler_params=pltpu.CompilerParams(
            dimension_semantics=("arbitrary",), vmem_limit_bytes=VMEM_LIMIT_BYTES),
        name="mem_kv",
    )(mem, mem_norm, wk_t, wv)


def _first_argmax_rows(v, vmax):
    row = lax.broadcasted_iota(jnp.int32, v.shape, 0)
    return jnp.min(jnp.where(v == vmax, row, v.shape[0]), axis=0, keepdims=True)


def _mix_cross_kernel(a_ref, u_ref, uprev_ref, x_ref, wout_ref, pw_ref, ps_ref, cn_ref, wq_ref, kt_ref, v_ref,
                      wo_ref, fn_ref, wrh_ref, wrl_ref, rb_ref, h2_ref, comb_ref):
    i = pl.program_id(1)
    tm = u_ref.shape[0]
    hd = kt_ref.shape[0] // N_CROSS_HEADS

    u = u_ref[...]
    prev = jnp.where(i > 0, uprev_ref[...], 0.0)
    pos1 = (i * tm + 1 + lax.broadcasted_iota(jnp.int32, (tm, 1), 0)).astype(F32)
    ys = []
    for g, w in enumerate(POOL_WINDOWS):
        sl = slice(g * POOL_DIM, (g + 1) * POOL_DIM)
        acc = jnp.concatenate([prev[:, sl], u[:, sl]], axis=0)
        span = 1
        while span < w:
            acc = acc + pltpu.roll(acc, span, axis=0)
            span *= 2
        mean = acc[MAX_WINDOW:] / jnp.minimum(pos1, float(w))
        d = (mean - u[:, sl]).astype(BF16)
        ys.append(_dot(d, pw_ref[g]) * ps_ref[:, sl])
    p = jnp.concatenate(ys, axis=-1).astype(BF16)

    h1 = x_ref[...] + _dot(jnp.concatenate([a_ref[...], p], axis=-1), wout_ref[...])

    q = _dot(_rms(h1, cn_ref[...]).astype(BF16), wq_ref[...]).astype(BF16)
    outs = []
    for c in range(N_CROSS_HEADS):
        sl = slice(c * hd, (c + 1) * hd)
        s = _dot(q[:, sl], kt_ref[sl, :]) * (hd ** -0.5)
        e = jnp.exp(s - jnp.max(s, axis=-1, keepdims=True))
        pr = (e / jnp.sum(e, axis=-1, keepdims=True)).astype(BF16)
        outs.append(_dot(pr, v_ref[:, sl]))
    o = jnp.concatenate(outs, axis=-1).astype(BF16)
    h2 = h1 + _dot(o, wo_ref[...])
    h2_ref[...] = h2

    hf = _rms(h2, fn_ref[...])
    hf_hi = hf.astype(BF16)
    hf_lo = (hf - hf_hi.astype(F32)).astype(BF16)
    lg = (_dot_nt(wrh_ref[...], hf_hi) + _dot_nt(wrl_ref[...], hf_hi) + _dot_nt(wrh_ref[...], hf_lo)
          + rb_ref[...])
    R = GROUP_ROW_STRIDE
    gl = lg[:R]
    gmax = jnp.max(gl, axis=0, keepdims=True)
    gsel = _first_argmax_rows(gl, gmax)
    g_w = 1.0 / jnp.sum(jnp.exp(gl - gmax), axis=0, keepdims=True)
    el = lg[R:2 * R]
    for g in range(1, N_EXPERT_GROUPS):
        el = jnp.where(gsel == g, lg[R * (g + 1):R * (g + 2)], el)
    row = lax.broadcasted_iota(jnp.int32, el.shape, 0)
    v0 = jnp.max(el, axis=0, keepdims=True)
    i0 = _first_argmax_rows(el, v0)
    el1 = jnp.where(row == i0, -jnp.inf, el)
    v1 = jnp.max(el1, axis=0, keepdims=True)
    i1 = _first_argmax_rows(el1, v1)
    t = jnp.exp(v1 - v0)
    w0 = 1.0 / (1.0 + t)
    w1 = t / (1.0 + t)
    inner = jnp.where(row == i0, w0, 0.0) + jnp.where(row == i1, w1, 0.0)
    blocks = [jnp.where(gsel == g, g_w * inner, 0.0) for g in range(N_EXPERT_GROUPS)]
    blocks.append(jnp.where(row == 0, gsel.astype(F32), 0.0))
    blocks.append(jnp.zeros((ROUTER_ROWS - R * (N_EXPERT_GROUPS + 1), tm), F32))
    comb_ref[...] = jnp.concatenate(blocks, axis=0).T


def _mix_cross(a, u, x, wout, pool_w, pool_scale, cross_norm, wq, kt, v, wo, ffn_norm, wr_hi, wr_lo, rbias, tm):
    B, S, D = x.shape
    M = v.shape[1]
    prev_blocks = tm // MAX_WINDOW
    const2 = lambda b, i: (0, 0)
    return pl.pallas_call(
        _mix_cross_kernel,
        grid=(B, S // tm),
        in_specs=[
            pl.BlockSpec((None, tm, DIFF_WIDTH), lambda b, i: (b, i, 0)),
            pl.BlockSpec((None, tm, POOL_WIDTH), lambda b, i: (b, i, 0)),
            pl.BlockSpec((None, MAX_WINDOW, POOL_WIDTH), lambda b, i: (b, jnp.maximum(i * prev_blocks - 1, 0), 0)),
            pl.BlockSpec((None, tm, D), lambda b, i: (b, i, 0)),
            pl.BlockSpec(wout.shape, const2),
            pl.BlockSpec(pool_w.shape, lambda b, i: (0, 0, 0)),
            pl.BlockSpec((1, POOL_WIDTH), const2),
            pl.BlockSpec((1, D), const2),
            pl.BlockSpec(wq.shape, const2),
            pl.BlockSpec((None, D, M), lambda b, i: (b, 0, 0)),
            pl.BlockSpec((None, M, D), lambda b, i: (b, 0, 0)),
            pl.BlockSpec(wo.shape, const2),
            pl.BlockSpec((1, D), const2),
            pl.BlockSpec(wr_hi.shape, const2),
            pl.BlockSpec(wr_lo.shape, const2),
            pl.BlockSpec(rbias.shape, const2),
        ],
        out_specs=[
            pl.BlockSpec((None, tm, D), lambda b, i: (b, i, 0)),
            pl.BlockSpec((None, tm, LANES), lambda b, i: (b, i, 0)),
        ],
        out_shape=[jax.ShapeDtypeStruct((B, S, D), F32), jax.ShapeDtypeStruct((B, S, LANES), F32)],
        compiler_params=pltpu.CompilerParams(
            dimension_semantics=("arbitrary", "arbitrary"), vmem_limit_bytes=VMEM_LIMIT_BYTES),
        name="mix_cross",
    )(a, u, u, x, wout, pool_w, pool_scale, cross_norm, wq, kt, v, wo, ffn_norm, wr_hi, wr_lo, rbias)


def _moe_kernel(h2_ref, comb_ref, fn_ref, wg_ref, wu_ref, wd_ref, gn_ref, y_ref, hf_s, acc_s):
    e = pl.program_id(1)

    @pl.when(e == 0)
    def _():
        hf_s[...] = _rms(h2_ref[...], fn_ref[...]).astype(BF16)
        acc_s[...] = jnp.zeros(acc_s.shape, F32)

    hf = hf_s[...]
    hg = _dot(hf, wg_ref[...])
    hu = _dot(hf, wu_ref[...])
    lane = lax.broadcasted_iota(jnp.int32, comb_ref.shape, 1)
    comb_lane = (e // EXPERTS_PER_GROUP) * GROUP_ROW_STRIDE + e % EXPERTS_PER_GROUP
    cw = jnp.sum(jnp.where(lane == comb_lane, comb_ref[...], 0.0), axis=-1, keepdims=True)
    act = (hg * (1.0 / (1.0 + jnp.exp(-hg))) * hu * cw).astype(BF16)
    acc_s[...] += _dot(act, wd_ref[...])

    @pl.when(e == pl.num_programs(1) - 1)
    def _():
        y_ref[...] = _rms(h2_ref[...] + acc_s[...], gn_ref[...])


def _moe(h2, comb, ffn_norm, wg, wu, wd, final_norm, tm):
    T, D = h2.shape
    FF = wg.shape[-1]
    return pl.pallas_call(
        _moe_kernel,
        grid=(T // tm, N_EXPERTS),
        in_specs=[
            pl.BlockSpec((tm, D), lambda t, e: (t, 0)),
            pl.BlockSpec((tm, LANES), lambda t, e: (t, 0)),
            pl.BlockSpec((1, D), lambda t, e: (0, 0)),
            pl.BlockSpec((None, D, FF), lambda t, e: (e, 0, 0)),
            pl.BlockSpec((None, D, FF), lambda t, e: (e, 0, 0)),
            pl.BlockSpec((None, FF, D), lambda t, e: (e, 0, 0)),
            pl.BlockSpec((1, D), lambda t, e: (0, 0)),
        ],
        out_specs=pl.BlockSpec((tm, D), lambda t, e: (t, 0)),
        out_shape=jax.ShapeDtypeStruct((T, D), F32),
        scratch_shapes=[pltpu.VMEM((tm, D), BF16), pltpu.VMEM((tm, D), F32)],
        compiler_params=pltpu.CompilerParams(
            dimension_semantics=("arbitrary", "arbitrary"), vmem_limit_bytes=VMEM_LIMIT_BYTES),
        name="moe",
    )(h2, comb, ffn_norm, wg, wu, wd, final_norm)


def _split3_bf16(c):
    hi = c.astype(BF16).astype(F32)
    mid = (c - hi).astype(BF16).astype(F32)
    lo = (c - hi - mid).astype(BF16).astype(F32)
    return jnp.stack([hi, mid, lo], axis=-1)


def _router_operands(router_group, router_group_bias, router_expert, router_expert_bias):
    D = router_group.shape[0]
    R = GROUP_ROW_STRIDE
    w = jnp.zeros((ROUTER_ROWS, D), F32)
    bias = jnp.zeros((ROUTER_ROWS,), F32)
    slab_pad = jnp.full((R - N_EXPERT_GROUPS,), MASK_VALUE, F32)
    w = w.at[:N_EXPERT_GROUPS].set(router_group.T)
    bias = bias.at[:R].set(jnp.concatenate([router_group_bias, slab_pad]))
    for g in range(N_EXPERT_GROUPS):
        w = w.at[R * (g + 1):R * (g + 1) + EXPERTS_PER_GROUP].set(router_expert[g].T)
        bias = bias.at[R * (g + 1):R * (g + 2)].set(jnp.concatenate([router_expert_bias[g], slab_pad]))
    w_hi = w.astype(BF16)
    w_lo = (w - w_hi.astype(F32)).astype(BF16)
    return w_hi, w_lo, bias[:, None]


def kernel(x, mem, rel_bias, attn_norm, w_in, lambda_q1, lambda_k1, lambda_q2, lambda_k2, diff_subln, pool_w,
           pool_scale, w_out, cross_norm, mem_norm, wq_cross, wkv_cross, wo_cross, ffn_norm, router_group,
           router_group_bias, router_expert, router_expert_bias, w_gate, w_up, w_down, final_norm):
    B, S, D = x.shape
    layer = 0
    w_in_l = w_in[layer]
    wqv_t = jnp.concatenate([w_in_l[:, :DIFF_WIDTH], w_in_l[:, 2 * DIFF_WIDTH:3 * DIFF_WIDTH]], axis=1).T.astype(BF16)
    wku = jnp.concatenate([w_in_l[:, DIFF_WIDTH:2 * DIFF_WIDTH], w_in_l[:, 3 * DIFF_WIDTH:]], axis=1).astype(BF16)
    qt, vt, k, u = _in_proj(x, attn_norm[layer][None], wqv_t, wku, tm=512)

    dtiles = _bias_tiles(rel_bias)
    far_split = _split3_bf16(rel_bias[FAR_BUCKET] * LOG2E)
    lam_params = jnp.stack([lambda_q1[layer], lambda_k1[layer], lambda_q2[layer], lambda_k2[layer]])
    a = _diff_attn(far_split, qt, k, vt, dtiles, lam_params, diff_subln[layer][None])

    wkv = wkv_cross[layer]
    kt, v = _mem_kv(mem, mem_norm[layer][None], wkv[:, :D].T.astype(BF16), wkv[:, D:].astype(BF16))

    wr_hi, wr_lo, rbias = _router_operands(router_group[layer], router_group_bias[layer],
                                           router_expert[layer], router_expert_bias[layer])
    h2, comb = _mix_cross(a, u, x, w_out[layer].astype(BF16), pool_w[layer].astype(BF16), pool_scale[layer][None],
                          cross_norm[layer][None], wq_cross[layer].astype(BF16), kt, v,
                          wo_cross[layer].astype(BF16), ffn_norm[layer][None], wr_hi, wr_lo, rbias, tm=512)

    y = _moe(h2.reshape(B * S, D), comb.reshape(B * S, LANES), ffn_norm[layer][None],
             w_gate[layer].astype(BF16), w_up[layer].astype(BF16), w_down[layer].astype(BF16),
             final_norm[None], tm=1024)
    return y.reshape(B, S, D)
```

```python
import functools
import math

import jax
import jax.numpy as jnp
from jax import lax
from jax.experimental import pallas as pl
from jax.experimental.pallas import tpu as pltpu

F32 = jnp.float32
BF16 = jnp.bfloat16

EPS = 1e-6
CHUNK = 64
N_DIFF_HEADS = 4
DIFF_DK = 64
DIFF_DV = 128
DIFF_WIDTH = N_DIFF_HEADS * DIFF_DV
POOL_WINDOWS = (2, 4, 8, 16)
POOL_DIM = 128
POOL_WIDTH = len(POOL_WINDOWS) * POOL_DIM
MAX_WINDOW = max(POOL_WINDOWS)
REL_BUCKETS = 32
REL_MAX_DISTANCE = 128
N_CROSS_HEADS = 4
N_EXPERT_GROUPS = 4
EXPERTS_PER_GROUP = 4
N_EXPERTS = N_EXPERT_GROUPS * EXPERTS_PER_GROUP
LAMBDA_INIT = 0.8 - 0.6 * math.exp(-0.3 * 0)

LANES = 128
SUBLANES = 8
MXU_DIM = 256
VMEM_LIMIT_BYTES = 56 * 1024 * 1024

ATTN_BLOCK = 2 * MXU_DIM
FAR_BUCKET = REL_BUCKETS // 2 - 1
MASK_VALUE = -1e30
M_FINITE = float(jnp.finfo(jnp.float32).max)
LOG2E = math.log2(math.e)
V_AUG_ROWS = DIFF_DV + 16

ROUTER_ROWS = LANES
GROUP_ROW_STRIDE = SUBLANES
ROUTE_GROUP_LANE = GROUP_ROW_STRIDE


def _rms(x, g):
    return x * lax.rsqrt(jnp.mean(x * x, axis=-1, keepdims=True) + EPS) * g


def _dot(a, b):
    return jnp.dot(a, b, preferred_element_type=F32)


def _dot_nt(a, b):
    return lax.dot_general(a, b, (((1,), (1,)), ((), ())), preferred_element_type=F32)


def _in_proj_kernel(x_ref, g_ref, wqv_t_ref, wku_ref, qt_ref, vt_ref, k_ref, u_ref):
    hn = _rms(x_ref[...], g_ref[...]).astype(BF16)
    zt = _dot_nt(wqv_t_ref[...], hn)
    qt_ref[...] = (zt[:DIFF_WIDTH] * (DIFF_DK ** -0.5 * LOG2E)).astype(BF16)
    for c in range(vt_ref.shape[0]):
        vt_ref[c] = zt[DIFF_WIDTH:, c * ATTN_BLOCK:(c + 1) * ATTN_BLOCK].astype(BF16)
    z = _dot(hn, wku_ref[...])
    k_ref[...] = z[:, :DIFF_WIDTH].astype(BF16)
    u_ref[...] = z[:, DIFF_WIDTH:]


def _in_proj(x, attn_norm, wqv_t, wku, tm):
    B, S, D = x.shape
    nkb = tm // ATTN_BLOCK
    return pl.pallas_call(
        _in_proj_kernel,
        grid=(B, S // tm),
        in_specs=[
            pl.BlockSpec((None, tm, D), lambda b, i: (b, i, 0)),
            pl.BlockSpec((1, D), lambda b, i: (0, 0)),
            pl.BlockSpec(wqv_t.shape, lambda b, i: (0, 0)),
            pl.BlockSpec(wku.shape, lambda b, i: (0, 0)),
        ],
        out_specs=[
            pl.BlockSpec((None, DIFF_WIDTH, tm), lambda b, i: (b, 0, i)),
            pl.BlockSpec((None, nkb, DIFF_WIDTH, ATTN_BLOCK), lambda b, i: (b, i, 0, 0)),
            pl.BlockSpec((None, tm, DIFF_WIDTH), lambda b, i: (b, i, 0)),
            pl.BlockSpec((None, tm, POOL_WIDTH), lambda b, i: (b, i, 0)),
        ],
        out_shape=[
            jax.ShapeDtypeStruct((B, DIFF_WIDTH, S), BF16),
            jax.ShapeDtypeStruct((B, S // ATTN_BLOCK, DIFF_WIDTH, ATTN_BLOCK), BF16),
            jax.ShapeDtypeStruct((B, S, DIFF_WIDTH), BF16),
            jax.ShapeDtypeStruct((B, S, POOL_WIDTH), F32),
        ],
        compiler_params=pltpu.CompilerParams(
            dimension_semantics=("arbitrary", "arbitrary"), vmem_limit_bytes=VMEM_LIMIT_BYTES),
        name="in_proj",
    )(x, attn_norm, wqv_t, wku)


def _rel_bucket(rel):
    nb = REL_BUCKETS // 2
    max_exact = nb // 2
    ret = (rel > 0).astype(jnp.int32) * nb
    n = jnp.abs(rel)
    nf = jnp.maximum(n, 1).astype(jnp.float32)
    large = max_exact + (jnp.log(nf / max_exact) / math.log(REL_MAX_DISTANCE / max_exact)
                         * (nb - max_exact)).astype(jnp.int32)
    large = jnp.minimum(large, nb - 1)
    return ret + jnp.where(n < max_exact, n, large)


def _near_bucket_tiles():
    kk = jnp.arange(ATTN_BLOCK, dtype=jnp.int32)[:, None]
    qq = jnp.arange(ATTN_BLOCK, dtype=jnp.int32)[None, :]
    diag = jnp.where(kk // CHUNK <= qq // CHUNK, _rel_bucket(kk - qq), -1)
    prev = _rel_bucket(kk - ATTN_BLOCK - qq)
    return jnp.stack([diag, prev])


def _bias_tile_kernel(rb_ref, bucket_ref, out_ref):
    h = pl.program_id(0)
    bk = bucket_ref[...]
    val = jnp.zeros(bk.shape, F32)
    for b in range(REL_BUCKETS):
        val = jnp.where(bk == b, rb_ref[b, h], val)
    out_ref[...] = jnp.where(bk < 0, MASK_VALUE, (val - rb_ref[FAR_BUCKET, h]) * LOG2E)


def _bias_tiles(rel_bias):
    buckets = _near_bucket_tiles()
    return pl.pallas_call(
        _bias_tile_kernel,
        grid=(N_DIFF_HEADS,),
        in_specs=[
            pl.BlockSpec(memory_space=pltpu.SMEM),
            pl.BlockSpec(buckets.shape, lambda h: (0, 0, 0)),
        ],
        out_specs=pl.BlockSpec((None,) + buckets.shape, lambda h: (h, 0, 0, 0)),
        out_shape=jax.ShapeDtypeStruct((N_DIFF_HEADS,) + buckets.shape, F32),
        name="bias_tiles",
    )(rel_bias, buckets)


def _attn_kernel(far_ref, qt_ref, k_ref, vt_ref, dt_ref, lam_ref, g_ref, o_ref,
                 kaug, vaug, qaug, qref, m_s, acc_s, acc0_s, s_buf, p_buf):
    h = pl.program_id(1)
    i = pl.program_id(2)
    nkb = vt_ref.shape[0]
    QB = ATTN_BLOCK
    KB = ATTN_BLOCK

    @pl.when(i == 0)
    def _():
        lane = lax.broadcasted_iota(jnp.int32, (KB, LANES), 1)
        pad = jnp.where(lane == 0, far_ref[h, 0],
                        jnp.where(lane == 1, far_ref[h, 1],
                                  jnp.where(lane == 2, far_ref[h, 2],
                                            jnp.where(lane < 6, 1.0, 0.0)))).astype(BF16)
        vrow = lax.broadcasted_iota(jnp.int32, (V_AUG_ROWS - DIFF_DV, KB), 0)
        ones_row = jnp.where(vrow == 0, 1.0, 0.0).astype(BF16)

        def fill(c, carry):
            r = pl.multiple_of(c * KB, KB)
            kaug[pl.ds(r, KB), :LANES] = k_ref[pl.ds(r, KB), :]
            kaug[pl.ds(r, KB), LANES:] = pad
            vaug[c, :DIFF_DV] = vt_ref[c]
            vaug[c, DIFF_DV:] = ones_row
            return carry

        lax.fori_loop(0, nkb, fill, 0)

    q = qt_ref[...]
    zeros = jnp.zeros((DIFF_DK, QB), BF16)
    row = lax.broadcasted_iota(jnp.int32, (LANES, QB), 0)
    ones_blk = jnp.where(row < 3, 1.0, 0.0)
    for buf in (qaug, qref):
        buf[0, :DIFF_DK] = q[:DIFF_DK]
        buf[0, DIFF_DK:2 * DIFF_DK] = zeros
        buf[1, :DIFF_DK] = zeros
        buf[1, DIFF_DK:2 * DIFF_DK] = q[DIFF_DK:]
    qaug[0, 2 * DIFF_DK:] = ones_blk.astype(BF16)
    qaug[1, 2 * DIFF_DK:] = ones_blk.astype(BF16)

    rd = pl.multiple_of(i * KB, KB)
    kd = kaug[pl.ds(rd, KB), :]
    vd = vaug[i]
    for half in range(2):
        s_buf[...] = _dot(kd, qaug[half]) + dt_ref[0]
        m = jnp.max(s_buf[...], axis=0, keepdims=True)
        m_s[half] = m
        acc = _dot(vd, jnp.exp2((s_buf[...] - m).astype(BF16)))
        acc_s[half] = acc
        acc0_s[half] = acc
        nm = -m
        hi = nm.astype(BF16).astype(F32)
        mid = (nm - hi).astype(BF16).astype(F32)
        lo = (nm - hi - mid).astype(BF16).astype(F32)
        qref[half, 2 * DIFF_DK:] = jnp.where(
            row == 3, hi, jnp.where(row == 4, mid, jnp.where(row == 5, lo, ones_blk))).astype(BF16)

    def stream_block(t):
        return jnp.where(t == 0, i - 1, t - 1)

    def probabilities(par, j, bias):
        r = pl.multiple_of(j * KB, KB)
        kblk = kaug[pl.ds(r, KB), :]
        for half in range(2):
            s = _dot(kblk, qref[half])
            if bias is not None:
                s = s + bias
            p_buf[par, half] = jnp.exp2(s.astype(BF16))

    def weighted_values(par, j):
        vblk = vaug[j]
        for half in range(2):
            acc_s[half] += _dot(vblk, p_buf[par, half])

    def stream_step(par, t):
        weighted_values(1 - par, stream_block(t - 1))
        probabilities(par, stream_block(t), None)

    @pl.when(i >= 1)
    def _():
        probabilities(0, i - 1, dt_ref[1])

        def stream_pair(g, carry):
            stream_step(1, 2 * g + 1)
            stream_step(0, 2 * g + 2)
            return carry

        lax.fori_loop(0, (i - 1) // 2, stream_pair, 0)

        @pl.when((i - 1) % 2 == 1)
        def _():
            stream_step(1, i - 1)

        for par in range(2):
            @pl.when((i - 1) % 2 == par)
            def _():
                weighted_values(par, stream_block(i - 1))

    n_far = jnp.maximum(i - 1, 0)

    bad = jnp.where(jnp.abs(acc_s[...]) <= M_FINITE, 0.0, 1.0)
    overflowed = jnp.max(bad) > 0.0

    @pl.when(overflowed)
    def _():
        acc_s[...] = acc0_s[...]

        def online_step(j, bias):
            r = pl.multiple_of(j * KB, KB)
            kblk = kaug[pl.ds(r, KB), :]
            vblk = vaug[j]
            for half in range(2):
                s_buf[...] = _dot(kblk, qaug[half])
                if bias is not None:
                    s_buf[...] += bias
                m_old = m_s[half]
                m_new = jnp.maximum(m_old, jnp.max(s_buf[...], axis=0, keepdims=True))
                p = jnp.exp2((s_buf[...] - m_new).astype(BF16))
                acc_s[half] = jnp.exp2(m_old - m_new) * acc_s[half] + _dot(vblk, p)
                m_s[half] = m_new

        @pl.when(i >= 1)
        def _():
            online_step(i - 1, dt_ref[1])

        def far_online(j, carry):
            online_step(j, None)
            return carry

        lax.fori_loop(0, n_far, far_online, 0)

    lam = (jnp.exp(jnp.sum(lam_ref[0:1] * lam_ref[1:2], axis=-1, keepdims=True))
           - jnp.exp(jnp.sum(lam_ref[2:3] * lam_ref[3:4], axis=-1, keepdims=True)) + LAMBDA_INIT)
    o0 = acc_s[0, :DIFF_DV] / acc_s[0, DIFF_DV:DIFF_DV + 1]
    o1 = acc_s[1, :DIFF_DV] / acc_s[1, DIFF_DV:DIFF_DV + 1]
    o = (o0 - lam * o1).T
    o_ref[...] = (_rms(o, g_ref[...]) * (1.0 - LAMBDA_INIT)).astype(BF16)


def _diff_attn(far_split, qt, k, vt, dtiles, lam_params, subln):
    B, S, _ = k.shape
    nq = S // ATTN_BLOCK
    return pl.pallas_call(
        _attn_kernel,
        grid=(B, N_DIFF_HEADS, nq),
        in_specs=[
            pl.BlockSpec(memory_space=pltpu.SMEM),
            pl.BlockSpec((None, 2 * DIFF_DK, ATTN_BLOCK), lambda b, h, i: (b, h, i)),
            pl.BlockSpec((None, S, 2 * DIFF_DK), lambda b, h, i: (b, 0, h)),
            pl.BlockSpec((None, nq, DIFF_DV, ATTN_BLOCK), lambda b, h, i: (b, 0, h, 0)),
            pl.BlockSpec((None, 2, ATTN_BLOCK, ATTN_BLOCK), lambda b, h, i: (h, 0, 0, 0)),
            pl.BlockSpec(lam_params.shape, lambda b, h, i: (0, 0)),
            pl.BlockSpec((1, DIFF_DV), lambda b, h, i: (0, 0)),
        ],
        out_specs=pl.BlockSpec((None, ATTN_BLOCK, DIFF_DV), lambda b, h, i: (b, i, h)),
        out_shape=jax.ShapeDtypeStruct((B, S, DIFF_WIDTH), BF16),
        scratch_shapes=[
            pltpu.VMEM((S, 2 * LANES), BF16),
            pltpu.VMEM((nq, V_AUG_ROWS, ATTN_BLOCK), BF16),
            pltpu.VMEM((2, 2 * LANES, ATTN_BLOCK), BF16),
            pltpu.VMEM((2, 2 * LANES, ATTN_BLOCK), BF16),
            pltpu.VMEM((2, 1, ATTN_BLOCK), F32),
            pltpu.VMEM((2, V_AUG_ROWS, ATTN_BLOCK), F32),
            pltpu.VMEM((2, V_AUG_ROWS, ATTN_BLOCK), F32),
            pltpu.VMEM((ATTN_BLOCK, ATTN_BLOCK), F32),
            pltpu.VMEM((2, 2, ATTN_BLOCK, ATTN_BLOCK), BF16),
        ],
        compiler_params=pltpu.CompilerParams(
            dimension_semantics=("arbitrary", "arbitrary", "arbitrary"), vmem_limit_bytes=VMEM_LIMIT_BYTES),
        name="diff_attn",
    )(far_split, qt, k, vt, dtiles, lam_params, subln)


def _mem_kv_kernel(mem_ref, g_ref, wk_t_ref, wv_ref, kt_ref, v_ref):
    mn = _rms(mem_ref[...], g_ref[...]).astype(BF16)
    kt_ref[...] = _dot_nt(wk_t_ref[...], mn).astype(BF16)
    v_ref[...] = _dot(mn, wv_ref[...]).astype(BF16)


def _mem_kv(mem, mem_norm, wk_t, wv):
    B, M, D = mem.shape
    return pl.pallas_call(
        _mem_kv_kernel,
        grid=(B,),
        in_specs=[
            pl.BlockSpec((None, M, D), lambda b: (b, 0, 0)),
            pl.BlockSpec((1, D), lambda b: (0, 0)),
            pl.BlockSpec((D, D), lambda b: (0, 0)),
            pl.BlockSpec((D, D), lambda b: (0, 0)),
        ],
        out_specs=[
            pl.BlockSpec((None, D, M), lambda b: (b, 0, 0)),
            pl.BlockSpec((None, M, D), lambda b: (b, 0, 0)),
        ],
        out_shape=[jax.ShapeDtypeStruct((B, D, M), BF16), jax.ShapeDtypeStruct((B, M, D), BF16)],
        compiler_params=pltpu.CompilerParams(
            dimension_semantics=("arbitrary",), vmem_limit_bytes=VMEM_LIMIT_BYTES),
        name="mem_kv",
    )(mem, mem_norm, wk_t, wv)


def _first_argmax_rows(v, vmax):
    row = lax.broadcasted_iota(jnp.int32, v.shape, 0)
    return jnp.min(jnp.where(v == vmax, row, v.shape[0]), axis=0, keepdims=True)


def _mix_cross_kernel(a_ref, u_ref, uprev_ref, x_ref, wout_ref, pw_ref, ps_ref, cn_ref, wq_ref, kt_ref, v_ref,
                      wo_ref, fn_ref, wrh_ref, wrl_ref, rb_ref, h2c_ref):
    i = pl.program_id(1)
    tm = u_ref.shape[0]
    D = x_ref.shape[1]
    hd = kt_ref.shape[0] // N_CROSS_HEADS

    u = u_ref[...]
    prev = jnp.where(i > 0, uprev_ref[...], 0.0)
    pos1 = (i * tm + 1 + lax.broadcasted_iota(jnp.int32, (tm, 1), 0)).astype(F32)
    ys = []
    for g, w in enumerate(POOL_WINDOWS):
        sl = slice(g * POOL_DIM, (g + 1) * POOL_DIM)
        acc = jnp.concatenate([prev[:, sl], u[:, sl]], axis=0)
        span = 1
        while span < w:
            acc = acc + pltpu.roll(acc, span, axis=0)
            span *= 2
        mean = acc[MAX_WINDOW:] / jnp.minimum(pos1, float(w))
        d = (mean - u[:, sl]).astype(BF16)
        ys.append(_dot(d, pw_ref[g]) * ps_ref[:, sl])
    p = jnp.concatenate(ys, axis=-1).astype(BF16)

    h1 = x_ref[...] + _dot(jnp.concatenate([a_ref[...], p], axis=-1), wout_ref[...])

    q = _dot(_rms(h1, cn_ref[...]).astype(BF16), wq_ref[...]).astype(BF16)
    outs = []
    for c in range(N_CROSS_HEADS):
        sl = slice(c * hd, (c + 1) * hd)
        s = _dot(q[:, sl], kt_ref[sl, :]) * (hd ** -0.5)
        e = jnp.exp(s - jnp.max(s, axis=-1, keepdims=True))
        pr = (e / jnp.sum(e, axis=-1, keepdims=True)).astype(BF16)
        outs.append(_dot(pr, v_ref[:, sl]))
    o = jnp.concatenate(outs, axis=-1).astype(BF16)
    h2 = h1 + _dot(o, wo_ref[...])
    h2c_ref[:, :D] = h2

    hf = _rms(h2, fn_ref[...])
    hf_hi = hf.astype(BF16)
    hf_lo = (hf - hf_hi.astype(F32)).astype(BF16)
    lg = (_dot_nt(wrh_ref[...], hf_hi) + _dot_nt(wrl_ref[...], hf_hi) + _dot_nt(wrh_ref[...], hf_lo)
          + rb_ref[...])
    R = GROUP_ROW_STRIDE
    gl = lg[:R]
    gmax = jnp.max(gl, axis=0, keepdims=True)
    gsel = _first_argmax_rows(gl, gmax)
    g_w = 1.0 / jnp.sum(jnp.exp(gl - gmax), axis=0, keepdims=True)
    el = lg[R:2 * R]
    for g in range(1, N_EXPERT_GROUPS):
        el = jnp.where(gsel == g, lg[R * (g + 1):R * (g + 2)], el)
    row = lax.broadcasted_iota(jnp.int32, el.shape, 0)
    v0 = jnp.max(el, axis=0, keepdims=True)
    i0 = _first_argmax_rows(el, v0)
    el1 = jnp.where(row == i0, -jnp.inf, el)
    v1 = jnp.max(el1, axis=0, keepdims=True)
    i1 = _first_argmax_rows(el1, v1)
    t = jnp.exp(v1 - v0)
    w0 = 1.0 / (1.0 + t)
    w1 = t / (1.0 + t)
    inner = jnp.where(row == i0, w0, 0.0) + jnp.where(row == i1, w1, 0.0)
    blocks = [g_w * inner, jnp.where(row == 0, gsel.astype(F32), 0.0),
              jnp.zeros((ROUTER_ROWS - 2 * R, tm), F32)]
    h2c_ref[:, D:] = jnp.concatenate(blocks, axis=0).T


def _mix_cross(a, u, x, wout, pool_w, pool_scale, cross_norm, wq, kt, v, wo, ffn_norm, wr_hi, wr_lo, rbias, tm):
    B, S, D = x.shape
    M = v.shape[1]
    prev_blocks = tm // MAX_WINDOW
    const2 = lambda b, i: (0, 0)
    return pl.pallas_call(
        _mix_cross_kernel,
        grid=(B, S // tm),
        in_specs=[
            pl.BlockSpec((None, tm, DIFF_WIDTH), lambda b, i: (b, i, 0)),
            pl.BlockSpec((None, tm, POOL_WIDTH), lambda b, i: (b, i, 0)),
            pl.BlockSpec((None, MAX_WINDOW, POOL_WIDTH), lambda b, i: (b, jnp.maximum(i * prev_blocks - 1, 0), 0)),
            pl.BlockSpec((None, tm, D), lambda b, i: (b, i, 0)),
            pl.BlockSpec(wout.shape, const2),
            pl.BlockSpec(pool_w.shape, lambda b, i: (0, 0, 0)),
            pl.BlockSpec((1, POOL_WIDTH), const2),
            pl.BlockSpec((1, D), const2),
            pl.BlockSpec(wq.shape, const2),
            pl.BlockSpec((None, D, M), lambda b, i: (b, 0, 0)),
            pl.BlockSpec((None, M, D), lambda b, i: (b, 0, 0)),
            pl.BlockSpec(wo.shape, const2),
            pl.BlockSpec((1, D), const2),
            pl.BlockSpec(wr_hi.shape, const2),
            pl.BlockSpec(wr_lo.shape, const2),
            pl.BlockSpec(rbias.shape, const2),
        ],
        out_specs=pl.BlockSpec((None, tm, D + LANES), lambda b, i: (b, i, 0)),
        out_shape=jax.ShapeDtypeStruct((B, S, D + LANES), F32),
        compiler_params=pltpu.CompilerParams(
            dimension_semantics=("arbitrary", "arbitrary"), vmem_limit_bytes=VMEM_LIMIT_BYTES),
        name="mix_cross",
    )(a, u, u, x, wout, pool_w, pool_scale, cross_norm, wq, kt, v, wo, ffn_norm, wr_hi, wr_lo, rbias)


def _moe_kernel(glo_ref, ghi_ref, src_ref, h2c_hbm, wg_hbm, wu_hbm, wd_hbm, fn_ref, wg_ref, wu_ref, wd_ref, gn_ref,
                y_hbm, xbuf, ybuf, hf_s, acc_s, wg_x, wu_x, wd_x, gsem, ssem, wsem):
    c = pl.program_id(0)
    ntiles = pl.num_programs(0)
    tm, D = hf_s.shape
    slot = c % 2

    def row_gather(tile, r):
        t = src_ref[tile * tm + r]
        return pltpu.make_async_copy(h2c_hbm.at[pl.ds(t, 1)], xbuf.at[tile % 2, pl.ds(r, 1)], gsem.at[tile % 2])

    def row_scatter(tile, r):
        t = src_ref[tile * tm + r]
        return pltpu.make_async_copy(ybuf.at[tile % 2, pl.ds(r, 1)], y_hbm.at[pl.ds(t, 1)], ssem.at[tile % 2])

    def start_rows(make, tile):
        def body(r, carry):
            make(tile, r).start()
            return carry
        lax.fori_loop(0, tm, body, 0, unroll=8)

    def wait_gather(s):
        pltpu.make_async_copy(h2c_hbm.at[pl.ds(0, tm)], xbuf.at[s], gsem.at[s]).wait()

    def wait_scatter(s):
        pltpu.make_async_copy(ybuf.at[s], y_hbm.at[pl.ds(0, tm)], ssem.at[s]).wait()

    @pl.when(c == 0)
    def _():
        start_rows(row_gather, 0)

    wait_gather(slot)

    x = xbuf[slot]
    h2 = x[:, :D]
    hf_s[...] = _rms(h2, fn_ref[...]).astype(BF16)
    group = x[:, D + ROUTE_GROUP_LANE:D + ROUTE_GROUP_LANE + 1]

    def expert_pass(g, wg, wu, wd, first, gather_next=False, scatter_prev=False):
        hf = hf_s[...]
        in_group = group == g.astype(F32)
        rows_per_expert = tm // EXPERTS_PER_GROUP
        for e in range(EXPERTS_PER_GROUP):
            hg = _dot(hf, wg[e])
            hu = _dot(hf, wu[e])
            cw = jnp.where(in_group, x[:, D + e:D + e + 1], 0.0)
            act = (hg * (1.0 / (1.0 + jnp.exp(-hg))) * hu * cw).astype(BF16)
            part = _dot(act, wd[e])
            if first and e == 0:
                acc_s[...] = part
            else:
                acc_s[...] += part
            for r in range(e * rows_per_expert, (e + 1) * rows_per_expert):
                if gather_next:
                    row_gather(c + 1, r).start()
                if scatter_prev:
                    row_scatter(c - 1, r).start()

    for has_next, has_prev in ((True, False), (True, True), (False, True), (False, False)):
        @pl.when(((c + 1 < ntiles) == has_next) & ((c >= 1) == has_prev))
        def _():
            expert_pass(glo_ref[c], wg_ref, wu_ref, wd_ref, True, gather_next=has_next, scatter_prev=has_prev)

    def extra_pass(g, carry):
        copies = [pltpu.make_async_copy(w_hbm.at[pl.ds(g * EXPERTS_PER_GROUP, EXPERTS_PER_GROUP)], w_x, wsem.at[n])
                  for n, (w_hbm, w_x) in enumerate(((wg_hbm, wg_x), (wu_hbm, wu_x), (wd_hbm, wd_x)))]
        for cp in copies:
            cp.start()
        for cp in copies:
            cp.wait()
        expert_pass(g, wg_x, wu_x, wd_x, False)
        return carry

    lax.fori_loop(glo_ref[c] + 1, ghi_ref[c] + 1, extra_pass, 0)

    @pl.when(c >= 2)
    def _():
        wait_scatter(slot)

    ybuf[slot] = _rms(h2 + acc_s[...], gn_ref[...])

    @pl.when(c == ntiles - 1)
    def _():
        start_rows(row_scatter, c)
        wait_scatter(slot)

        @pl.when(ntiles >= 2)
        def _():
            wait_scatter(1 - slot)


def _moe(h2c, tile_glo, tile_ghi, src, ffn_norm, wg, wu, wd, final_norm, tm):
    T, DL = h2c.shape
    D = DL - LANES
    FF = wg.shape[-1]
    E = EXPERTS_PER_GROUP
    any_spec = pl.BlockSpec(memory_space=pl.ANY)
    grid_spec = pltpu.PrefetchScalarGridSpec(
        num_scalar_prefetch=3,
        grid=(T // tm,),
        in_specs=[
            any_spec, any_spec, any_spec, any_spec,
            pl.BlockSpec((1, D), lambda c, glo, ghi, src: (0, 0)),
            pl.BlockSpec((E, D, FF), lambda c, glo, ghi, src: (glo[c], 0, 0)),
            pl.BlockSpec((E, D, FF), lambda c, glo, ghi, src: (glo[c], 0, 0)),
            pl.BlockSpec((E, FF, D), lambda c, glo, ghi, src: (glo[c], 0, 0)),
            pl.BlockSpec((1, D), lambda c, glo, ghi, src: (0, 0)),
        ],
        out_specs=any_spec,
        scratch_shapes=[
            pltpu.VMEM((2, tm, DL), F32),
            pltpu.VMEM((2, tm, D), F32),
            pltpu.VMEM((tm, D), BF16),
            pltpu.VMEM((tm, D), F32),
            pltpu.VMEM((E, D, FF), BF16),
            pltpu.VMEM((E, D, FF), BF16),
            pltpu.VMEM((E, FF, D), BF16),
            pltpu.SemaphoreType.DMA((2,)),
            pltpu.SemaphoreType.DMA((2,)),
            pltpu.SemaphoreType.DMA((3,)),
        ],
    )
    return pl.pallas_call(
        _moe_kernel,
        grid_spec=grid_spec,
        out_shape=jax.ShapeDtypeStruct((T, D), F32),
        compiler_params=pltpu.CompilerParams(
            dimension_semantics=("arbitrary",), vmem_limit_bytes=VMEM_LIMIT_BYTES),
        name="moe",
    )(tile_glo, tile_ghi, src, h2c, wg, wu, wd, ffn_norm, wg, wu, wd, final_norm)


def _group_sort_plan(group, tm):
    T = group.shape[0]
    onehot = (group[:, None] == jnp.arange(N_EXPERT_GROUPS, dtype=jnp.int32)[None, :]).astype(jnp.int32)
    ranks = jnp.cumsum(onehot, axis=0) - onehot
    counts = jnp.sum(onehot, axis=0)
    starts = jnp.cumsum(counts) - counts
    slot = jnp.sum(onehot * (ranks + starts[None, :]), axis=1)
    src = jnp.zeros((T,), jnp.int32).at[slot].set(jnp.arange(T, dtype=jnp.int32))
    sorted_group = jnp.zeros((T,), jnp.int32).at[slot].set(group)
    return src, sorted_group[::tm], sorted_group[tm - 1::tm]


def _split3_bf16(c):
    hi = c.astype(BF16).astype(F32)
    mid = (c - hi).astype(BF16).astype(F32)
    lo = (c - hi - mid).astype(BF16).astype(F32)
    return jnp.stack([hi, mid, lo], axis=-1)


def _router_operands(router_group, router_group_bias, router_expert, router_expert_bias):
    D = router_group.shape[0]
    R = GROUP_ROW_STRIDE
    w = jnp.zeros((ROUTER_ROWS, D), F32)
    bias = jnp.zeros((ROUTER_ROWS,), F32)
    slab_pad = jnp.full((R - N_EXPERT_GROUPS,), MASK_VALUE, F32)
    w = w.at[:N_EXPERT_GROUPS].set(router_group.T)
    bias = bias.at[:R].set(jnp.concatenate([router_group_bias, slab_pad]))
    for g in range(N_EXPERT_GROUPS):
        w = w.at[R * (g + 1):R * (g + 1) + EXPERTS_PER_GROUP].set(router_expert[g].T)
        bias = bias.at[R * (g + 1):R * (g + 2)].set(jnp.concatenate([router_expert_bias[g], slab_pad]))
    w_hi = w.astype(BF16)
    w_lo = (w - w_hi.astype(F32)).astype(BF16)
    return w_hi, w_lo, bias[:, None]


def kernel(x, mem, rel_bias, attn_norm, w_in, lambda_q1, lambda_k1, lambda_q2, lambda_k2, diff_subln, pool_w,
           pool_scale, w_out, cross_norm, mem_norm, wq_cross, wkv_cross, wo_cross, ffn_norm, router_group,
           router_group_bias, router_expert, router_expert_bias, w_gate, w_up, w_down, final_norm):
    B, S, D = x.shape
    layer = 0
    w_in_l = w_in[layer]
    wqv_t = jnp.concatenate([w_in_l[:, :DIFF_WIDTH], w_in_l[:, 2 * DIFF_WIDTH:3 * DIFF_WIDTH]], axis=1).T.astype(BF16)
    wku = jnp.concatenate([w_in_l[:, DIFF_WIDTH:2 * DIFF_WIDTH], w_in_l[:, 3 * DIFF_WIDTH:]], axis=1).astype(BF16)
    qt, vt, k, u = _in_proj(x, attn_norm[layer][None], wqv_t, wku, tm=512)

    dtiles = _bias_tiles(rel_bias)
    far_split = _split3_bf16(rel_bias[FAR_BUCKET] * LOG2E)
    lam_params = jnp.stack([lambda_q1[layer], lambda_k1[layer], lambda_q2[layer], lambda_k2[layer]])
    a = _diff_attn(far_split, qt, k, vt, dtiles, lam_params, diff_subln[layer][None])

    wkv = wkv_cross[layer]
    kt, v = _mem_kv(mem, mem_norm[layer][None], wkv[:, :D].T.astype(BF16), wkv[:, D:].astype(BF16))

    wr_hi, wr_lo, rbias = _router_operands(router_group[layer], router_group_bias[layer],
                                           router_expert[layer], router_expert_bias[layer])
    h2c = _mix_cross(a, u, x, w_out[layer].astype(BF16), pool_w[layer].astype(BF16), pool_scale[layer][None],
                     cross_norm[layer][None], wq_cross[layer].astype(BF16), kt, v,
                     wo_cross[layer].astype(BF16), ffn_norm[layer][None], wr_hi, wr_lo, rbias, tm=512)
    h2c = h2c.reshape(B * S, D + LANES)

    moe_tm = 512
    src, tile_glo, tile_ghi = _group_sort_plan(h2c[:, D + ROUTE_GROUP_LANE].astype(jnp.int32), moe_tm)
    y = _moe(h2c, tile_glo, tile_ghi, src, ffn_norm[layer][None], w_gate[layer].astype(BF16),
             w_up[layer].astype(BF16), w_down[layer].astype(BF16), final_norm[None], moe_tm)
    return y.reshape(B, S, D)
```

```python
import functools
import math

import jax
import jax.numpy as jnp
from jax import lax
from jax.experimental import pallas as pl
from jax.experimental.pallas import tpu as pltpu

F32 = jnp.float32
BF16 = jnp.bfloat16

EPS = 1e-6
CHUNK = 64
N_DIFF_HEADS = 4
DIFF_DK = 64
DIFF_DV = 128
DIFF_WIDTH = N_DIFF_HEADS * DIFF_DV
POOL_WINDOWS = (2, 4, 8, 16)
POOL_DIM = 128
POOL_WIDTH = len(POOL_WINDOWS) * POOL_DIM
MAX_WINDOW = max(POOL_WINDOWS)
REL_BUCKETS = 32
REL_MAX_DISTANCE = 128
N_CROSS_HEADS = 4
N_EXPERT_GROUPS = 4
EXPERTS_PER_GROUP = 4
N_EXPERTS = N_EXPERT_GROUPS * EXPERTS_PER_GROUP
LAMBDA_INIT = 0.8 - 0.6 * math.exp(-0.3 * 0)

LANES = 128
SUBLANES = 8
MXU_DIM = 256
VMEM_LIMIT_BYTES = 56 * 1024 * 1024

ATTN_BLOCK = 2 * MXU_DIM
FAR_BUCKET = REL_BUCKETS // 2 - 1
MASK_VALUE = -1e30
M_FINITE = float(jnp.finfo(jnp.float32).max)
LOG2E = math.log2(math.e)
V_AUG_ROWS = DIFF_DV + 16

ROUTER_ROWS = LANES
GROUP_ROW_STRIDE = SUBLANES
ROUTE_GROUP_LANE = GROUP_ROW_STRIDE


def _rms(x, g):
    return x * lax.rsqrt(jnp.mean(x * x, axis=-1, keepdims=True) + EPS) * g


def _dot(a, b):
    return jnp.dot(a, b, preferred_element_type=F32)


def _dot_nt(a, b):
    return lax.dot_general(a, b, (((1,), (1,)), ((), ())), preferred_element_type=F32)


def _in_proj_kernel(x_ref, g_ref, wqv_t_ref, wku_ref, qt_ref, vt_ref, k_ref, u_ref):
    hn = _rms(x_ref[...], g_ref[...]).astype(BF16)
    zt = _dot_nt(wqv_t_ref[...], hn)
    qt_ref[...] = (zt[:DIFF_WIDTH] * (DIFF_DK ** -0.5 * LOG2E)).astype(BF16)
    for c in range(vt_ref.shape[0]):
        vt_ref[c] = zt[DIFF_WIDTH:, c * ATTN_BLOCK:(c + 1) * ATTN_BLOCK].astype(BF16)
    z = _dot(hn, wku_ref[...])
    k_ref[...] = z[:, :DIFF_WIDTH].astype(BF16)
    u_ref[...] = z[:, DIFF_WIDTH:]


def _in_proj(x, attn_norm, wqv_t, wku, tm):
    B, S, D = x.shape
    nkb = tm // ATTN_BLOCK
    return pl.pallas_call(
        _in_proj_kernel,
        grid=(B, S // tm),
        in_specs=[
            pl.BlockSpec((None, tm, D), lambda b, i: (b, i, 0)),
            pl.BlockSpec((1, D), lambda b, i: (0, 0)),
            pl.BlockSpec(wqv_t.shape, lambda b, i: (0, 0)),
            pl.BlockSpec(wku.shape, lambda b, i: (0, 0)),
        ],
        out_specs=[
            pl.BlockSpec((None, DIFF_WIDTH, tm), lambda b, i: (b, 0, i)),
            pl.BlockSpec((None, nkb, DIFF_WIDTH, ATTN_BLOCK), lambda b, i: (b, i, 0, 0)),
            pl.BlockSpec((None, tm, DIFF_WIDTH), lambda b, i: (b, i, 0)),
            pl.BlockSpec((None, tm, POOL_WIDTH), lambda b, i: (b, i, 0)),
        ],
        out_shape=[
            jax.ShapeDtypeStruct((B, DIFF_WIDTH, S), BF16),
            jax.ShapeDtypeStruct((B, S // ATTN_BLOCK, DIFF_WIDTH, ATTN_BLOCK), BF16),
            jax.ShapeDtypeStruct((B, S, DIFF_WIDTH), BF16),
            jax.ShapeDtypeStruct((B, S, POOL_WIDTH), F32),
        ],
        compiler_params=pltpu.CompilerParams(
            dimension_semantics=("arbitrary", "arbitrary"), vmem_limit_bytes=VMEM_LIMIT_BYTES),
        name="in_proj",
    )(x, attn_norm, wqv_t, wku)


def _rel_bucket(rel):
    nb = REL_BUCKETS // 2
    max_exact = nb // 2
    ret = (rel > 0).astype(jnp.int32) * nb
    n = jnp.abs(rel)
    nf = jnp.maximum(n, 1).astype(jnp.float32)
    large = max_exact + (jnp.log(nf / max_exact) / math.log(REL_MAX_DISTANCE / max_exact)
                         * (nb - max_exact)).astype(jnp.int32)
    large = jnp.minimum(large, nb - 1)
    return ret + jnp.where(n < max_exact, n, large)


def _near_bucket_tiles():
    kk = jnp.arange(ATTN_BLOCK, dtype=jnp.int32)[:, None]
    qq = jnp.arange(ATTN_BLOCK, dtype=jnp.int32)[None, :]
    diag = jnp.where(kk // CHUNK <= qq // CHUNK, _rel_bucket(kk - qq), -1)
    prev = _rel_bucket(kk - ATTN_BLOCK - qq)
    return jnp.stack([diag, prev])


def _bias_tile_kernel(rb_ref, bucket_ref, out_ref):
    h = pl.program_id(0)
    bk = bucket_ref[...]
    val = jnp.zeros(bk.shape, F32)
    for b in range(REL_BUCKETS):
        val = jnp.where(bk == b, rb_ref[b, h], val)
    out_ref[...] = jnp.where(bk < 0, MASK_VALUE, (val - rb_ref[FAR_BUCKET, h]) * LOG2E)


def _bias_tiles(rel_bias):
    buckets = _near_bucket_tiles()
    return pl.pallas_call(
        _bias_tile_kernel,
        grid=(N_DIFF_HEADS,),
        in_specs=[
            pl.BlockSpec(memory_space=pltpu.SMEM),
            pl.BlockSpec(buckets.shape, lambda h: (0, 0, 0)),
        ],
        out_specs=pl.BlockSpec((None,) + buckets.shape, lambda h: (h, 0, 0, 0)),
        out_shape=jax.ShapeDtypeStruct((N_DIFF_HEADS,) + buckets.shape, F32),
        name="bias_tiles",
    )(rel_bias, buckets)


def _attn_kernel(far_ref, qt_ref, k_ref, vt_ref, dt_ref, lam_ref, g_ref, o_ref,
                 kaug, vaug, qaug, qref, m_s, acc_s, s_buf, p_buf):
    h = pl.program_id(1)
    i = pl.program_id(2)
    nkb = vt_ref.shape[0]
    QB = ATTN_BLOCK
    KB = ATTN_BLOCK

    @pl.when(i == 0)
    def _():
        lane = lax.broadcasted_iota(jnp.int32, (KB, LANES), 1)
        pad = jnp.where(lane == 0, far_ref[h, 0],
                        jnp.where(lane == 1, far_ref[h, 1],
                                  jnp.where(lane == 2, far_ref[h, 2],
                                            jnp.where(lane < 6, 1.0, 0.0)))).astype(BF16)
        vrow = lax.broadcasted_iota(jnp.int32, (V_AUG_ROWS - DIFF_DV, KB), 0)
        ones_row = jnp.where(vrow == 0, 1.0, 0.0).astype(BF16)

        def fill(c, carry):
            r = pl.multiple_of(c * KB, KB)
            kaug[pl.ds(r, KB), :LANES] = k_ref[pl.ds(r, KB), :]
            kaug[pl.ds(r, KB), LANES:] = pad
            vaug[c, :DIFF_DV] = vt_ref[c]
            vaug[c, DIFF_DV:] = ones_row
            return carry

        lax.fori_loop(0, nkb, fill, 0)

    q = qt_ref[...]
    zeros = jnp.zeros((DIFF_DK, QB), BF16)
    row = lax.broadcasted_iota(jnp.int32, (LANES, QB), 0)
    ones_blk = jnp.where(row < 3, 1.0, 0.0)
    for buf in (qaug, qref):
        buf[0, :DIFF_DK] = q[:DIFF_DK]
        buf[0, DIFF_DK:2 * DIFF_DK] = zeros
        buf[1, :DIFF_DK] = zeros
        buf[1, DIFF_DK:2 * DIFF_DK] = q[DIFF_DK:]
    qaug[0, 2 * DIFF_DK:] = ones_blk.astype(BF16)
    qaug[1, 2 * DIFF_DK:] = ones_blk.astype(BF16)

    rd = pl.multiple_of(i * KB, KB)
    k_own_t = kaug[pl.ds(rd, KB), :LANES].astype(F32).T
    qk = q.astype(F32) * k_own_t
    for half in range(2):
        m = jnp.sum(qk[half * DIFF_DK:(half + 1) * DIFF_DK], axis=0, keepdims=True) + far_ref[h, 3]
        m_s[half] = m
        nm = -m
        hi = nm.astype(BF16).astype(F32)
        mid = (nm - hi).astype(BF16).astype(F32)
        lo = (nm - hi - mid).astype(BF16).astype(F32)
        qref[half, 2 * DIFF_DK:] = jnp.where(
            row == 3, hi, jnp.where(row == 4, mid, jnp.where(row == 5, lo, ones_blk))).astype(BF16)
    acc_s[...] = jnp.zeros(acc_s.shape, F32)

    def stream_block(t):
        return jnp.where(t == 0, i, jnp.where(t == 1, i - 1, t - 2))

    def probabilities(par, j, bias):
        r = pl.multiple_of(j * KB, KB)
        kblk = kaug[pl.ds(r, KB), :]
        for half in range(2):
            s = _dot(kblk, qref[half])
            if bias is not None:
                s = s + bias
            p_buf[par, half] = jnp.exp2(s.astype(BF16))

    def weighted_values(par, j):
        vblk = vaug[j]
        for half in range(2):
            acc_s[half] += _dot(vblk, p_buf[par, half])

    def stream_step(par, t, bias=None):
        weighted_values(1 - par, stream_block(t - 1))
        probabilities(par, stream_block(t), bias)

    def finalize():
        lam = (jnp.exp(jnp.sum(lam_ref[0:1] * lam_ref[1:2], axis=-1, keepdims=True))
               - jnp.exp(jnp.sum(lam_ref[2:3] * lam_ref[3:4], axis=-1, keepdims=True)) + LAMBDA_INIT)
        l0 = acc_s[0, DIFF_DV:DIFF_DV + 1]
        l1 = acc_s[1, DIFF_DV:DIFF_DV + 1]
        ot = acc_s[0, :DIFF_DV] / l0 - lam * (acc_s[1, :DIFF_DV] / l1)
        o = ot.T
        o_ref[...] = (_rms(o, g_ref[...]) * (1.0 - LAMBDA_INIT)).astype(BF16)
        bad = jnp.where(jnp.abs(ot) <= M_FINITE, 0.0, 1.0) + jnp.where(jnp.maximum(l0, l1) <= M_FINITE, 0.0, 1.0)
        return jnp.max(bad) > 0.0

    probabilities(0, i, dt_ref[0])

    @pl.when(i >= 1)
    def _():
        stream_step(1, 1, dt_ref[1])

        def stream_pair(g, carry):
            stream_step(0, 2 * g + 2)
            stream_step(1, 2 * g + 3)
            return carry

        lax.fori_loop(0, (i - 1) // 2, stream_pair, 0)

        @pl.when((i - 1) % 2 == 1)
        def _():
            stream_step(0, i)

    for par in range(2):
        @pl.when(i % 2 == par)
        def _():
            weighted_values(par, stream_block(i))

    overflowed = finalize()

    @pl.when(overflowed)
    def _():
        acc_s[...] = jnp.zeros(acc_s.shape, F32)

        def online_step(j, bias):
            r = pl.multiple_of(j * KB, KB)
            kblk = kaug[pl.ds(r, KB), :]
            vblk = vaug[j]
            for half in range(2):
                s_buf[half] = _dot(kblk, qaug[half])
                if bias is not None:
                    s_buf[half] += bias
                m_old = m_s[half]
                m_new = jnp.maximum(m_old, jnp.max(s_buf[half], axis=0, keepdims=True))
                p = jnp.exp2((s_buf[half] - m_new).astype(BF16))
                acc_s[half] = jnp.exp2(m_old - m_new) * acc_s[half] + _dot(vblk, p)
                m_s[half] = m_new

        online_step(i, dt_ref[0])

        @pl.when(i >= 1)
        def _():
            online_step(i - 1, dt_ref[1])

        def far_online(j, carry):
            online_step(j, None)
            return carry

        lax.fori_loop(0, jnp.maximum(i - 1, 0), far_online, 0)
        finalize()


def _diff_attn(far_split, qt, k, vt, dtiles, lam_params, subln):
    B, S, _ = k.shape
    nq = S // ATTN_BLOCK
    return pl.pallas_call(
        _attn_kernel,
        grid=(B, N_DIFF_HEADS, nq),
        in_specs=[
            pl.BlockSpec(memory_space=pltpu.SMEM),
            pl.BlockSpec((None, 2 * DIFF_DK, ATTN_BLOCK), lambda b, h, i: (b, h, i)),
            pl.BlockSpec((None, S, 2 * DIFF_DK), lambda b, h, i: (b, 0, h)),
            pl.BlockSpec((None, nq, DIFF_DV, ATTN_BLOCK), lambda b, h, i: (b, 0, h, 0)),
            pl.BlockSpec((None, 2, ATTN_BLOCK, ATTN_BLOCK), lambda b, h, i: (h, 0, 0, 0)),
            pl.BlockSpec(lam_params.shape, lambda b, h, i: (0, 0)),
            pl.BlockSpec((1, DIFF_DV), lambda b, h, i: (0, 0)),
        ],
        out_specs=pl.BlockSpec((None, ATTN_BLOCK, DIFF_DV), lambda b, h, i: (b, i, h)),
        out_shape=jax.ShapeDtypeStruct((B, S, DIFF_WIDTH), BF16),
        scratch_shapes=[
            pltpu.VMEM((S, 2 * LANES), BF16),
            pltpu.VMEM((nq, V_AUG_ROWS, ATTN_BLOCK), BF16),
            pltpu.VMEM((2, 2 * LANES, ATTN_BLOCK), BF16),
            pltpu.VMEM((2, 2 * LANES, ATTN_BLOCK), BF16),
            pltpu.VMEM((2, 1, ATTN_BLOCK), F32),
            pltpu.VMEM((2, V_AUG_ROWS, ATTN_BLOCK), F32),
            pltpu.VMEM((2, ATTN_BLOCK, ATTN_BLOCK), F32),
            pltpu.VMEM((2, 2, ATTN_BLOCK, ATTN_BLOCK), BF16),
        ],
        compiler_params=pltpu.CompilerParams(
            dimension_semantics=("arbitrary", "arbitrary", "arbitrary"), vmem_limit_bytes=VMEM_LIMIT_BYTES),
        name="diff_attn",
    )(far_split, qt, k, vt, dtiles, lam_params, subln)


def _mem_kv_kernel(mem_ref, g_ref, wk_t_ref, wv_ref, kt_ref, v_ref):
    mn = _rms(mem_ref[...], g_ref[...]).astype(BF16)
    kt_ref[...] = _dot_nt(wk_t_ref[...], mn).astype(BF16)
    v_ref[...] = _dot(mn, wv_ref[...]).astype(BF16)


def _mem_kv(mem, mem_norm, wk_t, wv):
    B, M, D = mem.shape
    return pl.pallas_call(
        _mem_kv_kernel,
        grid=(B,),
        in_specs=[
            pl.BlockSpec((None, M, D), lambda b: (b, 0, 0)),
            pl.BlockSpec((1, D), lambda b: (0, 0)),
            pl.BlockSpec((D, D), lambda b: (0, 0)),
            pl.BlockSpec((D, D), lambda b: (0, 0)),
        ],
        out_specs=[
            pl.BlockSpec((None, D, M), lambda b: (b, 0, 0)),
            pl.BlockSpec((None, M, D), lambda b: (b, 0, 0)),
        ],
        out_shape=[jax.ShapeDtypeStruct((B, D, M), BF16), jax.ShapeDtypeStruct((B, M, D), BF16)],
        compiler_params=pltpu.CompilerParams(
            dimension_semantics=("arbitrary",), vmem_limit_bytes=VMEM_LIMIT_BYTES),
        name="mem_kv",
    )(mem, mem_norm, wk_t, wv)


def _first_argmax_rows(v, vmax):
    row = lax.broadcasted_iota(jnp.int32, v.shape, 0)
    return jnp.min(jnp.where(v == vmax, row, v.shape[0]), axis=0, keepdims=True)


def _mix_cross_kernel(a_ref, u_ref, uprev_ref, x_ref, wout_ref, pw_ref, ps_ref, cn_ref, wq_ref, kt_ref, v_ref,
                      wo_ref, fn_ref, wrh_ref, wrl_ref, rb_ref, h2c_ref):
    i = pl.program_id(1)
    tm = u_ref.shape[0]
    D = x_ref.shape[1]
    hd = kt_ref.shape[0] // N_CROSS_HEADS

    u = u_ref[...]
    prev = jnp.where(i > 0, uprev_ref[...], 0.0)
    pos1 = (i * tm + 1 + lax.broadcasted_iota(jnp.int32, (tm, 1), 0)).astype(F32)
    ys = []
    for g, w in enumerate(POOL_WINDOWS):
        sl = slice(g * POOL_DIM, (g + 1) * POOL_DIM)
        acc = jnp.concatenate([prev[:, sl], u[:, sl]], axis=0)
        span = 1
        while span < w:
            acc = acc + pltpu.roll(acc, span, axis=0)
            span *= 2
        mean = acc[MAX_WINDOW:] / jnp.minimum(pos1, float(w))
        d = (mean - u[:, sl]).astype(BF16)
        ys.append(_dot(d, pw_ref[g]) * ps_ref[:, sl])
    p = jnp.concatenate(ys, axis=-1).astype(BF16)

    h1 = x_ref[...] + _dot(jnp.concatenate([a_ref[...], p], axis=-1), wout_ref[...])

    q = _dot(_rms(h1, cn_ref[...]).astype(BF16), wq_ref[...]).astype(BF16)
    outs = []
    for c in range(N_CROSS_HEADS):
        sl = slice(c * hd, (c + 1) * hd)
        s = _dot(q[:, sl], kt_ref[sl, :]) * (hd ** -0.5)
        e = jnp.exp(s - jnp.max(s, axis=-1, keepdims=True))
        pr = (e / jnp.sum(e, axis=-1, keepdims=True)).astype(BF16)
        outs.append(_dot(pr, v_ref[:, sl]))
    o = jnp.concatenate(outs, axis=-1).astype(BF16)
    h2 = h1 + _dot(o, wo_ref[...])
    h2c_ref[:, :D] = h2

    hf = _rms(h2, fn_ref[...])
    hf_hi = hf.astype(BF16)
    hf_lo = (hf - hf_hi.astype(F32)).astype(BF16)
    lg = (_dot_nt(wrh_ref[...], hf_hi) + _dot_nt(wrl_ref[...], hf_hi) + _dot_nt(wrh_ref[...], hf_lo)
          + rb_ref[...])
    R = GROUP_ROW_STRIDE
    gl = lg[:R]
    gmax = jnp.max(gl, axis=0, keepdims=True)
    gsel = _first_argmax_rows(gl, gmax)
    g_w = 1.0 / jnp.sum(jnp.exp(gl - gmax), axis=0, keepdims=True)
    el = lg[R:2 * R]
    for g in range(1, N_EXPERT_GROUPS):
        el = jnp.where(gsel == g, lg[R * (g + 1):R * (g + 2)], el)
    row = lax.broadcasted_iota(jnp.int32, el.shape, 0)
    v0 = jnp.max(el, axis=0, keepdims=True)
    i0 = _first_argmax_rows(el, v0)
    el1 = jnp.where(row == i0, -jnp.inf, el)
    v1 = jnp.max(el1, axis=0, keepdims=True)
    i1 = _first_argmax_rows(el1, v1)
    t = jnp.exp(v1 - v0)
    w0 = 1.0 / (1.0 + t)
    w1 = t / (1.0 + t)
    inner = jnp.where(row == i0, w0, 0.0) + jnp.where(row == i1, w1, 0.0)
    blocks = [g_w * inner, jnp.where(row == 0, gsel.astype(F32), 0.0),
              jnp.zeros((ROUTER_ROWS - 2 * R, tm), F32)]
    h2c_ref[:, D:] = jnp.concatenate(blocks, axis=0).T


def _mix_cross(a, u, x, wout, pool_w, pool_scale, cross_norm, wq, kt, v, wo, ffn_norm, wr_hi, wr_lo, rbias, tm):
    B, S, D = x.shape
    M = v.shape[1]
    prev_blocks = tm // MAX_WINDOW
    const2 = lambda b, i: (0, 0)
    return pl.pallas_call(
        _mix_cross_kernel,
        grid=(B, S // tm),
        in_specs=[
            pl.BlockSpec((None, tm, DIFF_WIDTH), lambda b, i: (b, i, 0)),
            pl.BlockSpec((None, tm, POOL_WIDTH), lambda b, i: (b, i, 0)),
            pl.BlockSpec((None, MAX_WINDOW, POOL_WIDTH), lambda b, i: (b, jnp.maximum(i * prev_blocks - 1, 0), 0)),
            pl.BlockSpec((None, tm, D), lambda b, i: (b, i, 0)),
            pl.BlockSpec(wout.shape, const2),
            pl.BlockSpec(pool_w.shape, lambda b, i: (0, 0, 0)),
            pl.BlockSpec((1, POOL_WIDTH), const2),
            pl.BlockSpec((1, D), const2),
            pl.BlockSpec(wq.shape, const2),
            pl.BlockSpec((None, D, M), lambda b, i: (b, 0, 0)),
            pl.BlockSpec((None, M, D), lambda b, i: (b, 0, 0)),
            pl.BlockSpec(wo.shape, const2),
            pl.BlockSpec((1, D), const2),
            pl.BlockSpec(wr_hi.shape, const2),
            pl.BlockSpec(wr_lo.shape, const2),
            pl.BlockSpec(rbias.shape, const2),
        ],
        out_specs=pl.BlockSpec((None, tm, D + LANES), lambda b, i: (b, i, 0)),
        out_shape=jax.ShapeDtypeStruct((B, S, D + LANES), F32),
        compiler_params=pltpu.CompilerParams(
            dimension_semantics=("arbitrary", "arbitrary"), vmem_limit_bytes=VMEM_LIMIT_BYTES),
        name="mix_cross",
    )(a, u, u, x, wout, pool_w, pool_scale, cross_norm, wq, kt, v, wo, ffn_norm, wr_hi, wr_lo, rbias)


def _moe_kernel(glo_ref, ghi_ref, slot_ref, h2c_hbm, wg_hbm, wu_hbm, wd_hbm, fn_ref, wg_ref, wu_ref, wd_ref, gn_ref,
                y_hbm, src_ref, xbuf, ybuf, hf_s, acc_s, wg_x, wu_x, wd_x, gsem, ssem, wsem):
    c = pl.program_id(0)
    ntiles = pl.num_programs(0)
    tm, D = hf_s.shape
    slot = c % 2

    @pl.when(c == 0)
    def _():
        def invert(t, carry):
            src_ref[slot_ref[t]] = t
            return carry
        lax.fori_loop(0, slot_ref.shape[0], invert, 0, unroll=8)

    def row_gather(tile, r):
        t = src_ref[tile * tm + r]
        return pltpu.make_async_copy(h2c_hbm.at[pl.ds(t, 1)], xbuf.at[tile % 2, pl.ds(r, 1)], gsem.at[tile % 2])

    def row_scatter(tile, r):
        t = src_ref[tile * tm + r]
        return pltpu.make_async_copy(ybuf.at[tile % 2, pl.ds(r, 1)], y_hbm.at[pl.ds(t, 1)], ssem.at[tile % 2])

    def start_rows(make, tile):
        def body(r, carry):
            make(tile, r).start()
            return carry
        lax.fori_loop(0, tm, body, 0, unroll=8)

    def wait_gather(s):
        pltpu.make_async_copy(h2c_hbm.at[pl.ds(0, tm)], xbuf.at[s], gsem.at[s]).wait()

    def wait_scatter(s):
        pltpu.make_async_copy(ybuf.at[s], y_hbm.at[pl.ds(0, tm)], ssem.at[s]).wait()

    @pl.when(c == 0)
    def _():
        start_rows(row_gather, 0)

    wait_gather(slot)

    x = xbuf[slot]
    h2 = x[:, :D]
    hf_s[...] = _rms(h2, fn_ref[...]).astype(BF16)
    group = x[:, D + ROUTE_GROUP_LANE:D + ROUTE_GROUP_LANE + 1]

    def expert_pass(g, wg, wu, wd, first, gather_next=False, scatter_prev=False):
        hf = hf_s[...]
        in_group = group == g.astype(F32)
        rows_per_expert = tm // EXPERTS_PER_GROUP
        for e in range(EXPERTS_PER_GROUP):
            hg = _dot(hf, wg[e])
            hu = _dot(hf, wu[e])
            cw = jnp.where(in_group, x[:, D + e:D + e + 1], 0.0)
            act = (hg * (1.0 / (1.0 + jnp.exp(-hg))) * hu * cw).astype(BF16)
            part = _dot(act, wd[e])
            if first and e == 0:
                acc_s[...] = part
            else:
                acc_s[...] += part
            for r in range(e * rows_per_expert, (e + 1) * rows_per_expert):
                if gather_next:
                    row_gather(c + 1, r).start()
                if scatter_prev:
                    row_scatter(c - 1, r).start()

    for has_next, has_prev in ((True, False), (True, True), (False, True), (False, False)):
        @pl.when(((c + 1 < ntiles) == has_next) & ((c >= 1) == has_prev))
        def _():
            expert_pass(glo_ref[c], wg_ref, wu_ref, wd_ref, True, gather_next=has_next, scatter_prev=has_prev)

    def extra_pass(g, carry):
        copies = [pltpu.make_async_copy(w_hbm.at[pl.ds(g * EXPERTS_PER_GROUP, EXPERTS_PER_GROUP)], w_x, wsem.at[n])
                  for n, (w_hbm, w_x) in enumerate(((wg_hbm, wg_x), (wu_hbm, wu_x), (wd_hbm, wd_x)))]
        for cp in copies:
            cp.start()
        for cp in copies:
            cp.wait()
        expert_pass(g, wg_x, wu_x, wd_x, False)
        return carry

    lax.fori_loop(glo_ref[c] + 1, ghi_ref[c] + 1, extra_pass, 0)

    @pl.when(c >= 2)
    def _():
        wait_scatter(slot)

    ybuf[slot] = _rms(h2 + acc_s[...], gn_ref[...])

    @pl.when(c == ntiles - 1)
    def _():
        start_rows(row_scatter, c)
        wait_scatter(slot)

        @pl.when(ntiles >= 2)
        def _():
            wait_scatter(1 - slot)


def _moe(h2c, tile_glo, tile_ghi, slot, ffn_norm, wg, wu, wd, final_norm, tm):
    T, DL = h2c.shape
    D = DL - LANES
    FF = wg.shape[-1]
    E = EXPERTS_PER_GROUP
    any_spec = pl.BlockSpec(memory_space=pl.ANY)
    grid_spec = pltpu.PrefetchScalarGridSpec(
        num_scalar_prefetch=3,
        grid=(T // tm,),
        in_specs=[
            any_spec, any_spec, any_spec, any_spec,
            pl.BlockSpec((1, D), lambda c, glo, ghi, src: (0, 0)),
            pl.BlockSpec((E, D, FF), lambda c, glo, ghi, src: (glo[c], 0, 0)),
            pl.BlockSpec((E, D, FF), lambda c, glo, ghi, src: (glo[c], 0, 0)),
            pl.BlockSpec((E, FF, D), lambda c, glo, ghi, src: (glo[c], 0, 0)),
            pl.BlockSpec((1, D), lambda c, glo, ghi, src: (0, 0)),
        ],
        out_specs=any_spec,
        scratch_shapes=[
            pltpu.SMEM((T,), jnp.int32),
            pltpu.VMEM((2, tm, DL), F32),
            pltpu.VMEM((2, tm, D), F32),
            pltpu.VMEM((tm, D), BF16),
            pltpu.VMEM((tm, D), F32),
            pltpu.VMEM((E, D, FF), BF16),
            pltpu.VMEM((E, D, FF), BF16),
            pltpu.VMEM((E, FF, D), BF16),
            pltpu.SemaphoreType.DMA((2,)),
            pltpu.SemaphoreType.DMA((2,)),
            pltpu.SemaphoreType.DMA((3,)),
        ],
    )
    return pl.pallas_call(
        _moe_kernel,
        grid_spec=grid_spec,
        out_shape=jax.ShapeDtypeStruct((T, D), F32),
        compiler_params=pltpu.CompilerParams(
            dimension_semantics=("arbitrary",), vmem_limit_bytes=VMEM_LIMIT_BYTES),
        name="moe",
    )(tile_glo, tile_ghi, slot, h2c, wg, wu, wd, ffn_norm, wg, wu, wd, final_norm)


def _group_sort_plan(group, tm):
    T = group.shape[0]
    onehot = (group[:, None] == jnp.arange(N_EXPERT_GROUPS, dtype=jnp.int32)[None, :]).astype(jnp.int32)
    ranks = jnp.cumsum(onehot, axis=0) - onehot
    counts = jnp.sum(onehot, axis=0)
    starts = jnp.cumsum(counts) - counts
    slot = jnp.sum(onehot * (ranks + starts[None, :]), axis=1)
    first = jnp.arange(0, T, tm, dtype=jnp.int32)
    group_at = lambda pos: jnp.sum((starts[None, 1:] <= pos[:, None]).astype(jnp.int32), axis=1)
    return slot, group_at(first), group_at(first + (tm - 1))


def _split3_bf16(c):
    hi = c.astype(BF16).astype(F32)
    mid = (c - hi).astype(BF16).astype(F32)
    lo = (c - hi - mid).astype(BF16).astype(F32)
    return jnp.stack([hi, mid, lo], axis=-1)


def _router_operands(router_group, router_group_bias, router_expert, router_expert_bias):
    D = router_group.shape[0]
    R = GROUP_ROW_STRIDE
    w = jnp.zeros((ROUTER_ROWS, D), F32)
    bias = jnp.zeros((ROUTER_ROWS,), F32)
    slab_pad = jnp.full((R - N_EXPERT_GROUPS,), MASK_VALUE, F32)
    w = w.at[:N_EXPERT_GROUPS].set(router_group.T)
    bias = bias.at[:R].set(jnp.concatenate([router_group_bias, slab_pad]))
    for g in range(N_EXPERT_GROUPS):
        w = w.at[R * (g + 1):R * (g + 1) + EXPERTS_PER_GROUP].set(router_expert[g].T)
        bias = bias.at[R * (g + 1):R * (g + 2)].set(jnp.concatenate([router_expert_bias[g], slab_pad]))
    w_hi = w.astype(BF16)
    w_lo = (w - w_hi.astype(F32)).astype(BF16)
    return w_hi, w_lo, bias[:, None]


def kernel(x, mem, rel_bias, attn_norm, w_in, lambda_q1, lambda_k1, lambda_q2, lambda_k2, diff_subln, pool_w,
           pool_scale, w_out, cross_norm, mem_norm, wq_cross, wkv_cross, wo_cross, ffn_norm, router_group,
           router_group_bias, router_expert, router_expert_bias, w_gate, w_up, w_down, final_norm):
    B, S, D = x.shape
    layer = 0
    w_in_l = w_in[layer]
    wqv_t = jnp.concatenate([w_in_l[:, :DIFF_WIDTH], w_in_l[:, 2 * DIFF_WIDTH:3 * DIFF_WIDTH]], axis=1).T.astype(BF16)
    wku = jnp.concatenate([w_in_l[:, DIFF_WIDTH:2 * DIFF_WIDTH], w_in_l[:, 3 * DIFF_WIDTH:]], axis=1).astype(BF16)
    qt, vt, k, u = _in_proj(x, attn_norm[layer][None], wqv_t, wku, tm=512)

    dtiles = _bias_tiles(rel_bias)
    far_split = jnp.concatenate([_split3_bf16(rel_bias[FAR_BUCKET] * LOG2E), rel_bias[0][:, None] * LOG2E],
                                axis=1)
    lam_params = jnp.stack([lambda_q1[layer], lambda_k1[layer], lambda_q2[layer], lambda_k2[layer]])
    a = _diff_attn(far_split, qt, k, vt, dtiles, lam_params, diff_subln[layer][None])

    wkv = wkv_cross[layer]
    kt, v = _mem_kv(mem, mem_norm[layer][None], wkv[:, :D].T.astype(BF16), wkv[:, D:].astype(BF16))

    wr_hi, wr_lo, rbias = _router_operands(router_group[layer], router_group_bias[layer],
                                           router_expert[layer], router_expert_bias[layer])
    h2c = _mix_cross(a, u, x, w_out[layer].astype(BF16), pool_w[layer].astype(BF16), pool_scale[layer][None],
                     cross_norm[layer][None], wq_cross[layer].astype(BF16), kt, v,
                     wo_cross[layer].astype(BF16), ffn_norm[layer][None], wr_hi, wr_lo, rbias, tm=512)
    h2c = h2c.reshape(B * S, D + LANES)

    moe_tm = 512
    slot, tile_glo, tile_ghi = _group_sort_plan(h2c[:, D + ROUTE_GROUP_LANE].astype(jnp.int32), moe_tm)
    y = _moe(h2c, tile_glo, tile_ghi, slot, ffn_norm[layer][None], w_gate[layer].astype(BF16),
             w_up[layer].astype(BF16), w_down[layer].astype(BF16), final_norm[None], moe_tm)
    return y.reshape(B, S, D)
```

```python
import functools
import math

import jax
import jax.numpy as jnp
from jax import lax
from jax.experimental import pallas as pl
from jax.experimental.pallas import tpu as pltpu

F32 = jnp.float32
BF16 = jnp.bfloat16

EPS = 1e-6
CHUNK = 64
N_DIFF_HEADS = 4
DIFF_DK = 64
DIFF_DV = 128
DIFF_WIDTH = N_DIFF_HEADS * DIFF_DV
POOL_WINDOWS = (2, 4, 8, 16)
POOL_DIM = 128
POOL_WIDTH = len(POOL_WINDOWS) * POOL_DIM
MAX_WINDOW = max(POOL_WINDOWS)
REL_BUCKETS = 32
REL_MAX_DISTANCE = 128
N_CROSS_HEADS = 4
N_EXPERT_GROUPS = 4
EXPERTS_PER_GROUP = 4
N_EXPERTS = N_EXPERT_GROUPS * EXPERTS_PER_GROUP
LAMBDA_INIT = 0.8 - 0.6 * math.exp(-0.3 * 0)

LANES = 128
SUBLANES = 8
MXU_DIM = 256
VMEM_LIMIT_BYTES = 56 * 1024 * 1024

ATTN_BLOCK = 2 * MXU_DIM
FAR_BUCKET = REL_BUCKETS // 2 - 1
MASK_VALUE = -1e30
M_FINITE = float(jnp.finfo(jnp.float32).max)
LOG2E = math.log2(math.e)
V_AUG_ROWS = DIFF_DV + 16

ROUTER_ROWS = LANES
GROUP_ROW_STRIDE = SUBLANES
ROUTE_GROUP_LANE = GROUP_ROW_STRIDE


def _rms(x, g):
    return x * lax.rsqrt(jnp.mean(x * x, axis=-1, keepdims=True) + EPS) * g


def _dot(a, b):
    return jnp.dot(a, b, preferred_element_type=F32)


def _dot_nt(a, b):
    return lax.dot_general(a, b, (((1,), (1,)), ((), ())), preferred_element_type=F32)


def _in_proj_kernel(x_ref, g_ref, wqv_t_ref, wku_ref, qt_ref, vt_ref, k_ref, u_ref):
    hn = _rms(x_ref[...], g_ref[...]).astype(BF16)
    zt = _dot_nt(wqv_t_ref[...], hn)
    qt_ref[...] = (zt[:DIFF_WIDTH] * (DIFF_DK ** -0.5 * LOG2E)).astype(BF16)
    for c in range(vt_ref.shape[0]):
        vt_ref[c] = zt[DIFF_WIDTH:, c * ATTN_BLOCK:(c + 1) * ATTN_BLOCK].astype(BF16)
    z = _dot(hn, wku_ref[...])
    k_ref[...] = z[:, :DIFF_WIDTH].astype(BF16)
    u_ref[...] = z[:, DIFF_WIDTH:]


def _in_proj(x, attn_norm, wqv_t, wku, tm):
    B, S, D = x.shape
    nkb = tm // ATTN_BLOCK
    return pl.pallas_call(
        _in_proj_kernel,
        grid=(B, S // tm),
        in_specs=[
            pl.BlockSpec((None, tm, D), lambda b, i: (b, i, 0)),
            pl.BlockSpec((1, D), lambda b, i: (0, 0)),
            pl.BlockSpec(wqv_t.shape, lambda b, i: (0, 0)),
            pl.BlockSpec(wku.shape, lambda b, i: (0, 0)),
        ],
        out_specs=[
            pl.BlockSpec((None, DIFF_WIDTH, tm), lambda b, i: (b, 0, i)),
            pl.BlockSpec((None, nkb, DIFF_WIDTH, ATTN_BLOCK), lambda b, i: (b, i, 0, 0)),
            pl.BlockSpec((None, tm, DIFF_WIDTH), lambda b, i: (b, i, 0)),
            pl.BlockSpec((None, tm, POOL_WIDTH), lambda b, i: (b, i, 0)),
        ],
        out_shape=[
            jax.ShapeDtypeStruct((B, DIFF_WIDTH, S), BF16),
            jax.ShapeDtypeStruct((B, S // ATTN_BLOCK, DIFF_WIDTH, ATTN_BLOCK), BF16),
            jax.ShapeDtypeStruct((B, S, DIFF_WIDTH), BF16),
            jax.ShapeDtypeStruct((B, S, POOL_WIDTH), F32),
        ],
        compiler_params=pltpu.CompilerParams(
            dimension_semantics=("arbitrary", "arbitrary"), vmem_limit_bytes=VMEM_LIMIT_BYTES),
        name="in_proj",
    )(x, attn_norm, wqv_t, wku)


def _rel_bucket(rel):
    nb = REL_BUCKETS // 2
    max_exact = nb // 2
    ret = (rel > 0).astype(jnp.int32) * nb
    n = jnp.abs(rel)
    nf = jnp.maximum(n, 1).astype(jnp.float32)
    large = max_exact + (jnp.log(nf / max_exact) / math.log(REL_MAX_DISTANCE / max_exact)
                         * (nb - max_exact)).astype(jnp.int32)
    large = jnp.minimum(large, nb - 1)
    return ret + jnp.where(n < max_exact, n, large)


def _near_bucket_tiles():
    kk = jnp.arange(ATTN_BLOCK, dtype=jnp.int32)[:, None]
    qq = jnp.arange(ATTN_BLOCK, dtype=jnp.int32)[None, :]
    diag = jnp.where(kk // CHUNK <= qq // CHUNK, _rel_bucket(kk - qq), -1)
    prev = _rel_bucket(kk - ATTN_BLOCK - qq)
    return jnp.stack([diag, prev])


def _bias_tile_kernel(rb_ref, bucket_ref, out_ref):
    h = pl.program_id(0)
    bk = bucket_ref[...]
    val = jnp.zeros(bk.shape, F32)
    for b in range(REL_BUCKETS):
        val = jnp.where(bk == b, rb_ref[b, h], val)
    out_ref[...] = jnp.where(bk < 0, MASK_VALUE, (val - rb_ref[FAR_BUCKET, h]) * LOG2E)


def _bias_tiles(rel_bias):
    buckets = _near_bucket_tiles()
    return pl.pallas_call(
        _bias_tile_kernel,
        grid=(N_DIFF_HEADS,),
        in_specs=[
            pl.BlockSpec(memory_space=pltpu.SMEM),
            pl.BlockSpec(buckets.shape, lambda h: (0, 0, 0)),
        ],
        out_specs=pl.BlockSpec((None,) + buckets.shape, lambda h: (h, 0, 0, 0)),
        out_shape=jax.ShapeDtypeStruct((N_DIFF_HEADS,) + buckets.shape, F32),
        name="bias_tiles",
    )(rel_bias, buckets)


def _attn_kernel(far_ref, qt_ref, k_ref, vt_ref, dt_ref, lam_ref, g_ref, o_ref,
                 kaug, vaug, qaug, qref, m_s, acc_s, s_buf, p_buf):
    h = pl.program_id(1)
    i = pl.program_id(2)
    nkb = vt_ref.shape[0]
    QB = ATTN_BLOCK
    KB = ATTN_BLOCK

    @pl.when(i == 0)
    def _():
        lane = lax.broadcasted_iota(jnp.int32, (KB, LANES), 1)
        pad = jnp.where(lane == 0, far_ref[h, 0],
                        jnp.where(lane == 1, far_ref[h, 1],
                                  jnp.where(lane == 2, far_ref[h, 2],
                                            jnp.where(lane < 6, 1.0, 0.0)))).astype(BF16)
        vrow = lax.broadcasted_iota(jnp.int32, (V_AUG_ROWS - DIFF_DV, KB), 0)
        ones_row = jnp.where(vrow == 0, 1.0, 0.0).astype(BF16)

        def fill(c, carry):
            r = pl.multiple_of(c * KB, KB)
            kaug[pl.ds(r, KB), :LANES] = k_ref[pl.ds(r, KB), :]
            kaug[pl.ds(r, KB), LANES:] = pad
            vaug[c, :DIFF_DV] = vt_ref[c]
            vaug[c, DIFF_DV:] = ones_row
            return carry

        lax.fori_loop(0, nkb, fill, 0)

    q = qt_ref[...]
    zeros = jnp.zeros((DIFF_DK, QB), BF16)
    row = lax.broadcasted_iota(jnp.int32, (LANES, QB), 0)
    ones_blk = jnp.where(row < 3, 1.0, 0.0)
    for buf in (qaug, qref):
        buf[0, :DIFF_DK] = q[:DIFF_DK]
        buf[0, DIFF_DK:2 * DIFF_DK] = zeros
        buf[1, :DIFF_DK] = zeros
        buf[1, DIFF_DK:2 * DIFF_DK] = q[DIFF_DK:]
    qaug[0, 2 * DIFF_DK:] = ones_blk.astype(BF16)
    qaug[1, 2 * DIFF_DK:] = ones_blk.astype(BF16)

    rd = pl.multiple_of(i * KB, KB)
    k_own_t = kaug[pl.ds(rd, KB), :LANES].astype(F32).T
    qk = q.astype(F32) * k_own_t
    for half in range(2):
        m = jnp.sum(qk[half * DIFF_DK:(half + 1) * DIFF_DK], axis=0, keepdims=True) + far_ref[h, 3]
        m_s[half] = m
        nm = -m
        hi = nm.astype(BF16).astype(F32)
        mid = (nm - hi).astype(BF16).astype(F32)
        lo = (nm - hi - mid).astype(BF16).astype(F32)
        qref[half, 2 * DIFF_DK:] = jnp.where(
            row == 3, hi, jnp.where(row == 4, mid, jnp.where(row == 5, lo, ones_blk))).astype(BF16)
    acc_s[...] = jnp.zeros(acc_s.shape, F32)

    def stream_block(t):
        return jnp.where(t == 0, i, jnp.where(t == 1, i - 1, t - 2))

    def probabilities(par, j, bias):
        r = pl.multiple_of(j * KB, KB)
        kblk = kaug[pl.ds(r, KB), :]
        for half in range(2):
            s = _dot(kblk, qref[half])
            if bias is not None:
                s = s + bias
            p_buf[par, half] = jnp.exp2(s.astype(BF16))

    def weighted_values(par, j):
        vblk = vaug[j]
        for half in range(2):
            acc_s[half] += _dot(vblk, p_buf[par, half])

    def stream_step(par, t, bias=None):
        weighted_values(1 - par, stream_block(t - 1))
        probabilities(par, stream_block(t), bias)

    def finalize():
        lam = (jnp.exp(jnp.sum(lam_ref[0:1] * lam_ref[1:2], axis=-1, keepdims=True))
               - jnp.exp(jnp.sum(lam_ref[2:3] * lam_ref[3:4], axis=-1, keepdims=True)) + LAMBDA_INIT)
        l0 = acc_s[0, DIFF_DV:DIFF_DV + 1]
        l1 = acc_s[1, DIFF_DV:DIFF_DV + 1]
        ot = acc_s[0, :DIFF_DV] / l0 - lam * (acc_s[1, :DIFF_DV] / l1)
        o = ot.T
        o_ref[...] = (_rms(o, g_ref[...]) * (1.0 - LAMBDA_INIT)).astype(BF16)
        bad = jnp.where(jnp.abs(ot) <= M_FINITE, 0.0, 1.0) + jnp.where(jnp.maximum(l0, l1) <= M_FINITE, 0.0, 1.0)
        return jnp.max(bad) > 0.0

    probabilities(0, i, dt_ref[0])

    @pl.when(i >= 1)
    def _():
        stream_step(1, 1, dt_ref[1])

        def stream_pair(g, carry):
            stream_step(0, 2 * g + 2)
            stream_step(1, 2 * g + 3)
            return carry

        lax.fori_loop(0, (i - 1) // 2, stream_pair, 0)

        @pl.when((i - 1) % 2 == 1)
        def _():
            stream_step(0, i)

    for par in range(2):
        @pl.when(i % 2 == par)
        def _():
            weighted_values(par, stream_block(i))

    overflowed = finalize()

    @pl.when(overflowed)
    def _():
        acc_s[...] = jnp.zeros(acc_s.shape, F32)

        def online_step(j, bias):
            r = pl.multiple_of(j * KB, KB)
            kblk = kaug[pl.ds(r, KB), :]
            vblk = vaug[j]
            for half in range(2):
                s_buf[half] = _dot(kblk, qaug[half])
                if bias is not None:
                    s_buf[half] += bias
                m_old = m_s[half]
                m_new = jnp.maximum(m_old, jnp.max(s_buf[half], axis=0, keepdims=True))
                p = jnp.exp2((s_buf[half] - m_new).astype(BF16))
                acc_s[half] = jnp.exp2(m_old - m_new) * acc_s[half] + _dot(vblk, p)
                m_s[half] = m_new

        online_step(i, dt_ref[0])

        @pl.when(i >= 1)
        def _():
            online_step(i - 1, dt_ref[1])

        def far_online(j, carry):
            online_step(j, None)
            return carry

        lax.fori_loop(0, jnp.maximum(i - 1, 0), far_online, 0)
        finalize()


def _diff_attn(far_split, qt, k, vt, dtiles, lam_params, subln):
    B, S, _ = k.shape
    nq = S // ATTN_BLOCK
    return pl.pallas_call(
        _attn_kernel,
        grid=(B, N_DIFF_HEADS, nq),
        in_specs=[
            pl.BlockSpec(memory_space=pltpu.SMEM),
            pl.BlockSpec((None, 2 * DIFF_DK, ATTN_BLOCK), lambda b, h, i: (b, h, i)),
            pl.BlockSpec((None, S, 2 * DIFF_DK), lambda b, h, i: (b, 0, h)),
            pl.BlockSpec((None, nq, DIFF_DV, ATTN_BLOCK), lambda b, h, i: (b, 0, h, 0)),
            pl.BlockSpec((None, 2, ATTN_BLOCK, ATTN_BLOCK), lambda b, h, i: (h, 0, 0, 0)),
            pl.BlockSpec(lam_params.shape, lambda b, h, i: (0, 0)),
            pl.BlockSpec((1, DIFF_DV), lambda b, h, i: (0, 0)),
        ],
        out_specs=pl.BlockSpec((None, ATTN_BLOCK, DIFF_DV), lambda b, h, i: (b, i, h)),
        out_shape=jax.ShapeDtypeStruct((B, S, DIFF_WIDTH), BF16),
        scratch_shapes=[
            pltpu.VMEM((S, 2 * LANES), BF16),
            pltpu.VMEM((nq, V_AUG_ROWS, ATTN_BLOCK), BF16),
            pltpu.VMEM((2, 2 * LANES, ATTN_BLOCK), BF16),
            pltpu.VMEM((2, 2 * LANES, ATTN_BLOCK), BF16),
            pltpu.VMEM((2, 1, ATTN_BLOCK), F32),
            pltpu.VMEM((2, V_AUG_ROWS, ATTN_BLOCK), F32),
            pltpu.VMEM((2, ATTN_BLOCK, ATTN_BLOCK), F32),
            pltpu.VMEM((2, 2, ATTN_BLOCK, ATTN_BLOCK), BF16),
        ],
        compiler_params=pltpu.CompilerParams(
            dimension_semantics=("arbitrary", "arbitrary", "arbitrary"), vmem_limit_bytes=VMEM_LIMIT_BYTES),
        name="diff_attn",
    )(far_split, qt, k, vt, dtiles, lam_params, subln)


def _mem_kv_kernel(mem_ref, g_ref, wk_t_ref, wv_ref, kt_ref, v_ref):
    mn = _rms(mem_ref[...], g_ref[...]).astype(BF16)
    kt_ref[...] = _dot_nt(wk_t_ref[...], mn).astype(BF16)
    v_ref[...] = _dot(mn, wv_ref[...]).astype(BF16)


def _mem_kv(mem, mem_norm, wk_t, wv):
    B, M, D = mem.shape
    return pl.pallas_call(
        _mem_kv_kernel,
        grid=(B,),
        in_specs=[
            pl.BlockSpec((None, M, D), lambda b: (b, 0, 0)),
            pl.BlockSpec((1, D), lambda b: (0, 0)),
            pl.BlockSpec((D, D), lambda b: (0, 0)),
            pl.BlockSpec((D, D), lambda b: (0, 0)),
        ],
        out_specs=[
            pl.BlockSpec((None, D, M), lambda b: (b, 0, 0)),
            pl.BlockSpec((None, M, D), lambda b: (b, 0, 0)),
        ],
        out_shape=[jax.ShapeDtypeStruct((B, D, M), BF16), jax.ShapeDtypeStruct((B, M, D), BF16)],
        compiler_params=pltpu.CompilerParams(
            dimension_semantics=("arbitrary",), vmem_limit_bytes=VMEM_LIMIT_BYTES),
        name="mem_kv",
    )(mem, mem_norm, wk_t, wv)


def _first_argmax_rows(v, vmax):
    row = lax.broadcasted_iota(jnp.int32, v.shape, 0)
    return jnp.min(jnp.where(v == vmax, row, v.shape[0]), axis=0, keepdims=True)


def _mix_cross_kernel(a_ref, u_ref, uprev_ref, x_ref, wout_ref, pw_ref, ps_ref, cn_ref, wq_ref, kt_ref, v_ref,
                      wo_ref, fn_ref, wrh_ref, wrl_ref, rb_ref, h2c_ref):
    i = pl.program_id(1)
    tm = u_ref.shape[0]
    D = x_ref.shape[1]
    hd = kt_ref.shape[0] // N_CROSS_HEADS

    u = u_ref[...]
    prev = jnp.where(i > 0, uprev_ref[...], 0.0)
    pos1 = (i * tm + 1 + lax.broadcasted_iota(jnp.int32, (tm, 1), 0)).astype(F32)
    ys = []
    for g, w in enumerate(POOL_WINDOWS):
        sl = slice(g * POOL_DIM, (g + 1) * POOL_DIM)
        acc = jnp.concatenate([prev[:, sl], u[:, sl]], axis=0)
        span = 1
        while span < w:
            acc = acc + pltpu.roll(acc, span, axis=0)
            span *= 2
        mean = acc[MAX_WINDOW:] / jnp.minimum(pos1, float(w))
        d = (mean - u[:, sl]).astype(BF16)
        ys.append(_dot(d, pw_ref[g]) * ps_ref[:, sl])
    p = jnp.concatenate(ys, axis=-1).astype(BF16)

    h1 = x_ref[...] + _dot(jnp.concatenate([a_ref[...], p], axis=-1), wout_ref[...])

    q = _dot(_rms(h1, cn_ref[...]).astype(BF16), wq_ref[...]).astype(BF16)
    outs = []
    for c in range(N_CROSS_HEADS):
        sl = slice(c * hd, (c + 1) * hd)
        s = _dot(q[:, sl], kt_ref[sl, :]) * (hd ** -0.5)
        e = jnp.exp(s - jnp.max(s, axis=-1, keepdims=True))
        pr = (e / jnp.sum(e, axis=-1, keepdims=True)).astype(BF16)
        outs.append(_dot(pr, v_ref[:, sl]))
    o = jnp.concatenate(outs, axis=-1).astype(BF16)
    h2 = h1 + _dot(o, wo_ref[...])
    h2c_ref[:, :D] = h2

    hf = _rms(h2, fn_ref[...])
    hf_hi = hf.astype(BF16)
    hf_lo = (hf - hf_hi.astype(F32)).astype(BF16)
    lg = (_dot_nt(wrh_ref[...], hf_hi) + _dot_nt(wrl_ref[...], hf_hi) + _dot_nt(wrh_ref[...], hf_lo)
          + rb_ref[...])
    R = GROUP_ROW_STRIDE
    gl = lg[:R]
    gmax = jnp.max(gl, axis=0, keepdims=True)
    gsel = _first_argmax_rows(gl, gmax)
    g_w = 1.0 / jnp.sum(jnp.exp(gl - gmax), axis=0, keepdims=True)
    el = lg[R:2 * R]
    for g in range(1, N_EXPERT_GROUPS):
        el = jnp.where(gsel == g, lg[R * (g + 1):R * (g + 2)], el)
    row = lax.broadcasted_iota(jnp.int32, el.shape, 0)
    v0 = jnp.max(el, axis=0, keepdims=True)
    i0 = _first_argmax_rows(el, v0)
    el1 = jnp.where(row == i0, -jnp.inf, el)
    v1 = jnp.max(el1, axis=0, keepdims=True)
    i1 = _first_argmax_rows(el1, v1)
    t = jnp.exp(v1 - v0)
    w0 = 1.0 / (1.0 + t)
    w1 = t / (1.0 + t)
    inner = jnp.where(row == i0, w0, 0.0) + jnp.where(row == i1, w1, 0.0)
    blocks = [g_w * inner, jnp.where(row == 0, gsel.astype(F32), 0.0),
              jnp.zeros((ROUTER_ROWS - 2 * R, tm), F32)]
    h2c_ref[:, D:] = jnp.concatenate(blocks, axis=0).T


def _mix_cross(a, u, x, wout, pool_w, pool_scale, cross_norm, wq, kt, v, wo, ffn_norm, wr_hi, wr_lo, rbias, tm):
    B, S, D = x.shape
    M = v.shape[1]
    prev_blocks = tm // MAX_WINDOW
    const2 = lambda b, i: (0, 0)
    return pl.pallas_call(
        _mix_cross_kernel,
        grid=(B, S // tm),
        in_specs=[
            pl.BlockSpec((None, tm, DIFF_WIDTH), lambda b, i: (b, i, 0)),
            pl.BlockSpec((None, tm, POOL_WIDTH), lambda b, i: (b, i, 0)),
            pl.BlockSpec((None, MAX_WINDOW, POOL_WIDTH), lambda b, i: (b, jnp.maximum(i * prev_blocks - 1, 0), 0)),
            pl.BlockSpec((None, tm, D), lambda b, i: (b, i, 0)),
            pl.BlockSpec(wout.shape, const2),
            pl.BlockSpec(pool_w.shape, lambda b, i: (0, 0, 0)),
            pl.BlockSpec((1, POOL_WIDTH), const2),
            pl.BlockSpec((1, D), const2),
            pl.BlockSpec(wq.shape, const2),
            pl.BlockSpec((None, D, M), lambda b, i: (b, 0, 0)),
            pl.BlockSpec((None, M, D), lambda b, i: (b, 0, 0)),
            pl.BlockSpec(wo.shape, const2),
            pl.BlockSpec((1, D), const2),
            pl.BlockSpec(wr_hi.shape, const2),
            pl.BlockSpec(wr_lo.shape, const2),
            pl.BlockSpec(rbias.shape, const2),
        ],
        out_specs=pl.BlockSpec((None, tm, D + LANES), lambda b, i: (b, i, 0)),
        out_shape=jax.ShapeDtypeStruct((B, S, D + LANES), F32),
        compiler_params=pltpu.CompilerParams(
            dimension_semantics=("arbitrary", "arbitrary"), vmem_limit_bytes=VMEM_LIMIT_BYTES),
        name="mix_cross",
    )(a, u, u, x, wout, pool_w, pool_scale, cross_norm, wq, kt, v, wo, ffn_norm, wr_hi, wr_lo, rbias)


def _moe_kernel(glo_ref, ghi_ref, slot_ref, h2c_hbm, wg_hbm, wu_hbm, wd_hbm, fn_ref, wg_ref, wu_ref, wd_ref, gn_ref,
                y_hbm, src_ref, xbuf, ybuf, hf_s, acc_s, wg_x, wu_x, wd_x, gsem, ssem, wsem, fence_sem):
    c = pl.program_id(0)
    ntiles = pl.num_programs(0)
    tm, D = hf_s.shape
    slot = c % 2

    @pl.when(c == 0)
    def _():
        def invert(t, carry):
            src_ref[slot_ref[t]] = t
            return carry
        lax.fori_loop(0, slot_ref.shape[0], invert, 0, unroll=8)

    def row_gather(tile, r, s):
        t = src_ref[tile * tm + r]
        return pltpu.make_async_copy(h2c_hbm.at[pl.ds(t, 1)], xbuf.at[s, pl.ds(r, 1)], gsem.at[s])

    def row_scatter(tile, r, s):
        t = src_ref[tile * tm + r]
        return pltpu.make_async_copy(ybuf.at[s, pl.ds(r, 1)], y_hbm.at[pl.ds(t, 1)], ssem.at[s])

    def start_rows(make, tile):
        def body(r, carry):
            make(tile, r, tile % 2).start()
            return carry
        lax.fori_loop(0, tm, body, 0, unroll=8)

    def wait_gather(s):
        pltpu.make_async_copy(h2c_hbm.at[pl.ds(0, tm)], xbuf.at[s], gsem.at[s]).wait()

    def wait_scatter(s):
        pltpu.make_async_copy(ybuf.at[s], y_hbm.at[pl.ds(0, tm)], ssem.at[s]).wait()

    @pl.when(c == 0)
    def _():
        start_rows(row_gather, 0)

    wait_gather(slot)

    x = xbuf[slot]
    h2 = x[:, :D]
    hf_s[...] = _rms(h2, fn_ref[...]).astype(BF16)
    group = x[:, D + ROUTE_GROUP_LANE:D + ROUTE_GROUP_LANE + 1]

    def expert_pass(g, wg, wu, wd, first, neighbour_slot=None):
        hf = hf_s[...]
        in_group = group == g.astype(F32)
        n_segments = EXPERTS_PER_GROUP
        rows_per_segment = tm // n_segments
        experts_per_segment = EXPERTS_PER_GROUP // n_segments
        for e in range(EXPERTS_PER_GROUP):
            seg, first_in_seg = divmod(e, experts_per_segment)
            if neighbour_slot is not None and first_in_seg == 0:
                for r in range(seg * rows_per_segment, (seg + 1) * rows_per_segment):
                    row_gather(c + 1, r, neighbour_slot).start()
                    row_scatter(c - 1, r, neighbour_slot).start()
            hg = _dot(hf, wg[e])
            hu = _dot(hf, wu[e])
            cw = jnp.where(in_group, x[:, D + e:D + e + 1], 0.0)
            act = (hg * (1.0 / (1.0 + jnp.exp(-hg))) * hu * cw).astype(BF16)
            part = _dot(act, wd[e])
            if first and e == 0:
                acc_s[...] = part
            else:
                acc_s[...] += part
            if neighbour_slot is not None and first_in_seg == experts_per_segment - 1:
                pl.semaphore_signal(fence_sem, 1)
                pl.semaphore_wait(fence_sem, 1)

    interior = (c >= 1) & (c + 1 < ntiles)
    for par in range(2):
        @pl.when(interior & (slot == par))
        def _():
            expert_pass(glo_ref[c], wg_ref, wu_ref, wd_ref, True, neighbour_slot=1 - par)

    @pl.when(jnp.logical_not(interior))
    def _():
        @pl.when(c + 1 < ntiles)
        def _():
            start_rows(row_gather, c + 1)

        @pl.when(c >= 1)
        def _():
            start_rows(row_scatter, c - 1)

        expert_pass(glo_ref[c], wg_ref, wu_ref, wd_ref, True)

    def extra_pass(g, carry):
        copies = [pltpu.make_async_copy(w_hbm.at[pl.ds(g * EXPERTS_PER_GROUP, EXPERTS_PER_GROUP)], w_x, wsem.at[n])
                  for n, (w_hbm, w_x) in enumerate(((wg_hbm, wg_x), (wu_hbm, wu_x), (wd_hbm, wd_x)))]
        for cp in copies:
            cp.start()
        for cp in copies:
            cp.wait()
        expert_pass(g, wg_x, wu_x, wd_x, False)
        return carry

    lax.fori_loop(glo_ref[c] + 1, ghi_ref[c] + 1, extra_pass, 0)

    @pl.when(c >= 2)
    def _():
        wait_scatter(slot)

    ybuf[slot] = _rms(h2 + acc_s[...], gn_ref[...])

    @pl.when(c == ntiles - 1)
    def _():
        start_rows(row_scatter, c)
        wait_scatter(slot)

        @pl.when(ntiles >= 2)
        def _():
            wait_scatter(1 - slot)


def _moe(h2c, tile_glo, tile_ghi, slot, ffn_norm, wg, wu, wd, final_norm, tm):
    T, DL = h2c.shape
    D = DL - LANES
    FF = wg.shape[-1]
    E = EXPERTS_PER_GROUP
    any_spec = pl.BlockSpec(memory_space=pl.ANY)
    grid_spec = pltpu.PrefetchScalarGridSpec(
        num_scalar_prefetch=3,
        grid=(T // tm,),
        in_specs=[
            any_spec, any_spec, any_spec, any_spec,
            pl.BlockSpec((1, D), lambda c, glo, ghi, src: (0, 0)),
            pl.BlockSpec((E, D, FF), lambda c, glo, ghi, src: (glo[c], 0, 0)),
            pl.BlockSpec((E, D, FF), lambda c, glo, ghi, src: (glo[c], 0, 0)),
            pl.BlockSpec((E, FF, D), lambda c, glo, ghi, src: (glo[c], 0, 0)),
            pl.BlockSpec((1, D), lambda c, glo, ghi, src: (0, 0)),
        ],
        out_specs=any_spec,
        scratch_shapes=[
            pltpu.SMEM((T,), jnp.int32),
            pltpu.VMEM((2, tm, DL), F32),
            pltpu.VMEM((2, tm, D), F32),
            pltpu.VMEM((tm, D), BF16),
            pltpu.VMEM((tm, D), F32),
            pltpu.VMEM((E, D, FF), BF16),
            pltpu.VMEM((E, D, FF), BF16),
            pltpu.VMEM((E, FF, D), BF16),
            pltpu.SemaphoreType.DMA((2,)),
            pltpu.SemaphoreType.DMA((2,)),
            pltpu.SemaphoreType.DMA((3,)),
            pltpu.SemaphoreType.REGULAR,
        ],
    )
    return pl.pallas_call(
        _moe_kernel,
        grid_spec=grid_spec,
        out_shape=jax.ShapeDtypeStruct((T, D), F32),
        compiler_params=pltpu.CompilerParams(
            dimension_semantics=("arbitrary",), vmem_limit_bytes=VMEM_LIMIT_BYTES),
        name="moe",
    )(tile_glo, tile_ghi, slot, h2c, wg, wu, wd, ffn_norm, wg, wu, wd, final_norm)


def _group_sort_plan(group, tm):
    T = group.shape[0]
    onehot = (group[:, None] == jnp.arange(N_EXPERT_GROUPS, dtype=jnp.int32)[None, :]).astype(jnp.int32)
    ranks = jnp.cumsum(onehot, axis=0) - onehot
    counts = jnp.sum(onehot, axis=0)
    starts = jnp.cumsum(counts) - counts
    slot = jnp.sum(onehot * (ranks + starts[None, :]), axis=1)
    first = jnp.arange(0, T, tm, dtype=jnp.int32)
    group_at = lambda pos: jnp.sum((starts[None, 1:] <= pos[:, None]).astype(jnp.int32), axis=1)
    return slot, group_at(first), group_at(first + (tm - 1))


def _split3_bf16(c):
    hi = c.astype(BF16).astype(F32)
    mid = (c - hi).astype(BF16).astype(F32)
    lo = (c - hi - mid).astype(BF16).astype(F32)
    return jnp.stack([hi, mid, lo], axis=-1)


def _router_operands(router_group, router_group_bias, router_expert, router_expert_bias):
    D = router_group.shape[0]
    R = GROUP_ROW_STRIDE
    w = jnp.zeros((ROUTER_ROWS, D), F32)
    bias = jnp.zeros((ROUTER_ROWS,), F32)
    slab_pad = jnp.full((R - N_EXPERT_GROUPS,), MASK_VALUE, F32)
    w = w.at[:N_EXPERT_GROUPS].set(router_group.T)
    bias = bias.at[:R].set(jnp.concatenate([router_group_bias, slab_pad]))
    for g in range(N_EXPERT_GROUPS):
        w = w.at[R * (g + 1):R * (g + 1) + EXPERTS_PER_GROUP].set(router_expert[g].T)
        bias = bias.at[R * (g + 1):R * (g + 2)].set(jnp.concatenate([router_expert_bias[g], slab_pad]))
    w_hi = w.astype(BF16)
    w_lo = (w - w_hi.astype(F32)).astype(BF16)
    return w_hi, w_lo, bias[:, None]


def kernel(x, mem, rel_bias, attn_norm, w_in, lambda_q1, lambda_k1, lambda_q2, lambda_k2, diff_subln, pool_w,
           pool_scale, w_out, cross_norm, mem_norm, wq_cross, wkv_cross, wo_cross, ffn_norm, router_group,
           router_group_bias, router_expert, router_expert_bias, w_gate, w_up, w_down, final_norm):
    B, S, D = x.shape
    layer = 0
    w_in_l = w_in[layer]
    wqv_t = jnp.concatenate([w_in_l[:, :DIFF_WIDTH], w_in_l[:, 2 * DIFF_WIDTH:3 * DIFF_WIDTH]], axis=1).T.astype(BF16)
    wku = jnp.concatenate([w_in_l[:, DIFF_WIDTH:2 * DIFF_WIDTH], w_in_l[:, 3 * DIFF_WIDTH:]], axis=1).astype(BF16)
    qt, vt, k, u = _in_proj(x, attn_norm[layer][None], wqv_t, wku, tm=512)

    dtiles = _bias_tiles(rel_bias)
    far_split = jnp.concatenate([_split3_bf16(rel_bias[FAR_BUCKET] * LOG2E), rel_bias[0][:, None] * LOG2E],
                                axis=1)
    lam_params = jnp.stack([lambda_q1[layer], lambda_k1[layer], lambda_q2[layer], lambda_k2[layer]])
    a = _diff_attn(far_split, qt, k, vt, dtiles, lam_params, diff_subln[layer][None])

    wkv = wkv_cross[layer]
    kt, v = _mem_kv(mem, mem_norm[layer][None], wkv[:, :D].T.astype(BF16), wkv[:, D:].astype(BF16))

    wr_hi, wr_lo, rbias = _router_operands(router_group[layer], router_group_bias[layer],
                                           router_expert[layer], router_expert_bias[layer])
    h2c = _mix_cross(a, u, x, w_out[layer].astype(BF16), pool_w[layer].astype(BF16), pool_scale[layer][None],
                     cross_norm[layer][None], wq_cross[layer].astype(BF16), kt, v,
                     wo_cross[layer].astype(BF16), ffn_norm[layer][None], wr_hi, wr_lo, rbias, tm=512)
    h2c = h2c.reshape(B * S, D + LANES)

    moe_tm = 512
    slot, tile_glo, tile_ghi = _group_sort_plan(h2c[:, D + ROUTE_GROUP_LANE].astype(jnp.int32), moe_tm)
    y = _moe(h2c, tile_glo, tile_ghi, slot, ffn_norm[layer][None], w_gate[layer].astype(BF16),
             w_up[layer].astype(BF16), w_down[layer].astype(BF16), final_norm[None], moe_tm)
    return y.reshape(B, S, D)
```

```python
import functools
import math

import jax
import jax.numpy as jnp
from jax import lax
from jax.experimental import pallas as pl
from jax.experimental.pallas import tpu as pltpu

F32 = jnp.float32
BF16 = jnp.bfloat16

EPS = 1e-6
CHUNK = 64
N_DIFF_HEADS = 4
DIFF_DK = 64
DIFF_DV = 128
DIFF_WIDTH = N_DIFF_HEADS * DIFF_DV
POOL_WINDOWS = (2, 4, 8, 16)
POOL_DIM = 128
POOL_WIDTH = len(POOL_WINDOWS) * POOL_DIM
MAX_WINDOW = max(POOL_WINDOWS)
REL_BUCKETS = 32
REL_MAX_DISTANCE = 128
N_CROSS_HEADS = 4
N_EXPERT_GROUPS = 4
EXPERTS_PER_GROUP = 4
N_EXPERTS = N_EXPERT_GROUPS * EXPERTS_PER_GROUP
LAMBDA_INIT = 0.8 - 0.6 * math.exp(-0.3 * 0)

LANES = 128
SUBLANES = 8
MXU_DIM = 256
VMEM_LIMIT_BYTES = 56 * 1024 * 1024

ATTN_BLOCK = 2 * MXU_DIM
FAR_BUCKET = REL_BUCKETS // 2 - 1
MASK_VALUE = -1e30
M_FINITE = float(jnp.finfo(jnp.float32).max)
LOG2E = math.log2(math.e)
V_AUG_ROWS = DIFF_DV + 16

ROUTER_ROWS = LANES
GROUP_ROW_STRIDE = SUBLANES
ROUTE_GROUP_LANE = GROUP_ROW_STRIDE


def _rms(x, g):
    return x * lax.rsqrt(jnp.mean(x * x, axis=-1, keepdims=True) + EPS) * g


def _dot(a, b):
    return jnp.dot(a, b, preferred_element_type=F32)


def _dot_nt(a, b):
    return lax.dot_general(a, b, (((1,), (1,)), ((), ())), preferred_element_type=F32)


def _in_proj_kernel(x_ref, g_ref, wqv_t_ref, wku_ref, qt_ref, vt_ref, k_ref, u_ref):
    hn = _rms(x_ref[...], g_ref[...]).astype(BF16)
    zt = _dot_nt(wqv_t_ref[...], hn)
    qt_ref[...] = (zt[:DIFF_WIDTH] * (DIFF_DK ** -0.5 * LOG2E)).astype(BF16)
    for c in range(vt_ref.shape[0]):
        vt_ref[c] = zt[DIFF_WIDTH:, c * ATTN_BLOCK:(c + 1) * ATTN_BLOCK].astype(BF16)
    z = _dot(hn, wku_ref[...])
    k_ref[...] = z[:, :DIFF_WIDTH].astype(BF16)
    u_ref[...] = z[:, DIFF_WIDTH:]


def _in_proj(x, attn_norm, wqv_t, wku, tm):
    B, S, D = x.shape
    nkb = tm // ATTN_BLOCK
    return pl.pallas_call(
        _in_proj_kernel,
        grid=(B, S // tm),
        in_specs=[
            pl.BlockSpec((None, tm, D), lambda b, i: (b, i, 0)),
            pl.BlockSpec((1, D), lambda b, i: (0, 0)),
            pl.BlockSpec(wqv_t.shape, lambda b, i: (0, 0)),
            pl.BlockSpec(wku.shape, lambda b, i: (0, 0)),
        ],
        out_specs=[
            pl.BlockSpec((None, DIFF_WIDTH, tm), lambda b, i: (b, 0, i)),
            pl.BlockSpec((None, nkb, DIFF_WIDTH, ATTN_BLOCK), lambda b, i: (b, i, 0, 0)),
            pl.BlockSpec((None, tm, DIFF_WIDTH), lambda b, i: (b, i, 0)),
            pl.BlockSpec((None, tm, POOL_WIDTH), lambda b, i: (b, i, 0)),
        ],
        out_shape=[
            jax.ShapeDtypeStruct((B, DIFF_WIDTH, S), BF16),
            jax.ShapeDtypeStruct((B, S // ATTN_BLOCK, DIFF_WIDTH, ATTN_BLOCK), BF16),
            jax.ShapeDtypeStruct((B, S, DIFF_WIDTH), BF16),
            jax.ShapeDtypeStruct((B, S, POOL_WIDTH), F32),
        ],
        compiler_params=pltpu.CompilerParams(
            dimension_semantics=("arbitrary", "arbitrary"), vmem_limit_bytes=VMEM_LIMIT_BYTES),
        name="in_proj",
    )(x, attn_norm, wqv_t, wku)


def _rel_bucket(rel):
    nb = REL_BUCKETS // 2
    max_exact = nb // 2
    ret = (rel > 0).astype(jnp.int32) * nb
    n = jnp.abs(rel)
    nf = jnp.maximum(n, 1).astype(jnp.float32)
    large = max_exact + (jnp.log(nf / max_exact) / math.log(REL_MAX_DISTANCE / max_exact)
                         * (nb - max_exact)).astype(jnp.int32)
    large = jnp.minimum(large, nb - 1)
    return ret + jnp.where(n < max_exact, n, large)


def _bias_tile_kernel(band_ref, out_ref):
    n = ATTN_BLOCK
    kk = lax.broadcasted_iota(jnp.int32, (n, n), 0)
    qq = lax.broadcasted_iota(jnp.int32, (n, n), 1)
    for d in range(2):
        band = jnp.broadcast_to(band_ref[d], (n, 2 * n))
        tile = pltpu.roll(band, n + 1, axis=1, stride=1, stride_axis=0)[:, :n]
        if d == 0:
            tile = jnp.where(kk // CHUNK <= qq // CHUNK, tile, MASK_VALUE)
        out_ref[d] = tile


def _bias_tiles(rel_bias):
    n = ATTN_BLOCK
    m = jnp.arange(2 * n, dtype=jnp.int32)
    rel = jnp.stack([n - 1 - m, -1 - m])
    band = (rel_bias[_rel_bucket(rel)] - rel_bias[FAR_BUCKET]) * LOG2E
    band = band.transpose(2, 0, 1)[:, :, None, :]
    return pl.pallas_call(
        _bias_tile_kernel,
        grid=(N_DIFF_HEADS,),
        in_specs=[pl.BlockSpec((None, 2, 1, 2 * n), lambda h: (h, 0, 0, 0))],
        out_specs=pl.BlockSpec((None, 2, n, n), lambda h: (h, 0, 0, 0)),
        out_shape=jax.ShapeDtypeStruct((N_DIFF_HEADS, 2, n, n), F32),
        name="bias_tiles",
    )(band)


def _attn_kernel(far_ref, qt_ref, k_ref, vt_ref, dt_ref, lam_ref, g_ref, o_ref,
                 kaug, vaug, qaug, qref, m_s, acc_s, s_buf, p_buf):
    h = pl.program_id(1)
    i = pl.program_id(2)
    nkb = vt_ref.shape[0]
    QB = ATTN_BLOCK
    KB = ATTN_BLOCK

    @pl.when(i == 0)
    def _():
        lane = lax.broadcasted_iota(jnp.int32, (KB, LANES), 1)
        pad = jnp.where(lane == 0, far_ref[h, 0],
                        jnp.where(lane == 1, far_ref[h, 1],
                                  jnp.where(lane == 2, far_ref[h, 2],
                                            jnp.where(lane < 6, 1.0, 0.0)))).astype(BF16)
        vrow = lax.broadcasted_iota(jnp.int32, (V_AUG_ROWS - DIFF_DV, KB), 0)
        ones_row = jnp.where(vrow == 0, 1.0, 0.0).astype(BF16)

        def fill(c, carry):
            r = pl.multiple_of(c * KB, KB)
            kaug[pl.ds(r, KB), :LANES] = k_ref[pl.ds(r, KB), :]
            kaug[pl.ds(r, KB), LANES:] = pad
            vaug[c, :DIFF_DV] = vt_ref[c]
            vaug[c, DIFF_DV:] = ones_row
            return carry

        lax.fori_loop(0, nkb, fill, 0)

    q = qt_ref[...]
    zeros = jnp.zeros((DIFF_DK, QB), BF16)
    row = lax.broadcasted_iota(jnp.int32, (LANES, QB), 0)
    ones_blk = jnp.where(row < 3, 1.0, 0.0)
    def fill_query_rows(buf):
        buf[0, :DIFF_DK] = q[:DIFF_DK]
        buf[0, DIFF_DK:2 * DIFF_DK] = zeros
        buf[1, :DIFF_DK] = zeros
        buf[1, DIFF_DK:2 * DIFF_DK] = q[DIFF_DK:]

    fill_query_rows(qref)

    rd = pl.multiple_of(i * KB, KB)
    k_own_t = kaug[pl.ds(rd, KB), :LANES].astype(F32).T
    qk = q.astype(F32) * k_own_t
    for half in range(2):
        m = jnp.sum(qk[half * DIFF_DK:(half + 1) * DIFF_DK], axis=0, keepdims=True) + far_ref[h, 3]
        m_s[half] = m
        nm = -m
        hi = nm.astype(BF16).astype(F32)
        mid = (nm - hi).astype(BF16).astype(F32)
        lo = (nm - hi - mid).astype(BF16).astype(F32)
        qref[half, 2 * DIFF_DK:] = jnp.where(
            row == 3, hi, jnp.where(row == 4, mid, jnp.where(row == 5, lo, ones_blk))).astype(BF16)
    acc_s[...] = jnp.zeros(acc_s.shape, F32)

    def stream_block(t):
        return jnp.where(t == 0, i, jnp.where(t == 1, i - 1, t - 2))

    def probabilities(par, j, bias):
        r = pl.multiple_of(j * KB, KB)
        kblk = kaug[pl.ds(r, KB), :]
        for half in range(2):
            s = _dot(kblk, qref[half])
            if bias is not None:
                s = s + bias
            p_buf[par, half] = jnp.exp2(s.astype(BF16))

    def weighted_values(par, j):
        vblk = vaug[j]
        for half in range(2):
            acc_s[half] += _dot(vblk, p_buf[par, half])

    def stream_step(par, t, bias=None):
        weighted_values(1 - par, stream_block(t - 1))
        probabilities(par, stream_block(t), bias)

    def finalize():
        lam = (jnp.exp(jnp.sum(lam_ref[0:1] * lam_ref[1:2], axis=-1, keepdims=True))
               - jnp.exp(jnp.sum(lam_ref[2:3] * lam_ref[3:4], axis=-1, keepdims=True)) + LAMBDA_INIT)
        l0 = acc_s[0, DIFF_DV:DIFF_DV + 1]
        l1 = acc_s[1, DIFF_DV:DIFF_DV + 1]
        ot = acc_s[0, :DIFF_DV] / l0 - lam * (acc_s[1, :DIFF_DV] / l1)
        inv = lax.rsqrt(jnp.mean(ot * ot, axis=0, keepdims=True) + EPS)
        o_ref[...] = (((ot * inv) * g_ref[...]) * (1.0 - LAMBDA_INIT)).T.astype(BF16)
        bad = jnp.where(jnp.abs(ot) <= M_FINITE, 0.0, 1.0) + jnp.where(jnp.maximum(l0, l1) <= M_FINITE, 0.0, 1.0)
        return jnp.max(bad) > 0.0

    probabilities(0, i, dt_ref[0])

    @pl.when(i >= 1)
    def _():
        stream_step(1, 1, dt_ref[1])

        def stream_pair(g, carry):
            stream_step(0, 2 * g + 2)
            stream_step(1, 2 * g + 3)
            return carry

        lax.fori_loop(0, (i - 1) // 2, stream_pair, 0)

        @pl.when((i - 1) % 2 == 1)
        def _():
            stream_step(0, i)

    for par in range(2):
        @pl.when(i % 2 == par)
        def _():
            weighted_values(par, stream_block(i))

    overflowed = finalize()

    @pl.when(overflowed)
    def _():
        fill_query_rows(qaug)
        qaug[0, 2 * DIFF_DK:] = ones_blk.astype(BF16)
        qaug[1, 2 * DIFF_DK:] = ones_blk.astype(BF16)
        acc_s[...] = jnp.zeros(acc_s.shape, F32)

        def online_step(j, bias):
            r = pl.multiple_of(j * KB, KB)
            kblk = kaug[pl.ds(r, KB), :]
            vblk = vaug[j]
            for half in range(2):
                s_buf[half] = _dot(kblk, qaug[half])
                if bias is not None:
                    s_buf[half] += bias
                m_old = m_s[half]
                m_new = jnp.maximum(m_old, jnp.max(s_buf[half], axis=0, keepdims=True))
                p = jnp.exp2((s_buf[half] - m_new).astype(BF16))
                acc_s[half] = jnp.exp2(m_old - m_new) * acc_s[half] + _dot(vblk, p)
                m_s[half] = m_new

        online_step(i, dt_ref[0])

        @pl.when(i >= 1)
        def _():
            online_step(i - 1, dt_ref[1])

        def far_online(j, carry):
            online_step(j, None)
            return carry

        lax.fori_loop(0, jnp.maximum(i - 1, 0), far_online, 0)
        finalize()


def _diff_attn(far_split, qt, k, vt, dtiles, lam_params, subln):
    B, S, _ = k.shape
    nq = S // ATTN_BLOCK
    return pl.pallas_call(
        _attn_kernel,
        grid=(B, N_DIFF_HEADS, nq),
        in_specs=[
            pl.BlockSpec(memory_space=pltpu.SMEM),
            pl.BlockSpec((None, 2 * DIFF_DK, ATTN_BLOCK), lambda b, h, i: (b, h, i)),
            pl.BlockSpec((None, S, 2 * DIFF_DK), lambda b, h, i: (b, 0, h)),
            pl.BlockSpec((None, nq, DIFF_DV, ATTN_BLOCK), lambda b, h, i: (b, 0, h, 0)),
            pl.BlockSpec((None, 2, ATTN_BLOCK, ATTN_BLOCK), lambda b, h, i: (h, 0, 0, 0)),
            pl.BlockSpec(lam_params.shape, lambda b, h, i: (0, 0)),
            pl.BlockSpec((DIFF_DV, ATTN_BLOCK), lambda b, h, i: (0, 0)),
        ],
        out_specs=pl.BlockSpec((None, ATTN_BLOCK, DIFF_DV), lambda b, h, i: (b, i, h)),
        out_shape=jax.ShapeDtypeStruct((B, S, DIFF_WIDTH), BF16),
        scratch_shapes=[
            pltpu.VMEM((S, 2 * LANES), BF16),
            pltpu.VMEM((nq, V_AUG_ROWS, ATTN_BLOCK), BF16),
            pltpu.VMEM((2, 2 * LANES, ATTN_BLOCK), BF16),
            pltpu.VMEM((2, 2 * LANES, ATTN_BLOCK), BF16),
            pltpu.VMEM((2, 1, ATTN_BLOCK), F32),
            pltpu.VMEM((2, V_AUG_ROWS, ATTN_BLOCK), F32),
            pltpu.VMEM((2, ATTN_BLOCK, ATTN_BLOCK), F32),
            pltpu.VMEM((2, 2, ATTN_BLOCK, ATTN_BLOCK), BF16),
        ],
        compiler_params=pltpu.CompilerParams(
            dimension_semantics=("arbitrary", "arbitrary", "arbitrary"), vmem_limit_bytes=VMEM_LIMIT_BYTES),
        name="diff_attn",
    )(far_split, qt, k, vt, dtiles, lam_params, subln)


def _mem_kv_kernel(mem_ref, g_ref, wk_t_ref, wv_ref, kt_ref, v_ref):
    mn = _rms(mem_ref[...], g_ref[...]).astype(BF16)
    kt_ref[...] = _dot_nt(wk_t_ref[...], mn).astype(BF16)
    v_ref[...] = _dot(mn, wv_ref[...]).astype(BF16)


def _mem_kv(mem, mem_norm, wk_t, wv):
    B, M, D = mem.shape
    return pl.pallas_call(
        _mem_kv_kernel,
        grid=(B,),
        in_specs=[
            pl.BlockSpec((None, M, D), lambda b: (b, 0, 0)),
            pl.BlockSpec((1, D), lambda b: (0, 0)),
            pl.BlockSpec((D, D), lambda b: (0, 0)),
            pl.BlockSpec((D, D), lambda b: (0, 0)),
        ],
        out_specs=[
            pl.BlockSpec((None, D, M), lambda b: (b, 0, 0)),
            pl.BlockSpec((None, M, D), lambda b: (b, 0, 0)),
        ],
        out_shape=[jax.ShapeDtypeStruct((B, D, M), BF16), jax.ShapeDtypeStruct((B, M, D), BF16)],
        compiler_params=pltpu.CompilerParams(
            dimension_semantics=("arbitrary",), vmem_limit_bytes=VMEM_LIMIT_BYTES),
        name="mem_kv",
    )(mem, mem_norm, wk_t, wv)


def _first_argmax_rows(v, vmax):
    row = lax.broadcasted_iota(jnp.int32, v.shape, 0)
    return jnp.min(jnp.where(v == vmax, row, v.shape[0]), axis=0, keepdims=True)


def _mix_cross_kernel(a_ref, u_ref, uprev_ref, x_ref, wout_ref, pw_ref, ps_ref, cn_ref, wq_ref, kt_ref, v_ref,
                      wo_ref, fn_ref, wrh_ref, wrl_ref, rb_ref, h2c_ref):
    i = pl.program_id(1)
    tm = u_ref.shape[0]
    D = x_ref.shape[1]
    hd = kt_ref.shape[0] // N_CROSS_HEADS

    u = u_ref[...]
    prev = jnp.where(i > 0, uprev_ref[...], 0.0)
    pos1 = (i * tm + 1 + lax.broadcasted_iota(jnp.int32, (tm, 1), 0)).astype(F32)
    ys = []
    for g, w in enumerate(POOL_WINDOWS):
        sl = slice(g * POOL_DIM, (g + 1) * POOL_DIM)
        acc = jnp.concatenate([prev[:, sl], u[:, sl]], axis=0)
        span = 1
        while span < w:
            acc = acc[span:] + acc[:-span]
            span *= 2
        first = MAX_WINDOW - (w - 1)
        mean = acc[first:first + tm] / jnp.minimum(pos1, float(w))
        d = (mean - u[:, sl]).astype(BF16)
        ys.append(_dot(d, pw_ref[g]) * ps_ref[:, sl])
    p = jnp.concatenate(ys, axis=-1).astype(BF16)

    h1 = x_ref[...] + _dot(jnp.concatenate([a_ref[...], p], axis=-1), wout_ref[...])

    q = _dot(_rms(h1, cn_ref[...]).astype(BF16), wq_ref[...]).astype(BF16)
    outs = []
    for c in range(N_CROSS_HEADS):
        sl = slice(c * hd, (c + 1) * hd)
        s = _dot(q[:, sl], kt_ref[sl, :]) * (hd ** -0.5)
        e = jnp.exp(s - jnp.max(s, axis=-1, keepdims=True))
        pr = (e / jnp.sum(e, axis=-1, keepdims=True)).astype(BF16)
        outs.append(_dot(pr, v_ref[:, sl]))
    o = jnp.concatenate(outs, axis=-1).astype(BF16)
    h2 = h1 + _dot(o, wo_ref[...])
    h2c_ref[:, :D] = h2

    hf = _rms(h2, fn_ref[...])
    hf_hi = hf.astype(BF16)
    hf_lo = (hf - hf_hi.astype(F32)).astype(BF16)
    lg = (_dot_nt(wrh_ref[...], hf_hi) + _dot_nt(wrl_ref[...], hf_hi) + _dot_nt(wrh_ref[...], hf_lo)
          + rb_ref[...])
    R = GROUP_ROW_STRIDE
    gl = lg[:R]
    gmax = jnp.max(gl, axis=0, keepdims=True)
    gsel = _first_argmax_rows(gl, gmax)
    g_w = 1.0 / jnp.sum(jnp.exp(gl - gmax), axis=0, keepdims=True)
    el = lg[R:2 * R]
    for g in range(1, N_EXPERT_GROUPS):
        el = jnp.where(gsel == g, lg[R * (g + 1):R * (g + 2)], el)
    row = lax.broadcasted_iota(jnp.int32, el.shape, 0)
    v0 = jnp.max(el, axis=0, keepdims=True)
    i0 = _first_argmax_rows(el, v0)
    el1 = jnp.where(row == i0, -jnp.inf, el)
    v1 = jnp.max(el1, axis=0, keepdims=True)
    i1 = _first_argmax_rows(el1, v1)
    t = jnp.exp(v1 - v0)
    w0 = 1.0 / (1.0 + t)
    w1 = t / (1.0 + t)
    inner = jnp.where(row == i0, w0, 0.0) + jnp.where(row == i1, w1, 0.0)
    blocks = [g_w * inner, jnp.where(row == 0, gsel.astype(F32), 0.0),
              jnp.zeros((ROUTER_ROWS - 2 * R, tm), F32)]
    h2c_ref[:, D:] = jnp.concatenate(blocks, axis=0).T


def _mix_cross(a, u, x, wout, pool_w, pool_scale, cross_norm, wq, kt, v, wo, ffn_norm, wr_hi, wr_lo, rbias, tm):
    B, S, D = x.shape
    M = v.shape[1]
    prev_blocks = tm // MAX_WINDOW
    const2 = lambda b, i: (0, 0)
    return pl.pallas_call(
        _mix_cross_kernel,
        grid=(B, S // tm),
        in_specs=[
            pl.BlockSpec((None, tm, DIFF_WIDTH), lambda b, i: (b, i, 0)),
            pl.BlockSpec((None, tm, POOL_WIDTH), lambda b, i: (b, i, 0)),
            pl.BlockSpec((None, MAX_WINDOW, POOL_WIDTH), lambda b, i: (b, jnp.maximum(i * prev_blocks - 1, 0), 0)),
            pl.BlockSpec((None, tm, D), lambda b, i: (b, i, 0)),
            pl.BlockSpec(wout.shape, const2),
            pl.BlockSpec(pool_w.shape, lambda b, i: (0, 0, 0)),
            pl.BlockSpec((1, POOL_WIDTH), const2),
            pl.BlockSpec((1, D), const2),
            pl.BlockSpec(wq.shape, const2),
            pl.BlockSpec((None, D, M), lambda b, i: (b, 0, 0)),
            pl.BlockSpec((None, M, D), lambda b, i: (b, 0, 0)),
            pl.BlockSpec(wo.shape, const2),
            pl.BlockSpec((1, D), const2),
            pl.BlockSpec(wr_hi.shape, const2),
            pl.BlockSpec(wr_lo.shape, const2),
            pl.BlockSpec(rbias.shape, const2),
        ],
        out_specs=pl.BlockSpec((None, tm, D + LANES), lambda b, i: (b, i, 0)),
        out_shape=jax.ShapeDtypeStruct((B, S, D + LANES), F32),
        compiler_params=pltpu.CompilerParams(
            dimension_semantics=("arbitrary", "arbitrary"), vmem_limit_bytes=VMEM_LIMIT_BYTES),
        name="mix_cross",
    )(a, u, u, x, wout, pool_w, pool_scale, cross_norm, wq, kt, v, wo, ffn_norm, wr_hi, wr_lo, rbias)


def _moe_kernel(glo_ref, ghi_ref, slot_ref, h2c_hbm, wg_hbm, wu_hbm, wd_hbm, fn_ref, wg_ref, wu_ref, wd_ref, gn_ref,
                y_hbm, src_ref, xbuf, ybuf, hf_s, acc_s, wg_x, wu_x, wd_x, gsem, ssem, wsem, fence_sem):
    c = pl.program_id(0)
    ntiles = pl.num_programs(0)
    tm, D = hf_s.shape
    slot = c % 2

    @pl.when(c == 0)
    def _():
        def invert(t, carry):
            src_ref[slot_ref[t]] = t
            return carry
        lax.fori_loop(0, slot_ref.shape[0], invert, 0, unroll=8)

    def row_gather(tile, r, s):
        t = src_ref[tile * tm + r]
        return pltpu.make_async_copy(h2c_hbm.at[pl.ds(t, 1)], xbuf.at[s, pl.ds(r, 1)], gsem.at[s])

    def row_scatter(tile, r, s):
        t = src_ref[tile * tm + r]
        return pltpu.make_async_copy(ybuf.at[s, pl.ds(r, 1)], y_hbm.at[pl.ds(t, 1)], ssem.at[s])

    def start_rows(make, tile):
        def body(r, carry):
            make(tile, r, tile % 2).start()
            return carry
        lax.fori_loop(0, tm, body, 0, unroll=8)

    def wait_gather(s):
        pltpu.make_async_copy(h2c_hbm.at[pl.ds(0, tm)], xbuf.at[s], gsem.at[s]).wait()

    def wait_scatter(s):
        pltpu.make_async_copy(ybuf.at[s], y_hbm.at[pl.ds(0, tm)], ssem.at[s]).wait()

    @pl.when(c == 0)
    def _():
        start_rows(row_gather, 0)

    wait_gather(slot)

    x = xbuf[slot]
    h2 = x[:, :D]
    hf_s[...] = _rms(h2, fn_ref[...]).astype(BF16)
    group = x[:, D + ROUTE_GROUP_LANE:D + ROUTE_GROUP_LANE + 1]

    def expert_pass(g, wg, wu, wd, first, neighbour_slot=None):
        hf = hf_s[...]
        in_group = group == g.astype(F32)
        n_segments = EXPERTS_PER_GROUP
        rows_per_segment = tm // n_segments
        experts_per_segment = EXPERTS_PER_GROUP // n_segments
        for e in range(EXPERTS_PER_GROUP):
            seg, first_in_seg = divmod(e, experts_per_segment)
            if neighbour_slot is not None and first_in_seg == 0:
                for r in range(seg * rows_per_segment, (seg + 1) * rows_per_segment):
                    row_gather(c + 1, r, neighbour_slot).start(priority=r % 2)
                    row_scatter(c - 1, r, neighbour_slot).start(priority=(r + 1) % 2)
            hg = _dot(hf, wg[e].astype(BF16))
            hu = _dot(hf, wu[e].astype(BF16))
            cw = jnp.where(in_group, x[:, D + e:D + e + 1], 0.0)
            act = (hg * (1.0 / (1.0 + jnp.exp(-hg))) * hu * cw).astype(BF16)
            part = _dot(act, wd[e].astype(BF16))
            if first and e == 0:
                acc_s[...] = part
            else:
                acc_s[...] += part
            if neighbour_slot is not None and first_in_seg == experts_per_segment - 1:
                pl.semaphore_signal(fence_sem, 1)
                pl.semaphore_wait(fence_sem, 1)

    interior = (c >= 1) & (c + 1 < ntiles)
    for par in range(2):
        @pl.when(interior & (slot == par))
        def _():
            expert_pass(glo_ref[c], wg_ref, wu_ref, wd_ref, True, neighbour_slot=1 - par)

    @pl.when(jnp.logical_not(interior))
    def _():
        @pl.when(c + 1 < ntiles)
        def _():
            start_rows(row_gather, c + 1)

        @pl.when(c >= 1)
        def _():
            start_rows(row_scatter, c - 1)

        expert_pass(glo_ref[c], wg_ref, wu_ref, wd_ref, True)

    def extra_pass(g, carry):
        copies = [pltpu.make_async_copy(w_hbm.at[pl.ds(g * EXPERTS_PER_GROUP, EXPERTS_PER_GROUP)], w_x, wsem.at[n])
                  for n, (w_hbm, w_x) in enumerate(((wg_hbm, wg_x), (wu_hbm, wu_x), (wd_hbm, wd_x)))]
        for cp in copies:
            cp.start()
        for cp in copies:
            cp.wait()
        expert_pass(g, wg_x, wu_x, wd_x, False)
        return carry

    lax.fori_loop(glo_ref[c] + 1, ghi_ref[c] + 1, extra_pass, 0)

    @pl.when(c >= 2)
    def _():
        wait_scatter(slot)

    ybuf[slot] = _rms(h2 + acc_s[...], gn_ref[...])

    @pl.when(c == ntiles - 1)
    def _():
        start_rows(row_scatter, c)
        wait_scatter(slot)

        @pl.when(ntiles >= 2)
        def _():
            wait_scatter(1 - slot)


def _moe(h2c, tile_glo, tile_ghi, slot, ffn_norm, wg, wu, wd, final_norm, tm):
    T, DL = h2c.shape
    D = DL - LANES
    FF = wg.shape[-1]
    E = EXPERTS_PER_GROUP
    any_spec = pl.BlockSpec(memory_space=pl.ANY)
    grid_spec = pltpu.PrefetchScalarGridSpec(
        num_scalar_prefetch=3,
        grid=(T // tm,),
        in_specs=[
            any_spec, any_spec, any_spec, any_spec,
            pl.BlockSpec((1, D), lambda c, glo, ghi, src: (0, 0)),
            pl.BlockSpec((E, D, FF), lambda c, glo, ghi, src: (glo[c], 0, 0)),
            pl.BlockSpec((E, D, FF), lambda c, glo, ghi, src: (glo[c], 0, 0)),
            pl.BlockSpec((E, FF, D), lambda c, glo, ghi, src: (glo[c], 0, 0)),
            pl.BlockSpec((1, D), lambda c, glo, ghi, src: (0, 0)),
        ],
        out_specs=any_spec,
        scratch_shapes=[
            pltpu.SMEM((T,), jnp.int32),
            pltpu.VMEM((2, tm, DL), F32),
            pltpu.VMEM((2, tm, D), F32),
            pltpu.VMEM((tm, D), BF16),
            pltpu.VMEM((tm, D), F32),
            pltpu.VMEM((E, D, FF), wg.dtype),
            pltpu.VMEM((E, D, FF), wu.dtype),
            pltpu.VMEM((E, FF, D), wd.dtype),
            pltpu.SemaphoreType.DMA((2,)),
            pltpu.SemaphoreType.DMA((2,)),
            pltpu.SemaphoreType.DMA((3,)),
            pltpu.SemaphoreType.REGULAR,
        ],
    )
    return pl.pallas_call(
        _moe_kernel,
        grid_spec=grid_spec,
        out_shape=jax.ShapeDtypeStruct((T, D), F32),
        compiler_params=pltpu.CompilerParams(
            dimension_semantics=("arbitrary",), vmem_limit_bytes=VMEM_LIMIT_BYTES),
        name="moe",
    )(tile_glo, tile_ghi, slot, h2c, wg, wu, wd, ffn_norm, wg, wu, wd, final_norm)


def _group_sort_plan(group, tm):
    T = group.shape[0]
    onehot = (group[:, None] == jnp.arange(N_EXPERT_GROUPS, dtype=jnp.int32)[None, :]).astype(jnp.int32)
    ranks = jnp.cumsum(onehot, axis=0) - onehot
    counts = jnp.sum(onehot, axis=0)
    starts = jnp.cumsum(counts) - counts
    slot = jnp.sum(onehot * (ranks + starts[None, :]), axis=1)
    first = jnp.arange(0, T, tm, dtype=jnp.int32)
    group_at = lambda pos: jnp.sum((starts[None, 1:] <= pos[:, None]).astype(jnp.int32), axis=1)
    return slot, group_at(first), group_at(first + (tm - 1))


def _split3_bf16(c):
    hi = c.astype(BF16).astype(F32)
    mid = (c - hi).astype(BF16).astype(F32)
    lo = (c - hi - mid).astype(BF16).astype(F32)
    return jnp.stack([hi, mid, lo], axis=-1)


def _router_operands(router_group, router_group_bias, router_expert, router_expert_bias):
    D = router_group.shape[0]
    R = GROUP_ROW_STRIDE
    w = jnp.zeros((ROUTER_ROWS, D), F32)
    bias = jnp.zeros((ROUTER_ROWS,), F32)
    slab_pad = jnp.full((R - N_EXPERT_GROUPS,), MASK_VALUE, F32)
    w = w.at[:N_EXPERT_GROUPS].set(router_group.T)
    bias = bias.at[:R].set(jnp.concatenate([router_group_bias, slab_pad]))
    for g in range(N_EXPERT_GROUPS):
        w = w.at[R * (g + 1):R * (g + 1) + EXPERTS_PER_GROUP].set(router_expert[g].T)
        bias = bias.at[R * (g + 1):R * (g + 2)].set(jnp.concatenate([router_expert_bias[g], slab_pad]))
    w_hi = w.astype(BF16)
    w_lo = (w - w_hi.astype(F32)).astype(BF16)
    return w_hi, w_lo, bias[:, None]


def kernel(x, mem, rel_bias, attn_norm, w_in, lambda_q1, lambda_k1, lambda_q2, lambda_k2, diff_subln, pool_w,
           pool_scale, w_out, cross_norm, mem_norm, wq_cross, wkv_cross, wo_cross, ffn_norm, router_group,
           router_group_bias, router_expert, router_expert_bias, w_gate, w_up, w_down, final_norm):
    B, S, D = x.shape
    layer = 0
    w_in_l = w_in[layer]
    wqv_t = jnp.concatenate([w_in_l[:, :DIFF_WIDTH], w_in_l[:, 2 * DIFF_WIDTH:3 * DIFF_WIDTH]], axis=1).T.astype(BF16)
    wku = jnp.concatenate([w_in_l[:, DIFF_WIDTH:2 * DIFF_WIDTH], w_in_l[:, 3 * DIFF_WIDTH:]], axis=1).astype(BF16)
    qt, vt, k, u = _in_proj(x, attn_norm[layer][None], wqv_t, wku, tm=512)

    dtiles = _bias_tiles(rel_bias)
    far_split = jnp.concatenate([_split3_bf16(rel_bias[FAR_BUCKET] * LOG2E), rel_bias[0][:, None] * LOG2E],
                                axis=1)
    lam_params = jnp.stack([lambda_q1[layer], lambda_k1[layer], lambda_q2[layer], lambda_k2[layer]])
    subln_cols = jnp.broadcast_to(diff_subln[layer][:, None], (DIFF_DV, ATTN_BLOCK))
    a = _diff_attn(far_split, qt, k, vt, dtiles, lam_params, subln_cols)

    wkv = wkv_cross[layer]
    kt, v = _mem_kv(mem, mem_norm[layer][None], wkv[:, :D].T.astype(BF16), wkv[:, D:].astype(BF16))

    wr_hi, wr_lo, rbias = _router_operands(router_group[layer], router_group_bias[layer],
                                           router_expert[layer], router_expert_bias[layer])
    h2c = _mix_cross(a, u, x, w_out[layer].astype(BF16), pool_w[layer].astype(BF16), pool_scale[layer][None],
                     cross_norm[layer][None], wq_cross[layer].astype(BF16), kt, v,
                     wo_cross[layer].astype(BF16), ffn_norm[layer][None], wr_hi, wr_lo, rbias, tm=512)
    h2c = h2c.reshape(B * S, D + LANES)

    moe_tm = 512
    slot, tile_glo, tile_ghi = _group_sort_plan(h2c[:, D + ROUTE_GROUP_LANE].astype(jnp.int32), moe_tm)
    y = _moe(h2c, tile_glo, tile_ghi, slot, ffn_norm[layer][None], w_gate[layer], w_up[layer], w_down[layer],
             final_norm[None], moe_tm)
    return y.reshape(B, S, D)
```

```python
import functools
import math

import jax
import jax.numpy as jnp
from jax import lax
from jax.experimental import pallas as pl
from jax.experimental.pallas import tpu as pltpu

F32 = jnp.float32
BF16 = jnp.bfloat16

EPS = 1e-6
CHUNK = 64
N_DIFF_HEADS = 4
DIFF_DK = 64
DIFF_DV = 128
DIFF_WIDTH = N_DIFF_HEADS * DIFF_DV
POOL_WINDOWS = (2, 4, 8, 16)
POOL_DIM = 128
POOL_WIDTH = len(POOL_WINDOWS) * POOL_DIM
MAX_WINDOW = max(POOL_WINDOWS)
REL_BUCKETS = 32
REL_MAX_DISTANCE = 128
N_CROSS_HEADS = 4
N_EXPERT_GROUPS = 4
EXPERTS_PER_GROUP = 4
N_EXPERTS = N_EXPERT_GROUPS * EXPERTS_PER_GROUP
LAMBDA_INIT = 0.8 - 0.6 * math.exp(-0.3 * 0)

LANES = 128
SUBLANES = 8
MXU_DIM = 256
VMEM_LIMIT_BYTES = 56 * 1024 * 1024

ATTN_BLOCK = 2 * MXU_DIM
FAR_BUCKET = REL_BUCKETS // 2 - 1
MASK_VALUE = -1e30
M_FINITE = float(jnp.finfo(jnp.float32).max)
LOG2E = math.log2(math.e)
V_AUG_ROWS = DIFF_DV + 16

ROUTER_ROWS = LANES
GROUP_ROW_STRIDE = SUBLANES
ROUTE_GROUP_LANE = GROUP_ROW_STRIDE


def _rms(x, g):
    return x * lax.rsqrt(jnp.mean(x * x, axis=-1, keepdims=True) + EPS) * g


def _dot(a, b):
    return jnp.dot(a, b, preferred_element_type=F32)


def _dot_nt(a, b):
    return lax.dot_general(a, b, (((1,), (1,)), ((), ())), preferred_element_type=F32)


def _in_proj_kernel(x_ref, g_ref, wqv_t_ref, wku_ref, qt_ref, vt_ref, k_ref, u_ref):
    hn = _rms(x_ref[...], g_ref[...]).astype(BF16)
    zt = _dot_nt(wqv_t_ref[...], hn)
    qt_ref[...] = (zt[:DIFF_WIDTH] * (DIFF_DK ** -0.5 * LOG2E)).astype(BF16)
    for c in range(vt_ref.shape[0]):
        vt_ref[c] = zt[DIFF_WIDTH:, c * ATTN_BLOCK:(c + 1) * ATTN_BLOCK].astype(BF16)
    z = _dot(hn, wku_ref[...])
    k_ref[...] = z[:, :DIFF_WIDTH].astype(BF16)
    u_ref[...] = z[:, DIFF_WIDTH:]


def _in_proj(x, attn_norm, wqv_t, wku, tm):
    B, S, D = x.shape
    nkb = tm // ATTN_BLOCK
    return pl.pallas_call(
        _in_proj_kernel,
        grid=(B, S // tm),
        in_specs=[
            pl.BlockSpec((None, tm, D), lambda b, i: (b, i, 0)),
            pl.BlockSpec((1, D), lambda b, i: (0, 0)),
            pl.BlockSpec(wqv_t.shape, lambda b, i: (0, 0)),
            pl.BlockSpec(wku.shape, lambda b, i: (0, 0)),
        ],
        out_specs=[
            pl.BlockSpec((None, DIFF_WIDTH, tm), lambda b, i: (b, 0, i)),
            pl.BlockSpec((None, nkb, DIFF_WIDTH, ATTN_BLOCK), lambda b, i: (b, i, 0, 0)),
            pl.BlockSpec((None, tm, DIFF_WIDTH), lambda b, i: (b, i, 0)),
            pl.BlockSpec((None, tm, POOL_WIDTH), lambda b, i: (b, i, 0)),
        ],
        out_shape=[
            jax.ShapeDtypeStruct((B, DIFF_WIDTH, S), BF16),
            jax.ShapeDtypeStruct((B, S // ATTN_BLOCK, DIFF_WIDTH, ATTN_BLOCK), BF16),
            jax.ShapeDtypeStruct((B, S, DIFF_WIDTH), BF16),
            jax.ShapeDtypeStruct((B, S, POOL_WIDTH), F32),
        ],
        compiler_params=pltpu.CompilerParams(
            dimension_semantics=("arbitrary", "arbitrary"), vmem_limit_bytes=VMEM_LIMIT_BYTES),
        name="in_proj",
    )(x, attn_norm, wqv_t, wku)


def _rel_bucket(rel):
    nb = REL_BUCKETS // 2
    max_exact = nb // 2
    ret = (rel > 0).astype(jnp.int32) * nb
    n = jnp.abs(rel)
    nf = jnp.maximum(n, 1).astype(jnp.float32)
    large = max_exact + (jnp.log(nf / max_exact) / math.log(REL_MAX_DISTANCE / max_exact)
                         * (nb - max_exact)).astype(jnp.int32)
    large = jnp.minimum(large, nb - 1)
    return ret + jnp.where(n < max_exact, n, large)


def _bias_tile_kernel(band_ref, out_ref):
    n = ATTN_BLOCK
    kk = lax.broadcasted_iota(jnp.int32, (n, n), 0)
    qq = lax.broadcasted_iota(jnp.int32, (n, n), 1)
    for d in range(2):
        band = jnp.broadcast_to(band_ref[d], (n, 2 * n))
        tile = pltpu.roll(band, n + 1, axis=1, stride=1, stride_axis=0)[:, :n]
        if d == 0:
            tile = jnp.where(kk // CHUNK <= qq // CHUNK, tile, MASK_VALUE)
        out_ref[d] = tile
    out_ref[2] = jnp.zeros((n, n), F32)


def _bias_tiles(rel_bias):
    n = ATTN_BLOCK
    m = jnp.arange(2 * n, dtype=jnp.int32)
    rel = jnp.stack([n - 1 - m, -1 - m])
    band = (rel_bias[_rel_bucket(rel)] - rel_bias[FAR_BUCKET]) * LOG2E
    band = band.transpose(2, 0, 1)[:, :, None, :]
    return pl.pallas_call(
        _bias_tile_kernel,
        grid=(N_DIFF_HEADS,),
        in_specs=[pl.BlockSpec((None, 2, 1, 2 * n), lambda h: (h, 0, 0, 0))],
        out_specs=pl.BlockSpec((None, 3, n, n), lambda h: (h, 0, 0, 0)),
        out_shape=jax.ShapeDtypeStruct((N_DIFF_HEADS, 3, n, n), F32),
        name="bias_tiles",
    )(band)


def _attn_kernel(far_ref, qta_ref, qtb_ref, k_ref, vt_ref, dt_ref, lam_ref, g_ref, oa_ref, ob_ref,
                 kaug, vaug, qaug, qref, m_s, acc_s, s_buf, p_buf):
    h = pl.program_id(1)
    step = pl.program_id(2)
    nkb = vt_ref.shape[0]
    QB = ATTN_BLOCK
    KB = ATTN_BLOCK
    q_block = (step, nkb - 1 - step)
    n_a = step + 1

    @pl.when(step == 0)
    def _():
        lane = lax.broadcasted_iota(jnp.int32, (KB, LANES), 1)
        pad = jnp.where(lane == 0, far_ref[h, 0],
                        jnp.where(lane == 1, far_ref[h, 1],
                                  jnp.where(lane == 2, far_ref[h, 2],
                                            jnp.where(lane < 6, 1.0, 0.0)))).astype(BF16)
        vrow = lax.broadcasted_iota(jnp.int32, (V_AUG_ROWS - DIFF_DV, KB), 0)
        ones_row = jnp.where(vrow == 0, 1.0, 0.0).astype(BF16)

        def fill(c, carry):
            r = pl.multiple_of(c * KB, KB)
            kaug[pl.ds(r, KB), :LANES] = k_ref[pl.ds(r, KB), :]
            kaug[pl.ds(r, KB), LANES:] = pad
            vaug[c, :DIFF_DV] = vt_ref[c]
            vaug[c, DIFF_DV:] = ones_row
            return carry

        lax.fori_loop(0, nkb, fill, 0)

    zeros = jnp.zeros((DIFF_DK, QB), BF16)
    row = lax.broadcasted_iota(jnp.int32, (LANES, QB), 0)
    ones_blk = jnp.where(row < 3, 1.0, 0.0)

    def fill_query_rows(buf, w, q):
        buf[w, 0, :DIFF_DK] = q[:DIFF_DK]
        buf[w, 0, DIFF_DK:2 * DIFF_DK] = zeros
        buf[w, 1, :DIFF_DK] = zeros
        buf[w, 1, DIFF_DK:2 * DIFF_DK] = q[DIFF_DK:]

    for w, qt_ref in enumerate((qta_ref, qtb_ref)):
        q = qt_ref[...]
        fill_query_rows(qref, w, q)
        rd = pl.multiple_of(q_block[w] * KB, KB)
        k_own_t = kaug[pl.ds(rd, KB), :LANES].astype(F32).T
        qk = q.astype(F32) * k_own_t
        for half in range(2):
            m = jnp.sum(qk[half * DIFF_DK:(half + 1) * DIFF_DK], axis=0, keepdims=True) + far_ref[h, 3]
            m_s[w, half] = m
            nm = -m
            hi = nm.astype(BF16).astype(F32)
            mid = (nm - hi).astype(BF16).astype(F32)
            lo = (nm - hi - mid).astype(BF16).astype(F32)
            qref[w, half, 2 * DIFF_DK:] = jnp.where(
                row == 3, hi, jnp.where(row == 4, mid, jnp.where(row == 5, lo, ones_blk))).astype(BF16)
    acc_s[...] = jnp.zeros(acc_s.shape, F32)

    def stream_attrs(u):
        w = (u >= n_a).astype(jnp.int32)
        t = u - w * n_a
        i_q = jnp.where(w == 0, q_block[0], q_block[1])
        j = jnp.where(t == 0, i_q, jnp.where(t == 1, i_q - 1, t - 2))
        return w, j, jnp.minimum(t, 2)

    def probabilities(par, u):
        w, j, kind = stream_attrs(u)
        r = pl.multiple_of(j * KB, KB)
        kblk = kaug[pl.ds(r, KB), :]
        bias = dt_ref[kind]
        for half in range(2):
            s = _dot(kblk, qref[w, half]) + bias
            p_buf[par, half] = jnp.exp2(s.astype(BF16))

    def weighted_values(par, u):
        w, j, _ = stream_attrs(u)
        vblk = vaug[j]
        for half in range(2):
            acc_s[w, half] += _dot(vblk, p_buf[par, half])

    def stream_step(par, u):
        weighted_values(1 - par, u - 1)
        probabilities(par, u)

    probabilities(0, jnp.int32(0))

    def stream_pair(g, carry):
        stream_step(1, 2 * g + 1)
        stream_step(0, 2 * g + 2)
        return carry

    lax.fori_loop(0, nkb // 2, stream_pair, 0)
    weighted_values(0, jnp.int32(nkb))

    lam = (jnp.exp(jnp.sum(lam_ref[0:1] * lam_ref[1:2], axis=-1, keepdims=True))
           - jnp.exp(jnp.sum(lam_ref[2:3] * lam_ref[3:4], axis=-1, keepdims=True)) + LAMBDA_INIT)

    def finalize(w, o_ref):
        l0 = acc_s[w, 0, DIFF_DV:DIFF_DV + 1]
        l1 = acc_s[w, 1, DIFF_DV:DIFF_DV + 1]
        ot = acc_s[w, 0, :DIFF_DV] / l0 - lam * (acc_s[w, 1, :DIFF_DV] / l1)
        inv = lax.rsqrt(jnp.mean(ot * ot, axis=0, keepdims=True) + EPS)
        o_ref[...] = (((ot * inv) * g_ref[...]) * (1.0 - LAMBDA_INIT)).T.astype(BF16)
        bad = jnp.where(jnp.abs(ot) <= M_FINITE, 0.0, 1.0) + jnp.where(jnp.maximum(l0, l1) <= M_FINITE, 0.0, 1.0)
        return jnp.max(bad) > 0.0

    for w, (qt_ref, o_ref) in enumerate(((qta_ref, oa_ref), (qtb_ref, ob_ref))):
        overflowed = finalize(w, o_ref)

        @pl.when(overflowed)
        def _():
            i_q = q_block[w]
            fill_query_rows(qaug, 0, qt_ref[...])
            qaug[0, 0, 2 * DIFF_DK:] = ones_blk.astype(BF16)
            qaug[0, 1, 2 * DIFF_DK:] = ones_blk.astype(BF16)
            acc_s[w] = jnp.zeros(acc_s.shape[1:], F32)

            def online_step(j, kind):
                r = pl.multiple_of(j * KB, KB)
                kblk = kaug[pl.ds(r, KB), :]
                vblk = vaug[j]
                for half in range(2):
                    s_buf[half] = _dot(kblk, qaug[0, half]) + dt_ref[kind]
                    m_old = m_s[w, half]
                    m_new = jnp.maximum(m_old, jnp.max(s_buf[half], axis=0, keepdims=True))
                    p = jnp.exp2((s_buf[half] - m_new).astype(BF16))
                    acc_s[w, half] = jnp.exp2(m_old - m_new) * acc_s[w, half] + _dot(vblk, p)
                    m_s[w, half] = m_new

            online_step(i_q, 0)

            @pl.when(i_q >= 1)
            def _():
                online_step(i_q - 1, 1)

            def far_online(j, carry):
                online_step(j, 2)
                return carry

            lax.fori_loop(0, jnp.maximum(i_q - 1, 0), far_online, 0)
            finalize(w, o_ref)


def _diff_attn(far_split, qt, k, vt, dtiles, lam_params, subln_cols):
    B, S, _ = k.shape
    nq = S // ATTN_BLOCK
    half_blocks = nq // 2
    out_half = jax.ShapeDtypeStruct((B, S // 2, DIFF_WIDTH), BF16)
    return pl.pallas_call(
        _attn_kernel,
        grid=(B, N_DIFF_HEADS, half_blocks),
        in_specs=[
            pl.BlockSpec(memory_space=pltpu.SMEM),
            pl.BlockSpec((None, 2 * DIFF_DK, ATTN_BLOCK), lambda b, h, s: (b, h, s)),
            pl.BlockSpec((None, 2 * DIFF_DK, ATTN_BLOCK), lambda b, h, s: (b, h, nq - 1 - s)),
            pl.BlockSpec((None, S, 2 * DIFF_DK), lambda b, h, s: (b, 0, h)),
            pl.BlockSpec((None, nq, DIFF_DV, ATTN_BLOCK), lambda b, h, s: (b, 0, h, 0)),
            pl.BlockSpec((None, 3, ATTN_BLOCK, ATTN_BLOCK), lambda b, h, s: (h, 0, 0, 0)),
            pl.BlockSpec(lam_params.shape, lambda b, h, s: (0, 0)),
            pl.BlockSpec((DIFF_DV, ATTN_BLOCK), lambda b, h, s: (0, 0)),
        ],
        out_specs=[
            pl.BlockSpec((None, ATTN_BLOCK, DIFF_DV), lambda b, h, s: (b, s, h)),
            pl.BlockSpec((None, ATTN_BLOCK, DIFF_DV), lambda b, h, s: (b, half_blocks - 1 - s, h)),
        ],
        out_shape=[out_half, out_half],
        scratch_shapes=[
            pltpu.VMEM((S, 2 * LANES), BF16),
            pltpu.VMEM((nq, V_AUG_ROWS, ATTN_BLOCK), BF16),
            pltpu.VMEM((1, 2, 2 * LANES, ATTN_BLOCK), BF16),
            pltpu.VMEM((2, 2, 2 * LANES, ATTN_BLOCK), BF16),
            pltpu.VMEM((2, 2, 1, ATTN_BLOCK), F32),
            pltpu.VMEM((2, 2, V_AUG_ROWS, ATTN_BLOCK), F32),
            pltpu.VMEM((2, ATTN_BLOCK, ATTN_BLOCK), F32),
            pltpu.VMEM((2, 2, ATTN_BLOCK, ATTN_BLOCK), BF16),
        ],
        compiler_params=pltpu.CompilerParams(
            dimension_semantics=("arbitrary", "arbitrary", "arbitrary"), vmem_limit_bytes=VMEM_LIMIT_BYTES),
        name="diff_attn",
    )(far_split, qt, qt, k, vt, dtiles, lam_params, subln_cols)


def _mem_kv_kernel(mem_ref, g_ref, wk_t_ref, wv_ref, kt_ref, v_ref):
    mn = _rms(mem_ref[...], g_ref[...]).astype(BF16)
    kt_ref[...] = _dot_nt(wk_t_ref[...], mn).astype(BF16)
    v_ref[...] = _dot(mn, wv_ref[...]).astype(BF16)


def _mem_kv(mem, mem_norm, wk_t, wv):
    B, M, D = mem.shape
    return pl.pallas_call(
        _mem_kv_kernel,
        grid=(B,),
        in_specs=[
            pl.BlockSpec((None, M, D), lambda b: (b, 0, 0)),
            pl.BlockSpec((1, D), lambda b: (0, 0)),
            pl.BlockSpec((D, D), lambda b: (0, 0)),
            pl.BlockSpec((D, D), lambda b: (0, 0)),
        ],
        out_specs=[
            pl.BlockSpec((None, D, M), lambda b: (b, 0, 0)),
            pl.BlockSpec((None, M, D), lambda b: (b, 0, 0)),
        ],
        out_shape=[jax.ShapeDtypeStruct((B, D, M), BF16), jax.ShapeDtypeStruct((B, M, D), BF16)],
        compiler_params=pltpu.CompilerParams(
            dimension_semantics=("arbitrary",), vmem_limit_bytes=VMEM_LIMIT_BYTES),
        name="mem_kv",
    )(mem, mem_norm, wk_t, wv)


def _first_argmax_rows(v, vmax):
    row = lax.broadcasted_iota(jnp.int32, v.shape, 0)
    return jnp.min(jnp.where(v == vmax, row, v.shape[0]), axis=0, keepdims=True)


def _mix_cross_kernel(alo_ref, ahi_ref, u_ref, uprev_ref, x_ref, wout_ref, pw_ref, ps_ref, cn_ref, wq_ref, kt_ref, v_ref,
                      wo_ref, fn_ref, wrh_ref, wrl_ref, rb_ref, h2c_ref):
    i = pl.program_id(1)
    tm = u_ref.shape[0]
    D = x_ref.shape[1]
    hd = kt_ref.shape[0] // N_CROSS_HEADS

    u = u_ref[...]
    prev = jnp.where(i > 0, uprev_ref[...], 0.0)
    pos1 = (i * tm + 1 + lax.broadcasted_iota(jnp.int32, (tm, 1), 0)).astype(F32)
    ys = []
    for g, w in enumerate(POOL_WINDOWS):
        sl = slice(g * POOL_DIM, (g + 1) * POOL_DIM)
        acc = jnp.concatenate([prev[:, sl], u[:, sl]], axis=0)
        span = 1
        while span < w:
            acc = acc[span:] + acc[:-span]
            span *= 2
        first = MAX_WINDOW - (w - 1)
        mean = acc[first:first + tm] / jnp.minimum(pos1, float(w))
        d = (mean - u[:, sl]).astype(BF16)
        ys.append(_dot(d, pw_ref[g]) * ps_ref[:, sl])
    p = jnp.concatenate(ys, axis=-1).astype(BF16)

    a = jnp.where(i < pl.num_programs(1) // 2, alo_ref[...], ahi_ref[...])
    h1 = x_ref[...] + _dot(jnp.concatenate([a, p], axis=-1), wout_ref[...])

    q = _dot(_rms(h1, cn_ref[...]).astype(BF16), wq_ref[...]).astype(BF16)
    outs = []
    for c in range(N_CROSS_HEADS):
        sl = slice(c * hd, (c + 1) * hd)
        s = _dot(q[:, sl], kt_ref[sl, :]) * (hd ** -0.5)
        e = jnp.exp(s - jnp.max(s, axis=-1, keepdims=True))
        pr = (e / jnp.sum(e, axis=-1, keepdims=True)).astype(BF16)
        outs.append(_dot(pr, v_ref[:, sl]))
    o = jnp.concatenate(outs, axis=-1).astype(BF16)
    h2 = h1 + _dot(o, wo_ref[...])
    h2c_ref[:, :D] = h2

    hf = _rms(h2, fn_ref[...])
    hf_hi = hf.astype(BF16)
    hf_lo = (hf - hf_hi.astype(F32)).astype(BF16)
    lg = (_dot_nt(wrh_ref[...], hf_hi) + _dot_nt(wrl_ref[...], hf_hi) + _dot_nt(wrh_ref[...], hf_lo)
          + rb_ref[...])
    R = GROUP_ROW_STRIDE
    gl = lg[:R]
    gmax = jnp.max(gl, axis=0, keepdims=True)
    gsel = _first_argmax_rows(gl, gmax)
    g_w = 1.0 / jnp.sum(jnp.exp(gl - gmax), axis=0, keepdims=True)
    el = lg[R:2 * R]
    for g in range(1, N_EXPERT_GROUPS):
        el = jnp.where(gsel == g, lg[R * (g + 1):R * (g + 2)], el)
    row = lax.broadcasted_iota(jnp.int32, el.shape, 0)
    v0 = jnp.max(el, axis=0, keepdims=True)
    i0 = _first_argmax_rows(el, v0)
    el1 = jnp.where(row == i0, -jnp.inf, el)
    v1 = jnp.max(el1, axis=0, keepdims=True)
    i1 = _first_argmax_rows(el1, v1)
    t = jnp.exp(v1 - v0)
    w0 = 1.0 / (1.0 + t)
    w1 = t / (1.0 + t)
    inner = jnp.where(row == i0, w0, 0.0) + jnp.where(row == i1, w1, 0.0)
    blocks = [g_w * inner, jnp.where(row == 0, gsel.astype(F32), 0.0),
              jnp.zeros((ROUTER_ROWS - 2 * R, tm), F32)]
    h2c_ref[:, D:] = jnp.concatenate(blocks, axis=0).T


def _mix_cross(a_lo, a_hi, u, x, wout, pool_w, pool_scale, cross_norm, wq, kt, v, wo, ffn_norm, wr_hi, wr_lo, rbias, tm):
    B, S, D = x.shape
    M = v.shape[1]
    prev_blocks = tm // MAX_WINDOW
    half_tiles = S // tm // 2
    const2 = lambda b, i: (0, 0)
    return pl.pallas_call(
        _mix_cross_kernel,
        grid=(B, S // tm),
        in_specs=[
            pl.BlockSpec((None, tm, DIFF_WIDTH), lambda b, i: (b, jnp.minimum(i, half_tiles - 1), 0)),
            pl.BlockSpec((None, tm, DIFF_WIDTH), lambda b, i: (b, jnp.maximum(i - half_tiles, 0), 0)),
            pl.BlockSpec((None, tm, POOL_WIDTH), lambda b, i: (b, i, 0)),
            pl.BlockSpec((None, MAX_WINDOW, POOL_WIDTH), lambda b, i: (b, jnp.maximum(i * prev_blocks - 1, 0), 0)),
            pl.BlockSpec((None, tm, D), lambda b, i: (b, i, 0)),
            pl.BlockSpec(wout.shape, const2),
            pl.BlockSpec(pool_w.shape, lambda b, i: (0, 0, 0)),
            pl.BlockSpec((1, POOL_WIDTH), const2),
            pl.BlockSpec((1, D), const2),
            pl.BlockSpec(wq.shape, const2),
            pl.BlockSpec((None, D, M), lambda b, i: (b, 0, 0)),
            pl.BlockSpec((None, M, D), lambda b, i: (b, 0, 0)),
            pl.BlockSpec(wo.shape, const2),
            pl.BlockSpec((1, D), const2),
            pl.BlockSpec(wr_hi.shape, const2),
            pl.BlockSpec(wr_lo.shape, const2),
            pl.BlockSpec(rbias.shape, const2),
        ],
        out_specs=pl.BlockSpec((None, tm, D + LANES), lambda b, i: (b, i, 0)),
        out_shape=jax.ShapeDtypeStruct((B, S, D + LANES), F32),
        compiler_params=pltpu.CompilerParams(
            dimension_semantics=("arbitrary", "arbitrary"), vmem_limit_bytes=VMEM_LIMIT_BYTES),
        name="mix_cross",
    )(a_lo, a_hi, u, u, x, wout, pool_w, pool_scale, cross_norm, wq, kt, v, wo, ffn_norm, wr_hi, wr_lo, rbias)


def _moe_kernel(glo_ref, ghi_ref, slot_ref, h2c_hbm, wg_hbm, wu_hbm, wd_hbm, fn_ref, wg_ref, wu_ref, wd_ref, gn_ref,
                y_hbm, src_ref, xbuf, ybuf, hf_s, acc_s, wg_x, wu_x, wd_x, gsem, ssem, wsem, fence_sem):
    c = pl.program_id(0)
    ntiles = pl.num_programs(0)
    tm, D = hf_s.shape
    slot = c % 2

    @pl.when(c == 0)
    def _():
        def invert(t, carry):
            src_ref[slot_ref[t]] = t
            return carry
        lax.fori_loop(0, slot_ref.shape[0], invert, 0, unroll=8)

    def row_gather(tile, r, s):
        t = src_ref[tile * tm + r]
        return pltpu.make_async_copy(h2c_hbm.at[pl.ds(t, 1)], xbuf.at[s, pl.ds(r, 1)], gsem.at[s])

    def row_scatter(tile, r, s):
        t = src_ref[tile * tm + r]
        return pltpu.make_async_copy(ybuf.at[s, pl.ds(r, 1)], y_hbm.at[pl.ds(t, 1)], ssem.at[s])

    def start_rows(make, tile):
        def body(r, carry):
            make(tile, r, tile % 2).start()
            return carry
        lax.fori_loop(0, tm, body, 0, unroll=8)

    def wait_gather(s):
        pltpu.make_async_copy(h2c_hbm.at[pl.ds(0, tm)], xbuf.at[s], gsem.at[s]).wait()

    def wait_scatter(s):
        pltpu.make_async_copy(ybuf.at[s], y_hbm.at[pl.ds(0, tm)], ssem.at[s]).wait()

    @pl.when(c == 0)
    def _():
        start_rows(row_gather, 0)

    wait_gather(slot)

    x = xbuf[slot]
    h2 = x[:, :D]
    hf_s[...] = _rms(h2, fn_ref[...]).astype(BF16)
    group = x[:, D + ROUTE_GROUP_LANE:D + ROUTE_GROUP_LANE + 1]

    def expert_pass(g, wg, wu, wd, first, neighbour_slot=None):
        hf = hf_s[...]
        in_group = group == g.astype(F32)
        n_segments = EXPERTS_PER_GROUP
        rows_per_segment = tm // n_segments
        experts_per_segment = EXPERTS_PER_GROUP // n_segments
        for e in range(EXPERTS_PER_GROUP):
            seg, first_in_seg = divmod(e, experts_per_segment)
            if neighbour_slot is not None and first_in_seg == 0:
                for r in range(seg * rows_per_segment, (seg + 1) * rows_per_segment):
                    row_gather(c + 1, r, neighbour_slot).start(priority=r % 2)
                    row_scatter(c - 1, r, neighbour_slot).start(priority=(r + 1) % 2)
            hg = _dot(hf, wg[e].astype(BF16))
            hu = _dot(hf, wu[e].astype(BF16))
            cw = jnp.where(in_group, x[:, D + e:D + e + 1], 0.0)
            act = (hg * (1.0 / (1.0 + jnp.exp(-hg))) * hu * cw).astype(BF16)
            part = _dot(act, wd[e].astype(BF16))
            if first and e == 0:
                acc_s[...] = part
            else:
                acc_s[...] += part
            if neighbour_slot is not None and first_in_seg == experts_per_segment - 1:
                pl.semaphore_signal(fence_sem, 1)
                pl.semaphore_wait(fence_sem, 1)

    interior = (c >= 1) & (c + 1 < ntiles)
    for par in range(2):
        @pl.when(interior & (slot == par))
        def _():
            expert_pass(glo_ref[c], wg_ref, wu_ref, wd_ref, True, neighbour_slot=1 - par)

    @pl.when(jnp.logical_not(interior))
    def _():
        @pl.when(c + 1 < ntiles)
        def _():
            start_rows(row_gather, c + 1)

        @pl.when(c >= 1)
        def _():
            start_rows(row_scatter, c - 1)

        expert_pass(glo_ref[c], wg_ref, wu_ref, wd_ref, True)

    def extra_pass(g, carry):
        copies = [pltpu.make_async_copy(w_hbm.at[pl.ds(g * EXPERTS_PER_GROUP, EXPERTS_PER_GROUP)], w_x, wsem.at[n])
                  for n, (w_hbm, w_x) in enumerate(((wg_hbm, wg_x), (wu_hbm, wu_x), (wd_hbm, wd_x)))]
        for cp in copies:
            cp.start()
        for cp in copies:
            cp.wait()
        expert_pass(g, wg_x, wu_x, wd_x, False)
        return carry

    lax.fori_loop(glo_ref[c] + 1, ghi_ref[c] + 1, extra_pass, 0)

    @pl.when(c >= 2)
    def _():
        wait_scatter(slot)

    ybuf[slot] = _rms(h2 + acc_s[...], gn_ref[...])

    @pl.when(c == ntiles - 1)
    def _():
        start_rows(row_scatter, c)
        wait_scatter(slot)

        @pl.when(ntiles >= 2)
        def _():
            wait_scatter(1 - slot)


def _moe(h2c, tile_glo, tile_ghi, slot, ffn_norm, wg, wu, wd, final_norm, tm):
    T, DL = h2c.shape
    D = DL - LANES
    FF = wg.shape[-1]
    E = EXPERTS_PER_GROUP
    any_spec = pl.BlockSpec(memory_space=pl.ANY)
    grid_spec = pltpu.PrefetchScalarGridSpec(
        num_scalar_prefetch=3,
        grid=(T // tm,),
        in_specs=[
            any_spec, any_spec, any_spec, any_spec,
            pl.BlockSpec((1, D), lambda c, glo, ghi, src: (0, 0)),
            pl.BlockSpec((E, D, FF), lambda c, glo, ghi, src: (glo[c], 0, 0)),
            pl.BlockSpec((E, D, FF), lambda c, glo, ghi, src: (glo[c], 0, 0)),
            pl.BlockSpec((E, FF, D), lambda c, glo, ghi, src: (glo[c], 0, 0)),
            pl.BlockSpec((1, D), lambda c, glo, ghi, src: (0, 0)),
        ],
        out_specs=any_spec,
        scratch_shapes=[
            pltpu.SMEM((T,), jnp.int32),
            pltpu.VMEM((2, tm, DL), F32),
            pltpu.VMEM((2, tm, D), F32),
            pltpu.VMEM((tm, D), BF16),
            pltpu.VMEM((tm, D), F32),
            pltpu.VMEM((E, D, FF), wg.dtype),
            pltpu.VMEM((E, D, FF), wu.dtype),
            pltpu.VMEM((E, FF, D), wd.dtype),
            pltpu.SemaphoreType.DMA((2,)),
            pltpu.SemaphoreType.DMA((2,)),
            pltpu.SemaphoreType.DMA((3,)),
            pltpu.SemaphoreType.REGULAR,
        ],
    )
    return pl.pallas_call(
        _moe_kernel,
        grid_spec=grid_spec,
        out_shape=jax.ShapeDtypeStruct((T, D), F32),
        compiler_params=pltpu.CompilerParams(
            dimension_semantics=("arbitrary",), vmem_limit_bytes=VMEM_LIMIT_BYTES),
        name="moe",
    )(tile_glo, tile_ghi, slot, h2c, wg, wu, wd, ffn_norm, wg, wu, wd, final_norm)


def _group_sort_plan(group, tm):
    T = group.shape[0]
    onehot = (group[:, None] == jnp.arange(N_EXPERT_GROUPS, dtype=jnp.int32)[None, :]).astype(jnp.int32)
    ranks = jnp.cumsum(onehot, axis=0) - onehot
    counts = jnp.sum(onehot, axis=0)
    starts = jnp.cumsum(counts) - counts
    slot = jnp.sum(onehot * (ranks + starts[None, :]), axis=1)
    first = jnp.arange(0, T, tm, dtype=jnp.int32)
    group_at = lambda pos: jnp.sum((starts[None, 1:] <= pos[:, None]).astype(jnp.int32), axis=1)
    return slot, group_at(first), group_at(first + (tm - 1))


def _split3_bf16(c):
    hi = c.astype(BF16).astype(F32)
    mid = (c - hi).astype(BF16).astype(F32)
    lo = (c - hi - mid).astype(BF16).astype(F32)
    return jnp.stack([hi, mid, lo], axis=-1)


def _router_operands(router_group, router_group_bias, router_expert, router_expert_bias):
    D = router_group.shape[0]
    R = GROUP_ROW_STRIDE
    w = jnp.zeros((ROUTER_ROWS, D), F32)
    bias = jnp.zeros((ROUTER_ROWS,), F32)
    slab_pad = jnp.full((R - N_EXPERT_GROUPS,), MASK_VALUE, F32)
    w = w.at[:N_EXPERT_GROUPS].set(router_group.T)
    bias = bias.at[:R].set(jnp.concatenate([router_group_bias, slab_pad]))
    for g in range(N_EXPERT_GROUPS):
        w = w.at[R * (g + 1):R * (g + 1) + EXPERTS_PER_GROUP].set(router_expert[g].T)
        bias = bias.at[R * (g + 1):R * (g + 2)].set(jnp.concatenate([router_expert_bias[g], slab_pad]))
    w_hi = w.astype(BF16)
    w_lo = (w - w_hi.astype(F32)).astype(BF16)
    return w_hi, w_lo, bias[:, None]


def kernel(x, mem, rel_bias, attn_norm, w_in, lambda_q1, lambda_k1, lambda_q2, lambda_k2, diff_subln, pool_w,
           pool_scale, w_out, cross_norm, mem_norm, wq_cross, wkv_cross, wo_cross, ffn_norm, router_group,
           router_group_bias, router_expert, router_expert_bias, w_gate, w_up, w_down, final_norm):
    B, S, D = x.shape
    layer = 0
    w_in_l = w_in[layer]
    wqv_t = jnp.concatenate([w_in_l[:, :DIFF_WIDTH], w_in_l[:, 2 * DIFF_WIDTH:3 * DIFF_WIDTH]], axis=1).T.astype(BF16)
    wku = jnp.concatenate([w_in_l[:, DIFF_WIDTH:2 * DIFF_WIDTH], w_in_l[:, 3 * DIFF_WIDTH:]], axis=1).astype(BF16)
    qt, vt, k, u = _in_proj(x, attn_norm[layer][None], wqv_t, wku, tm=512)

    dtiles = _bias_tiles(rel_bias)
    far_split = jnp.concatenate([_split3_bf16(rel_bias[FAR_BUCKET] * LOG2E), rel_bias[0][:, None] * LOG2E],
                                axis=1)
    lam_params = jnp.stack([lambda_q1[layer], lambda_k1[layer], lambda_q2[layer], lambda_k2[layer]])
    subln_cols = jnp.broadcast_to(diff_subln[layer][:, None], (DIFF_DV, ATTN_BLOCK))
    a_lo, a_hi = _diff_attn(far_split, qt, k, vt, dtiles, lam_params, subln_cols)

    wkv = wkv_cross[layer]
    kt, v = _mem_kv(mem, mem_norm[layer][None], wkv[:, :D].T.astype(BF16), wkv[:, D:].astype(BF16))

    wr_hi, wr_lo, rbias = _router_operands(router_group[layer], router_group_bias[layer],
                                           router_expert[layer], router_expert_bias[layer])
    h2c = _mix_cross(a_lo, a_hi, u, x, w_out[layer].astype(BF16), pool_w[layer].astype(BF16), pool_scale[layer][None],
                     cross_norm[layer][None], wq_cross[layer].astype(BF16), kt, v,
                     wo_cross[layer].astype(BF16), ffn_norm[layer][None], wr_hi, wr_lo, rbias, tm=512)
    h2c = h2c.reshape(B * S, D + LANES)

    moe_tm = 512
    slot, tile_glo, tile_ghi = _group_sort_plan(h2c[:, D + ROUTE_GROUP_LANE].astype(jnp.int32), moe_tm)
    y = _moe(h2c, tile_glo, tile_ghi, slot, ffn_norm[layer][None], w_gate[layer], w_up[layer], w_down[layer],
             final_norm[None], moe_tm)
    return y.reshape(B, S, D)
```

```python
import functools
import math

import jax
import jax.numpy as jnp
from jax import lax
from jax.experimental import pallas as pl
from jax.experimental.pallas import tpu as pltpu

F32 = jnp.float32
BF16 = jnp.bfloat16

EPS = 1e-6
CHUNK = 64
N_DIFF_HEADS = 4
DIFF_DK = 64
DIFF_DV = 128
DIFF_WIDTH = N_DIFF_HEADS * DIFF_DV
POOL_WINDOWS = (2, 4, 8, 16)
POOL_DIM = 128
POOL_WIDTH = len(POOL_WINDOWS) * POOL_DIM
MAX_WINDOW = max(POOL_WINDOWS)
REL_BUCKETS = 32
REL_MAX_DISTANCE = 128
N_CROSS_HEADS = 4
N_EXPERT_GROUPS = 4
EXPERTS_PER_GROUP = 4
N_EXPERTS = N_EXPERT_GROUPS * EXPERTS_PER_GROUP
LAMBDA_INIT = 0.8 - 0.6 * math.exp(-0.3 * 0)

LANES = 128
SUBLANES = 8
MXU_DIM = 256
VMEM_LIMIT_BYTES = 56 * 1024 * 1024

ATTN_BLOCK = 2 * MXU_DIM
FAR_BUCKET = REL_BUCKETS // 2 - 1
MASK_VALUE = -1e30
M_FINITE = float(jnp.finfo(jnp.float32).max)
LOG2E = math.log2(math.e)
V_AUG_ROWS = DIFF_DV + 16

ROUTER_ROWS = LANES
GROUP_ROW_STRIDE = SUBLANES
ROUTE_GROUP_LANE = GROUP_ROW_STRIDE


def _rms(x, g):
    return x * lax.rsqrt(jnp.mean(x * x, axis=-1, keepdims=True) + EPS) * g


def _dot(a, b):
    return jnp.dot(a, b, preferred_element_type=F32)


def _dot_nt(a, b):
    return lax.dot_general(a, b, (((1,), (1,)), ((), ())), preferred_element_type=F32)


def _in_proj_kernel(x_ref, g_ref, wqv_t_ref, wku_ref, qt_ref, vt_ref, k_ref, u_ref):
    hn = _rms(x_ref[...], g_ref[...]).astype(BF16)
    zt = _dot_nt(wqv_t_ref[...], hn)
    qt_ref[...] = (zt[:DIFF_WIDTH] * (DIFF_DK ** -0.5 * LOG2E)).astype(BF16)
    for c in range(vt_ref.shape[0]):
        vt_ref[c] = zt[DIFF_WIDTH:, c * ATTN_BLOCK:(c + 1) * ATTN_BLOCK].astype(BF16)
    z = _dot(hn, wku_ref[...])
    k_ref[...] = z[:, :DIFF_WIDTH].astype(BF16)
    u_ref[...] = z[:, DIFF_WIDTH:]


def _in_proj(x, attn_norm, wqv_t, wku, tm):
    B, S, D = x.shape
    nkb = tm // ATTN_BLOCK
    return pl.pallas_call(
        _in_proj_kernel,
        grid=(B, S // tm),
        in_specs=[
            pl.BlockSpec((None, tm, D), lambda b, i: (b, i, 0)),
            pl.BlockSpec((1, D), lambda b, i: (0, 0)),
            pl.BlockSpec(wqv_t.shape, lambda b, i: (0, 0)),
            pl.BlockSpec(wku.shape, lambda b, i: (0, 0)),
        ],
        out_specs=[
            pl.BlockSpec((None, DIFF_WIDTH, tm), lambda b, i: (b, 0, i)),
            pl.BlockSpec((None, nkb, DIFF_WIDTH, ATTN_BLOCK), lambda b, i: (b, i, 0, 0)),
            pl.BlockSpec((None, tm, DIFF_WIDTH), lambda b, i: (b, i, 0)),
            pl.BlockSpec((None, tm, POOL_WIDTH), lambda b, i: (b, i, 0)),
        ],
        out_shape=[
            jax.ShapeDtypeStruct((B, DIFF_WIDTH, S), BF16),
            jax.ShapeDtypeStruct((B, S // ATTN_BLOCK, DIFF_WIDTH, ATTN_BLOCK), BF16),
            jax.ShapeDtypeStruct((B, S, DIFF_WIDTH), BF16),
            jax.ShapeDtypeStruct((B, S, POOL_WIDTH), F32),
        ],
        compiler_params=pltpu.CompilerParams(
            dimension_semantics=("arbitrary", "arbitrary"), vmem_limit_bytes=VMEM_LIMIT_BYTES),
        name="in_proj",
    )(x, attn_norm, wqv_t, wku)


def _rel_bucket(rel):
    nb = REL_BUCKETS // 2
    max_exact = nb // 2
    ret = (rel > 0).astype(jnp.int32) * nb
    n = jnp.abs(rel)
    nf = jnp.maximum(n, 1).astype(jnp.float32)
    large = max_exact + (jnp.log(nf / max_exact) / math.log(REL_MAX_DISTANCE / max_exact)
                         * (nb - max_exact)).astype(jnp.int32)
    large = jnp.minimum(large, nb - 1)
    return ret + jnp.where(n < max_exact, n, large)


def _bias_tile_kernel(band_ref, out_ref):
    n = ATTN_BLOCK
    kk = lax.broadcasted_iota(jnp.int32, (n, n), 0)
    qq = lax.broadcasted_iota(jnp.int32, (n, n), 1)
    for d in range(2):
        band = jnp.broadcast_to(band_ref[d], (n, 2 * n))
        tile = pltpu.roll(band, n + 1, axis=1, stride=1, stride_axis=0)[:, :n]
        if d == 0:
            tile = jnp.where(kk // CHUNK <= qq // CHUNK, tile, MASK_VALUE)
        out_ref[d] = tile
    out_ref[2] = jnp.full((n, n), MASK_VALUE, F32)


def _bias_tiles(rel_bias):
    n = ATTN_BLOCK
    m = jnp.arange(2 * n, dtype=jnp.int32)
    rel = jnp.stack([n - 1 - m, -1 - m])
    band = (rel_bias[_rel_bucket(rel)] - rel_bias[FAR_BUCKET]) * LOG2E
    band = band.transpose(2, 0, 1)[:, :, None, :]
    return pl.pallas_call(
        _bias_tile_kernel,
        grid=(N_DIFF_HEADS,),
        in_specs=[pl.BlockSpec((None, 2, 1, 2 * n), lambda h: (h, 0, 0, 0))],
        out_specs=pl.BlockSpec((None, 3, n, n), lambda h: (h, 0, 0, 0)),
        out_shape=jax.ShapeDtypeStruct((N_DIFF_HEADS, 3, n, n), F32),
        name="bias_tiles",
    )(band)


def _attn_kernel(far_ref, qta_ref, qtb_ref, k_ref, vt_ref, dt_ref, lam_ref, g_ref, oa_ref, ob_ref,
                 kaug, vaug, qaug, qref, m_s, acc_s, s_buf, p_buf):
    h = pl.program_id(1)
    step = pl.program_id(2)
    nkb = vt_ref.shape[0]
    QB = ATTN_BLOCK
    KB = ATTN_BLOCK
    q_block = (step, nkb - 1 - step)

    @pl.when(step == 0)
    def _():
        lane = lax.broadcasted_iota(jnp.int32, (KB, LANES), 1)
        pad = jnp.where(lane == 0, far_ref[h, 0],
                        jnp.where(lane == 1, far_ref[h, 1],
                                  jnp.where(lane == 2, far_ref[h, 2],
                                            jnp.where(lane < 6, 1.0, 0.0)))).astype(BF16)
        vrow = lax.broadcasted_iota(jnp.int32, (V_AUG_ROWS - DIFF_DV, KB), 0)
        ones_row = jnp.where(vrow == 0, 1.0, 0.0).astype(BF16)

        def fill(c, carry):
            r = pl.multiple_of(c * KB, KB)
            kaug[pl.ds(r, KB), :LANES] = k_ref[pl.ds(r, KB), :]
            kaug[pl.ds(r, KB), LANES:] = pad
            vaug[c, :DIFF_DV] = vt_ref[c]
            vaug[c, DIFF_DV:] = ones_row
            return carry

        lax.fori_loop(0, nkb, fill, 0)

    zeros = jnp.zeros((DIFF_DK, QB), BF16)
    row = lax.broadcasted_iota(jnp.int32, (LANES, QB), 0)
    ones_blk = jnp.where(row < 3, 1.0, 0.0)

    def fill_query_rows(buf, w, q):
        buf[w, 0, :DIFF_DK] = q[:DIFF_DK]
        buf[w, 0, DIFF_DK:2 * DIFF_DK] = zeros
        buf[w, 1, :DIFF_DK] = zeros
        buf[w, 1, DIFF_DK:2 * DIFF_DK] = q[DIFF_DK:]

    for w, qt_ref in enumerate((qta_ref, qtb_ref)):
        q = qt_ref[...]
        fill_query_rows(qref, w, q)
        rd = pl.multiple_of(q_block[w] * KB, KB)
        k_own_t = kaug[pl.ds(rd, KB), :LANES].astype(F32).T
        qk = q.astype(F32) * k_own_t
        for half in range(2):
            m = jnp.sum(qk[half * DIFF_DK:(half + 1) * DIFF_DK], axis=0, keepdims=True) + far_ref[h, 3]
            m_s[w, half] = m
            nm = -m
            hi = nm.astype(BF16).astype(F32)
            mid = (nm - hi).astype(BF16).astype(F32)
            lo = (nm - hi - mid).astype(BF16).astype(F32)
            qref[w, half, 2 * DIFF_DK:] = jnp.where(
                row == 3, hi, jnp.where(row == 4, mid, jnp.where(row == 5, lo, ones_blk))).astype(BF16)
    acc_s[...] = jnp.zeros(acc_s.shape, F32)

    i_a, i_b = q_block
    near_a = (0, jnp.maximum(i_a - 1, 0))
    biased = ((0, i_a), near_a, (1, i_b), (1, i_b - 1))
    bias_tile = (0, jnp.where(i_a >= 1, 1, 2), 0, 1)
    n_far_a = jnp.maximum(i_a - 1, 0)
    n_far = n_far_a + i_b - 1

    def far_block(f):
        w = (f >= n_far_a).astype(jnp.int32)
        return w, f - w * n_far_a

    def probabilities(par, blk, bias):
        w, j = blk
        r = pl.multiple_of(j * KB, KB)
        kblk = kaug[pl.ds(r, KB), :]
        for half in range(2):
            s = _dot(kblk, qref[w, half])
            if bias is not None:
                s = s + dt_ref[bias]
            p_buf[par, half] = jnp.exp2(s.astype(BF16))

    def weighted_values(par, blk):
        w, j = blk
        vblk = vaug[j]
        for half in range(2):
            acc_s[w, half] += _dot(vblk, p_buf[par, half])

    probabilities(0, biased[0], bias_tile[0])
    for u in range(1, 4):
        weighted_values((u - 1) % 2, biased[u - 1])
        probabilities(u % 2, biased[u], bias_tile[u])

    def far_step(par, f):
        w_prev, j_prev = far_block(f - 1)
        first = f == 0
        weighted_values(1 - par, (jnp.where(first, biased[3][0], w_prev), jnp.where(first, biased[3][1], j_prev)))
        probabilities(par, far_block(f), None)

    def far_pair(g, carry):
        far_step(0, 2 * g)
        far_step(1, 2 * g + 1)
        return carry

    lax.fori_loop(0, n_far // 2, far_pair, 0)

    @pl.when(n_far % 2 == 1)
    def _():
        far_step(0, n_far - 1)

    w_last, j_last = far_block(n_far - 1)
    last = (jnp.where(n_far == 0, biased[3][0], w_last), jnp.where(n_far == 0, biased[3][1], j_last))
    for par in range(2):
        @pl.when((n_far + 1) % 2 == par)
        def _():
            weighted_values(par, last)

    lam = (jnp.exp(jnp.sum(lam_ref[0:1] * lam_ref[1:2], axis=-1, keepdims=True))
           - jnp.exp(jnp.sum(lam_ref[2:3] * lam_ref[3:4], axis=-1, keepdims=True)) + LAMBDA_INIT)

    def finalize(w, o_ref):
        l0 = acc_s[w, 0, DIFF_DV:DIFF_DV + 1]
        l1 = acc_s[w, 1, DIFF_DV:DIFF_DV + 1]
        ot = acc_s[w, 0, :DIFF_DV] / l0 - lam * (acc_s[w, 1, :DIFF_DV] / l1)
        inv = lax.rsqrt(jnp.mean(ot * ot, axis=0, keepdims=True) + EPS)
        o_ref[...] = (((ot * inv) * g_ref[...]) * (1.0 - LAMBDA_INIT)).T.astype(BF16)
        bad = jnp.where(jnp.abs(ot) <= M_FINITE, 0.0, 1.0) + jnp.where(jnp.maximum(l0, l1) <= M_FINITE, 0.0, 1.0)
        return jnp.max(bad) > 0.0

    for w, (qt_ref, o_ref) in enumerate(((qta_ref, oa_ref), (qtb_ref, ob_ref))):
        overflowed = finalize(w, o_ref)

        @pl.when(overflowed)
        def _():
            i_q = q_block[w]
            fill_query_rows(qaug, 0, qt_ref[...])
            qaug[0, 0, 2 * DIFF_DK:] = ones_blk.astype(BF16)
            qaug[0, 1, 2 * DIFF_DK:] = ones_blk.astype(BF16)
            acc_s[w] = jnp.zeros(acc_s.shape[1:], F32)

            def online_step(j, kind):
                r = pl.multiple_of(j * KB, KB)
                kblk = kaug[pl.ds(r, KB), :]
                vblk = vaug[j]
                for half in range(2):
                    s_buf[half] = _dot(kblk, qaug[0, half])
                    if kind is not None:
                        s_buf[half] += dt_ref[kind]
                    m_old = m_s[w, half]
                    m_new = jnp.maximum(m_old, jnp.max(s_buf[half], axis=0, keepdims=True))
                    p = jnp.exp2((s_buf[half] - m_new).astype(BF16))
                    acc_s[w, half] = jnp.exp2(m_old - m_new) * acc_s[w, half] + _dot(vblk, p)
                    m_s[w, half] = m_new

            online_step(i_q, 0)

            @pl.when(i_q >= 1)
            def _():
                online_step(i_q - 1, 1)

            def far_online(j, carry):
                online_step(j, None)
                return carry

            lax.fori_loop(0, jnp.maximum(i_q - 1, 0), far_online, 0)
            finalize(w, o_ref)


def _diff_attn(far_split, qt, k, vt, dtiles, lam_params, subln_cols):
    B, S, _ = k.shape
    nq = S // ATTN_BLOCK
    half_blocks = nq // 2
    out_half = jax.ShapeDtypeStruct((B, S // 2, DIFF_WIDTH), BF16)
    return pl.pallas_call(
        _attn_kernel,
        grid=(B, N_DIFF_HEADS, half_blocks),
        in_specs=[
            pl.BlockSpec(memory_space=pltpu.SMEM),
            pl.BlockSpec((None, 2 * DIFF_DK, ATTN_BLOCK), lambda b, h, s: (b, h, s)),
            pl.BlockSpec((None, 2 * DIFF_DK, ATTN_BLOCK), lambda b, h, s: (b, h, nq - 1 - s)),
            pl.BlockSpec((None, S, 2 * DIFF_DK), lambda b, h, s: (b, 0, h)),
            pl.BlockSpec((None, nq, DIFF_DV, ATTN_BLOCK), lambda b, h, s: (b, 0, h, 0)),
            pl.BlockSpec((None, 3, ATTN_BLOCK, ATTN_BLOCK), lambda b, h, s: (h, 0, 0, 0)),
            pl.BlockSpec(lam_params.shape, lambda b, h, s: (0, 0)),
            pl.BlockSpec((DIFF_DV, ATTN_BLOCK), lambda b, h, s: (0, 0)),
        ],
        out_specs=[
            pl.BlockSpec((None, ATTN_BLOCK, DIFF_DV), lambda b, h, s: (b, s, h)),
            pl.BlockSpec((None, ATTN_BLOCK, DIFF_DV), lambda b, h, s: (b, half_blocks - 1 - s, h)),
        ],
        out_shape=[out_half, out_half],
        scratch_shapes=[
            pltpu.VMEM((S, 2 * LANES), BF16),
            pltpu.VMEM((nq, V_AUG_ROWS, ATTN_BLOCK), BF16),
            pltpu.VMEM((1, 2, 2 * LANES, ATTN_BLOCK), BF16),
            pltpu.VMEM((2, 2, 2 * LANES, ATTN_BLOCK), BF16),
            pltpu.VMEM((2, 2, 1, ATTN_BLOCK), F32),
            pltpu.VMEM((2, 2, V_AUG_ROWS, ATTN_BLOCK), F32),
            pltpu.VMEM((2, ATTN_BLOCK, ATTN_BLOCK), F32),
            pltpu.VMEM((2, 2, ATTN_BLOCK, ATTN_BLOCK), BF16),
        ],
        compiler_params=pltpu.CompilerParams(
            dimension_semantics=("arbitrary", "arbitrary", "arbitrary"), vmem_limit_bytes=VMEM_LIMIT_BYTES),
        name="diff_attn",
    )(far_split, qt, qt, k, vt, dtiles, lam_params, subln_cols)


def _mem_kv_kernel(mem_ref, g_ref, wk_t_ref, wv_ref, kt_ref, v_ref):
    mn = _rms(mem_ref[...], g_ref[...]).astype(BF16)
    kt_ref[...] = _dot_nt(wk_t_ref[...], mn).astype(BF16)
    v_ref[...] = _dot(mn, wv_ref[...]).astype(BF16)


def _mem_kv(mem, mem_norm, wk_t, wv):
    B, M, D = mem.shape
    return pl.pallas_call(
        _mem_kv_kernel,
        grid=(B,),
        in_specs=[
            pl.BlockSpec((None, M, D), lambda b: (b, 0, 0)),
            pl.BlockSpec((1, D), lambda b: (0, 0)),
            pl.BlockSpec((D, D), lambda b: (0, 0)),
            pl.BlockSpec((D, D), lambda b: (0, 0)),
        ],
        out_specs=[
            pl.BlockSpec((None, D, M), lambda b: (b, 0, 0)),
            pl.BlockSpec((None, M, D), lambda b: (b, 0, 0)),
        ],
        out_shape=[jax.ShapeDtypeStruct((B, D, M), BF16), jax.ShapeDtypeStruct((B, M, D), BF16)],
        compiler_params=pltpu.CompilerParams(
            dimension_semantics=("arbitrary",), vmem_limit_bytes=VMEM_LIMIT_BYTES),
        name="mem_kv",
    )(mem, mem_norm, wk_t, wv)


def _first_argmax_rows(v, vmax):
    row = lax.broadcasted_iota(jnp.int32, v.shape, 0)
    return jnp.min(jnp.where(v == vmax, row, v.shape[0]), axis=0, keepdims=True)


def _mix_cross_kernel(alo_ref, ahi_ref, u_ref, uprev_ref, x_ref, wout_ref, pw_ref, ps_ref, cn_ref, wq_ref, kt_ref, v_ref,
                      wo_ref, fn_ref, wrh_ref, wrl_ref, rb_ref, h2c_ref):
    i = pl.program_id(1)
    tm = u_ref.shape[0]
    D = x_ref.shape[1]
    hd = kt_ref.shape[0] // N_CROSS_HEADS

    u = u_ref[...]
    prev = jnp.where(i > 0, uprev_ref[...], 0.0)
    pos1 = (i * tm + 1 + lax.broadcasted_iota(jnp.int32, (tm, 1), 0)).astype(F32)
    ys = []
    for g, w in enumerate(POOL_WINDOWS):
        sl = slice(g * POOL_DIM, (g + 1) * POOL_DIM)
        acc = jnp.concatenate([prev[:, sl], u[:, sl]], axis=0)
        span = 1
        while span < w:
            acc = acc[span:] + acc[:-span]
            span *= 2
        first = MAX_WINDOW - (w - 1)
        mean = acc[first:first + tm] / jnp.minimum(pos1, float(w))
        d = (mean - u[:, sl]).astype(BF16)
        ys.append(_dot(d, pw_ref[g]) * ps_ref[:, sl])
    p = jnp.concatenate(ys, axis=-1).astype(BF16)

    a = jnp.where(i < pl.num_programs(1) // 2, alo_ref[...], ahi_ref[...])
    h1 = x_ref[...] + _dot(jnp.concatenate([a, p], axis=-1), wout_ref[...])

    q = _dot(_rms(h1, cn_ref[...]).astype(BF16), wq_ref[...]).astype(BF16)
    outs = []
    for c in range(N_CROSS_HEADS):
        sl = slice(c * hd, (c + 1) * hd)
        s = _dot(q[:, sl], kt_ref[sl, :]) * (hd ** -0.5)
        e = jnp.exp(s - jnp.max(s, axis=-1, keepdims=True))
        pr = (e / jnp.sum(e, axis=-1, keepdims=True)).astype(BF16)
        outs.append(_dot(pr, v_ref[:, sl]))
    o = jnp.concatenate(outs, axis=-1).astype(BF16)
    h2 = h1 + _dot(o, wo_ref[...])
    h2c_ref[:, :D] = h2

    hf = _rms(h2, fn_ref[...])
    hf_hi = hf.astype(BF16)
    hf_lo = (hf - hf_hi.astype(F32)).astype(BF16)
    lg = (_dot_nt(wrh_ref[...], hf_hi) + _dot_nt(wrl_ref[...], hf_hi) + _dot_nt(wrh_ref[...], hf_lo)
          + rb_ref[...])
    R = GROUP_ROW_STRIDE
    gl = lg[:R]
    gmax = jnp.max(gl, axis=0, keepdims=True)
    gsel = _first_argmax_rows(gl, gmax)
    g_w = 1.0 / jnp.sum(jnp.exp(gl - gmax), axis=0, keepdims=True)
    el = lg[R:2 * R]
    for g in range(1, N_EXPERT_GROUPS):
        el = jnp.where(gsel == g, lg[R * (g + 1):R * (g + 2)], el)
    row = lax.broadcasted_iota(jnp.int32, el.shape, 0)
    v0 = jnp.max(el, axis=0, keepdims=True)
    i0 = _first_argmax_rows(el, v0)
    el1 = jnp.where(row == i0, -jnp.inf, el)
    v1 = jnp.max(el1, axis=0, keepdims=True)
    i1 = _first_argmax_rows(el1, v1)
    t = jnp.exp(v1 - v0)
    w0 = 1.0 / (1.0 + t)
    w1 = t / (1.0 + t)
    inner = jnp.where(row == i0, w0, 0.0) + jnp.where(row == i1, w1, 0.0)
    blocks = [g_w * inner, jnp.where(row == 0, gsel.astype(F32), 0.0),
              jnp.zeros((ROUTER_ROWS - 2 * R, tm), F32)]
    h2c_ref[:, D:] = jnp.concatenate(blocks, axis=0).T


def _mix_cross(a_lo, a_hi, u, x, wout, pool_w, pool_scale, cross_norm, wq, kt, v, wo, ffn_norm, wr_hi, wr_lo, rbias, tm):
    B, S, D = x.shape
    M = v.shape[1]
    prev_blocks = tm // MAX_WINDOW
    half_tiles = S // tm // 2
    const2 = lambda b, i: (0, 0)
    return pl.pallas_call(
        _mix_cross_kernel,
        grid=(B, S // tm),
        in_specs=[
            pl.BlockSpec((None, tm, DIFF_WIDTH), lambda b, i: (b, jnp.minimum(i, half_tiles - 1), 0)),
            pl.BlockSpec((None, tm, DIFF_WIDTH), lambda b, i: (b, jnp.maximum(i - half_tiles, 0), 0)),
            pl.BlockSpec((None, tm, POOL_WIDTH), lambda b, i: (b, i, 0)),
            pl.BlockSpec((None, MAX_WINDOW, POOL_WIDTH), lambda b, i: (b, jnp.maximum(i * prev_blocks - 1, 0), 0)),
            pl.BlockSpec((None, tm, D), lambda b, i: (b, i, 0)),
            pl.BlockSpec(wout.shape, const2),
            pl.BlockSpec(pool_w.shape, lambda b, i: (0, 0, 0)),
            pl.BlockSpec((1, POOL_WIDTH), const2),
            pl.BlockSpec((1, D), const2),
            pl.BlockSpec(wq.shape, const2),
            pl.BlockSpec((None, D, M), lambda b, i: (b, 0, 0)),
            pl.BlockSpec((None, M, D), lambda b, i: (b, 0, 0)),
            pl.BlockSpec(wo.shape, const2),
            pl.BlockSpec((1, D), const2),
            pl.BlockSpec(wr_hi.shape, const2),
            pl.BlockSpec(wr_lo.shape, const2),
            pl.BlockSpec(rbias.shape, const2),
        ],
        out_specs=pl.BlockSpec((None, tm, D + LANES), lambda b, i: (b, i, 0)),
        out_shape=jax.ShapeDtypeStruct((B, S, D + LANES), F32),
        compiler_params=pltpu.CompilerParams(
            dimension_semantics=("arbitrary", "arbitrary"), vmem_limit_bytes=VMEM_LIMIT_BYTES),
        name="mix_cross",
    )(a_lo, a_hi, u, u, x, wout, pool_w, pool_scale, cross_norm, wq, kt, v, wo, ffn_norm, wr_hi, wr_lo, rbias)


def _moe_kernel(glo_ref, ghi_ref, slot_ref, h2c_hbm, wg_hbm, wu_hbm, wd_hbm, fn_ref, wg_ref, wu_ref, wd_ref, gn_ref,
                y_hbm, src_ref, xbuf, ybuf, hf_s, acc_s, wg_x, wu_x, wd_x, gsem, ssem, wsem, fence_sem):
    c = pl.program_id(0)
    ntiles = pl.num_programs(0)
    tm, D = hf_s.shape
    slot = c % 2

    @pl.when(c == 0)
    def _():
        def invert(t, carry):
            src_ref[slot_ref[t]] = t
            return carry
        lax.fori_loop(0, slot_ref.shape[0], invert, 0, unroll=8)

    def row_gather(tile, r, s):
        t = src_ref[tile * tm + r]
        return pltpu.make_async_copy(h2c_hbm.at[pl.ds(t, 1)], xbuf.at[s, pl.ds(r, 1)], gsem.at[s])

    def row_scatter(tile, r, s):
        t = src_ref[tile * tm + r]
        return pltpu.make_async_copy(ybuf.at[s, pl.ds(r, 1)], y_hbm.at[pl.ds(t, 1)], ssem.at[s])

    def start_rows(make, tile):
        def body(r, carry):
            make(tile, r, tile % 2).start()
            return carry
        lax.fori_loop(0, tm, body, 0, unroll=8)

    def wait_gather(s):
        pltpu.make_async_copy(h2c_hbm.at[pl.ds(0, tm)], xbuf.at[s], gsem.at[s]).wait()

    def wait_scatter(s):
        pltpu.make_async_copy(ybuf.at[s], y_hbm.at[pl.ds(0, tm)], ssem.at[s]).wait()

    @pl.when(c == 0)
    def _():
        start_rows(row_gather, 0)

    wait_gather(slot)

    x = xbuf[slot]
    h2 = x[:, :D]
    hf_s[...] = _rms(h2, fn_ref[...]).astype(BF16)
    group = x[:, D + ROUTE_GROUP_LANE:D + ROUTE_GROUP_LANE + 1]

    def expert_pass(g, wg, wu, wd, first, neighbour_slot=None):
        hf = hf_s[...]
        in_group = group == g.astype(F32)
        n_segments = EXPERTS_PER_GROUP
        rows_per_segment = tm // n_segments
        experts_per_segment = EXPERTS_PER_GROUP // n_segments
        for e in range(EXPERTS_PER_GROUP):
            seg, first_in_seg = divmod(e, experts_per_segment)
            if neighbour_slot is not None and first_in_seg == 0:
                for r in range(seg * rows_per_segment, (seg + 1) * rows_per_segment):
                    row_gather(c + 1, r, neighbour_slot).start(priority=r % 2)
                    row_scatter(c - 1, r, neighbour_slot).start(priority=(r + 1) % 2)
            hg = _dot(hf, wg[e].astype(BF16))
            hu = _dot(hf, wu[e].astype(BF16))
            cw = jnp.where(in_group, x[:, D + e:D + e + 1], 0.0)
            act = (hg * (1.0 / (1.0 + jnp.exp(-hg))) * hu * cw).astype(BF16)
            part = _dot(act, wd[e].astype(BF16))
            if first and e == 0:
                acc_s[...] = part
            else:
                acc_s[...] += part
            if neighbour_slot is not None and first_in_seg == experts_per_segment - 1:
                pl.semaphore_signal(fence_sem, 1)
                pl.semaphore_wait(fence_sem, 1)

    interior = (c >= 1) & (c + 1 < ntiles)
    for par in range(2):
        @pl.when(interior & (slot == par))
        def _():
            expert_pass(glo_ref[c], wg_ref, wu_ref, wd_ref, True, neighbour_slot=1 - par)

    @pl.when(jnp.logical_not(interior))
    def _():
        @pl.when(c + 1 < ntiles)
        def _():
            start_rows(row_gather, c + 1)

        @pl.when(c >= 1)
        def _():
            start_rows(row_scatter, c - 1)

        expert_pass(glo_ref[c], wg_ref, wu_ref, wd_ref, True)

    def extra_pass(g, carry):
        copies = [pltpu.make_async_copy(w_hbm.at[pl.ds(g * EXPERTS_PER_GROUP, EXPERTS_PER_GROUP)], w_x, wsem.at[n])
                  for n, (w_hbm, w_x) in enumerate(((wg_hbm, wg_x), (wu_hbm, wu_x), (wd_hbm, wd_x)))]
        for cp in copies:
            cp.start()
        for cp in copies:
            cp.wait()
        expert_pass(g, wg_x, wu_x, wd_x, False)
        return carry

    lax.fori_loop(glo_ref[c] + 1, ghi_ref[c] + 1, extra_pass, 0)

    @pl.when(c >= 2)
    def _():
        wait_scatter(slot)

    ybuf[slot] = _rms(h2 + acc_s[...], gn_ref[...])

    @pl.when(c == ntiles - 1)
    def _():
        start_rows(row_scatter, c)
        wait_scatter(slot)

        @pl.when(ntiles >= 2)
        def _():
            wait_scatter(1 - slot)


def _moe(h2c, tile_glo, tile_ghi, slot, ffn_norm, wg, wu, wd, final_norm, tm):
    T, DL = h2c.shape
    D = DL - LANES
    FF = wg.shape[-1]
    E = EXPERTS_PER_GROUP
    any_spec = pl.BlockSpec(memory_space=pl.ANY)
    grid_spec = pltpu.PrefetchScalarGridSpec(
        num_scalar_prefetch=3,
        grid=(T // tm,),
        in_specs=[
            any_spec, any_spec, any_spec, any_spec,
            pl.BlockSpec((1, D), lambda c, glo, ghi, src: (0, 0)),
            pl.BlockSpec((E, D, FF), lambda c, glo, ghi, src: (glo[c], 0, 0)),
            pl.BlockSpec((E, D, FF), lambda c, glo, ghi, src: (glo[c], 0, 0)),
            pl.BlockSpec((E, FF, D), lambda c, glo, ghi, src: (glo[c], 0, 0)),
            pl.BlockSpec((1, D), lambda c, glo, ghi, src: (0, 0)),
        ],
        out_specs=any_spec,
        scratch_shapes=[
            pltpu.SMEM((T,), jnp.int32),
            pltpu.VMEM((2, tm, DL), F32),
            pltpu.VMEM((2, tm, D), F32),
            pltpu.VMEM((tm, D), BF16),
            pltpu.VMEM((tm, D), F32),
            pltpu.VMEM((E, D, FF), wg.dtype),
            pltpu.VMEM((E, D, FF), wu.dtype),
            pltpu.VMEM((E, FF, D), wd.dtype),
            pltpu.SemaphoreType.DMA((2,)),
            pltpu.SemaphoreType.DMA((2,)),
            pltpu.SemaphoreType.DMA((3,)),
            pltpu.SemaphoreType.REGULAR,
        ],
    )
    return pl.pallas_call(
        _moe_kernel,
        grid_spec=grid_spec,
        out_shape=jax.ShapeDtypeStruct((T, D), F32),
        compiler_params=pltpu.CompilerParams(
            dimension_semantics=("arbitrary",), vmem_limit_bytes=VMEM_LIMIT_BYTES),
        name="moe",
    )(tile_glo, tile_ghi, slot, h2c, wg, wu, wd, ffn_norm, wg, wu, wd, final_norm)


def _group_sort_plan(group, tm):
    T = group.shape[0]
    onehot = (group[:, None] == jnp.arange(N_EXPERT_GROUPS, dtype=jnp.int32)[None, :]).astype(jnp.int32)
    ranks = jnp.cumsum(onehot, axis=0) - onehot
    counts = jnp.sum(onehot, axis=0)
    starts = jnp.cumsum(counts) - counts
    slot = jnp.sum(onehot * (ranks + starts[None, :]), axis=1)
    first = jnp.arange(0, T, tm, dtype=jnp.int32)
    group_at = lambda pos: jnp.sum((starts[None, 1:] <= pos[:, None]).astype(jnp.int32), axis=1)
    return slot, group_at(first), group_at(first + (tm - 1))


def _split3_bf16(c):
    hi = c.astype(BF16).astype(F32)
    mid = (c - hi).astype(BF16).astype(F32)
    lo = (c - hi - mid).astype(BF16).astype(F32)
    return jnp.stack([hi, mid, lo], axis=-1)


def _router_operands(router_group, router_group_bias, router_expert, router_expert_bias):
    D = router_group.shape[0]
    R = GROUP_ROW_STRIDE
    w = jnp.zeros((ROUTER_ROWS, D), F32)
    bias = jnp.zeros((ROUTER_ROWS,), F32)
    slab_pad = jnp.full((R - N_EXPERT_GROUPS,), MASK_VALUE, F32)
    w = w.at[:N_EXPERT_GROUPS].set(router_group.T)
    bias = bias.at[:R].set(jnp.concatenate([router_group_bias, slab_pad]))
    for g in range(N_EXPERT_GROUPS):
        w = w.at[R * (g + 1):R * (g + 1) + EXPERTS_PER_GROUP].set(router_expert[g].T)
        bias = bias.at[R * (g + 1):R * (g + 2)].set(jnp.concatenate([router_expert_bias[g], slab_pad]))
    w_hi = w.astype(BF16)
    w_lo = (w - w_hi.astype(F32)).astype(BF16)
    return w_hi, w_lo, bias[:, None]


def kernel(x, mem, rel_bias, attn_norm, w_in, lambda_q1, lambda_k1, lambda_q2, lambda_k2, diff_subln, pool_w,
           pool_scale, w_out, cross_norm, mem_norm, wq_cross, wkv_cross, wo_cross, ffn_norm, router_group,
           router_group_bias, router_expert, router_expert_bias, w_gate, w_up, w_down, final_norm):
    B, S, D = x.shape
    layer = 0
    w_in_l = w_in[layer]
    wqv_t = jnp.concatenate([w_in_l[:, :DIFF_WIDTH], w_in_l[:, 2 * DIFF_WIDTH:3 * DIFF_WIDTH]], axis=1).T.astype(BF16)
    wku = jnp.concatenate([w_in_l[:, DIFF_WIDTH:2 * DIFF_WIDTH], w_in_l[:, 3 * DIFF_WIDTH:]], axis=1).astype(BF16)
    qt, vt, k, u = _in_proj(x, attn_norm[layer][None], wqv_t, wku, tm=512)

    dtiles = _bias_tiles(rel_bias)
    far_split = jnp.concatenate([_split3_bf16(rel_bias[FAR_BUCKET] * LOG2E), rel_bias[0][:, None] * LOG2E],
                                axis=1)
    lam_params = jnp.stack([lambda_q1[layer], lambda_k1[layer], lambda_q2[layer], lambda_k2[layer]])
    subln_cols = jnp.broadcast_to(diff_subln[layer][:, None], (DIFF_DV, ATTN_BLOCK))
    a_lo, a_hi = _diff_attn(far_split, qt, k, vt, dtiles, lam_params, subln_cols)

    wkv = wkv_cross[layer]
    kt, v = _mem_kv(mem, mem_norm[layer][None], wkv[:, :D].T.astype(BF16), wkv[:, D:].astype(BF16))

    wr_hi, wr_lo, rbias = _router_operands(router_group[layer], router_group_bias[layer],
                                           router_expert[layer], router_expert_bias[layer])
    h2c = _mix_cross(a_lo, a_hi, u, x, w_out[layer].astype(BF16), pool_w[layer].astype(BF16), pool_scale[layer][None],
                     cross_norm[layer][None], wq_cross[layer].astype(BF16), kt, v,
                     wo_cross[layer].astype(BF16), ffn_norm[layer][None], wr_hi, wr_lo, rbias, tm=512)
    h2c = h2c.reshape(B * S, D + LANES)

    moe_tm = 512
    slot, tile_glo, tile_ghi = _group_sort_plan(h2c[:, D + ROUTE_GROUP_LANE].astype(jnp.int32), moe_tm)
    y = _moe(h2c, tile_glo, tile_ghi, slot, ffn_norm[layer][None], w_gate[layer], w_up[layer], w_down[layer],
             final_norm[None], moe_tm)
    return y.reshape(B, S, D)
```

```python
import functools
import math

import jax
import jax.numpy as jnp
from jax import lax
from jax.experimental import pallas as pl
from jax.experimental.pallas import tpu as pltpu

F32 = jnp.float32
BF16 = jnp.bfloat16

EPS = 1e-6
CHUNK = 64
N_DIFF_HEADS = 4
DIFF_DK = 64
DIFF_DV = 128
DIFF_WIDTH = N_DIFF_HEADS * DIFF_DV
POOL_WINDOWS = (2, 4, 8, 16)
POOL_DIM = 128
POOL_WIDTH = len(POOL_WINDOWS) * POOL_DIM
MAX_WINDOW = max(POOL_WINDOWS)
REL_BUCKETS = 32
REL_MAX_DISTANCE = 128
N_CROSS_HEADS = 4
N_EXPERT_GROUPS = 4
EXPERTS_PER_GROUP = 4
N_EXPERTS = N_EXPERT_GROUPS * EXPERTS_PER_GROUP
LAMBDA_INIT = 0.8 - 0.6 * math.exp(-0.3 * 0)

LANES = 128
SUBLANES = 8
MXU_DIM = 256
VMEM_LIMIT_BYTES = 56 * 1024 * 1024

ATTN_BLOCK = 2 * MXU_DIM
FAR_BUCKET = REL_BUCKETS // 2 - 1
MASK_VALUE = -1e30
M_FINITE = float(jnp.finfo(jnp.float32).max)
LOG2E = math.log2(math.e)
V_AUG_ROWS = DIFF_DV + 16

ROUTER_ROWS = LANES
GROUP_ROW_STRIDE = SUBLANES
ROUTE_GROUP_LANE = GROUP_ROW_STRIDE
TOKEN_ROWS = 9


def _rms(x, g):
    return x * lax.rsqrt(jnp.mean(x * x, axis=-1, keepdims=True) + EPS) * g


def _dot(a, b):
    return jnp.dot(a, b, preferred_element_type=F32)


def _dot_nt(a, b):
    return lax.dot_general(a, b, (((1,), (1,)), ((), ())), preferred_element_type=F32)


def _in_proj_kernel(x_ref, g_ref, wqv_t_ref, wku_ref, qt_ref, vt_ref, k_ref, u_ref):
    hn = _rms(x_ref[...], g_ref[...]).astype(BF16)
    zt = _dot_nt(wqv_t_ref[...], hn)
    qt_ref[...] = (zt[:DIFF_WIDTH] * (DIFF_DK ** -0.5 * LOG2E)).astype(BF16)
    for c in range(vt_ref.shape[0]):
        vt_ref[c] = zt[DIFF_WIDTH:, c * ATTN_BLOCK:(c + 1) * ATTN_BLOCK].astype(BF16)
    z = _dot(hn, wku_ref[...])
    k_ref[...] = z[:, :DIFF_WIDTH].astype(BF16)
    u_ref[...] = z[:, DIFF_WIDTH:]


def _in_proj(x, attn_norm, wqv_t, wku, tm):
    B, S, D = x.shape
    nkb = tm // ATTN_BLOCK
    return pl.pallas_call(
        _in_proj_kernel,
        grid=(B, S // tm),
        in_specs=[
            pl.BlockSpec((None, tm, D), lambda b, i: (b, i, 0)),
            pl.BlockSpec((1, D), lambda b, i: (0, 0)),
            pl.BlockSpec(wqv_t.shape, lambda b, i: (0, 0)),
            pl.BlockSpec(wku.shape, lambda b, i: (0, 0)),
        ],
        out_specs=[
            pl.BlockSpec((None, DIFF_WIDTH, tm), lambda b, i: (b, 0, i)),
            pl.BlockSpec((None, nkb, DIFF_WIDTH, ATTN_BLOCK), lambda b, i: (b, i, 0, 0)),
            pl.BlockSpec((None, tm, DIFF_WIDTH), lambda b, i: (b, i, 0)),
            pl.BlockSpec((None, tm, POOL_WIDTH), lambda b, i: (b, i, 0)),
        ],
        out_shape=[
            jax.ShapeDtypeStruct((B, DIFF_WIDTH, S), BF16),
            jax.ShapeDtypeStruct((B, S // ATTN_BLOCK, DIFF_WIDTH, ATTN_BLOCK), BF16),
            jax.ShapeDtypeStruct((B, S, DIFF_WIDTH), BF16),
            jax.ShapeDtypeStruct((B, S, POOL_WIDTH), F32),
        ],
        compiler_params=pltpu.CompilerParams(
            dimension_semantics=("arbitrary", "arbitrary"), vmem_limit_bytes=VMEM_LIMIT_BYTES),
        name="in_proj",
    )(x, attn_norm, wqv_t, wku)


def _rel_bucket(rel):
    nb = REL_BUCKETS // 2
    max_exact = nb // 2
    ret = (rel > 0).astype(jnp.int32) * nb
    n = jnp.abs(rel)
    nf = jnp.maximum(n, 1).astype(jnp.float32)
    large = max_exact + (jnp.log(nf / max_exact) / math.log(REL_MAX_DISTANCE / max_exact)
                         * (nb - max_exact)).astype(jnp.int32)
    large = jnp.minimum(large, nb - 1)
    return ret + jnp.where(n < max_exact, n, large)


def _bias_tile_kernel(band_ref, out_ref):
    n = ATTN_BLOCK
    kk = lax.broadcasted_iota(jnp.int32, (n, n), 0)
    qq = lax.broadcasted_iota(jnp.int32, (n, n), 1)
    for d in range(2):
        band = jnp.broadcast_to(band_ref[d], (n, 2 * n))
        tile = pltpu.roll(band, n + 1, axis=1, stride=1, stride_axis=0)[:, :n]
        if d == 0:
            tile = jnp.where(kk // CHUNK <= qq // CHUNK, tile, MASK_VALUE)
        out_ref[d] = tile
    out_ref[2] = jnp.full((n, n), MASK_VALUE, F32)


def _bias_tiles(rel_bias):
    n = ATTN_BLOCK
    m = jnp.arange(2 * n, dtype=jnp.int32)
    rel = jnp.stack([n - 1 - m, -1 - m])
    band = (rel_bias[_rel_bucket(rel)] - rel_bias[FAR_BUCKET]) * LOG2E
    band = band.transpose(2, 0, 1)[:, :, None, :]
    return pl.pallas_call(
        _bias_tile_kernel,
        grid=(N_DIFF_HEADS,),
        in_specs=[pl.BlockSpec((None, 2, 1, 2 * n), lambda h: (h, 0, 0, 0))],
        out_specs=pl.BlockSpec((None, 3, n, n), lambda h: (h, 0, 0, 0)),
        out_shape=jax.ShapeDtypeStruct((N_DIFF_HEADS, 3, n, n), F32),
        name="bias_tiles",
    )(band)


def _attn_kernel(far_ref, qta_ref, qtb_ref, k_ref, vt_ref, dt_ref, lam_ref, g_ref, oa_ref, ob_ref,
                 kaug, vaug, qaug, qref, m_s, acc_s, s_buf, p_buf):
    h = pl.program_id(1)
    step = pl.program_id(2)
    nkb = vt_ref.shape[0]
    QB = ATTN_BLOCK
    KB = ATTN_BLOCK
    q_block = (step, nkb - 1 - step)

    @pl.when(step == 0)
    def _():
        lane = lax.broadcasted_iota(jnp.int32, (KB, LANES), 1)
        pad = jnp.where(lane == 0, far_ref[h, 0],
                        jnp.where(lane == 1, far_ref[h, 1],
                                  jnp.where(lane == 2, far_ref[h, 2],
                                            jnp.where(lane < 6, 1.0, 0.0)))).astype(BF16)
        vrow = lax.broadcasted_iota(jnp.int32, (V_AUG_ROWS - DIFF_DV, KB), 0)
        ones_row = jnp.where(vrow == 0, 1.0, 0.0).astype(BF16)

        def fill(c, carry):
            r = pl.multiple_of(c * KB, KB)
            kaug[pl.ds(r, KB), :LANES] = k_ref[pl.ds(r, KB), :]
            kaug[pl.ds(r, KB), LANES:] = pad
            vaug[c, :DIFF_DV] = vt_ref[c]
            vaug[c, DIFF_DV:] = ones_row
            return carry

        lax.fori_loop(0, nkb, fill, 0)

    zeros = jnp.zeros((DIFF_DK, QB), BF16)
    row = lax.broadcasted_iota(jnp.int32, (LANES, QB), 0)
    ones_blk = jnp.where(row < 3, 1.0, 0.0)

    def fill_query_rows(buf, w, q):
        buf[w, 0, :DIFF_DK] = q[:DIFF_DK]
        buf[w, 0, DIFF_DK:2 * DIFF_DK] = zeros
        buf[w, 1, :DIFF_DK] = zeros
        buf[w, 1, DIFF_DK:2 * DIFF_DK] = q[DIFF_DK:]

    for w, qt_ref in enumerate((qta_ref, qtb_ref)):
        q = qt_ref[...]
        fill_query_rows(qref, w, q)
        rd = pl.multiple_of(q_block[w] * KB, KB)
        k_own_t = kaug[pl.ds(rd, KB), :LANES].astype(F32).T
        qk = q.astype(F32) * k_own_t
        for half in range(2):
            m = jnp.sum(qk[half * DIFF_DK:(half + 1) * DIFF_DK], axis=0, keepdims=True) + far_ref[h, 3]
            m_s[w, half] = m
            nm = -m
            hi = nm.astype(BF16).astype(F32)
            mid = (nm - hi).astype(BF16).astype(F32)
            lo = (nm - hi - mid).astype(BF16).astype(F32)
            qref[w, half, 2 * DIFF_DK:] = jnp.where(
                row == 3, hi, jnp.where(row == 4, mid, jnp.where(row == 5, lo, ones_blk))).astype(BF16)
    acc_s[...] = jnp.zeros(acc_s.shape, F32)

    i_a, i_b = q_block
    near_a = (0, jnp.maximum(i_a - 1, 0))
    biased = ((0, i_a), near_a, (1, i_b), (1, i_b - 1))
    bias_tile = (0, jnp.where(i_a >= 1, 1, 2), 0, 1)
    n_far_a = jnp.maximum(i_a - 1, 0)
    n_far = n_far_a + i_b - 1

    def far_block(f):
        w = (f >= n_far_a).astype(jnp.int32)
        return w, f - w * n_far_a

    def probabilities(par, blk, bias):
        w, j = blk
        r = pl.multiple_of(j * KB, KB)
        kblk = kaug[pl.ds(r, KB), :]
        for half in range(2):
            s = _dot(kblk, qref[w, half])
            if bias is not None:
                s = s + dt_ref[bias]
            p_buf[par, half] = jnp.exp2(s.astype(BF16))

    def weighted_values(par, blk):
        w, j = blk
        vblk = vaug[j]
        for half in range(2):
            acc_s[w, half] += _dot(vblk, p_buf[par, half])

    probabilities(0, biased[0], bias_tile[0])
    for u in range(1, 4):
        weighted_values((u - 1) % 2, biased[u - 1])
        probabilities(u % 2, biased[u], bias_tile[u])

    def far_step(par, f):
        w_prev, j_prev = far_block(f - 1)
        first = f == 0
        weighted_values(1 - par, (jnp.where(first, biased[3][0], w_prev), jnp.where(first, biased[3][1], j_prev)))
        probabilities(par, far_block(f), None)

    def far_pair(g, carry):
        far_step(0, 2 * g)
        far_step(1, 2 * g + 1)
        return carry

    lax.fori_loop(0, n_far // 2, far_pair, 0)

    @pl.when(n_far % 2 == 1)
    def _():
        far_step(0, n_far - 1)

    w_last, j_last = far_block(n_far - 1)
    last = (jnp.where(n_far == 0, biased[3][0], w_last), jnp.where(n_far == 0, biased[3][1], j_last))
    for par in range(2):
        @pl.when((n_far + 1) % 2 == par)
        def _():
            weighted_values(par, last)

    lam = (jnp.exp(jnp.sum(lam_ref[0:1] * lam_ref[1:2], axis=-1, keepdims=True))
           - jnp.exp(jnp.sum(lam_ref[2:3] * lam_ref[3:4], axis=-1, keepdims=True)) + LAMBDA_INIT)

    def finalize(w, o_ref):
        l0 = acc_s[w, 0, DIFF_DV:DIFF_DV + 1]
        l1 = acc_s[w, 1, DIFF_DV:DIFF_DV + 1]
        ot = acc_s[w, 0, :DIFF_DV] / l0 - lam * (acc_s[w, 1, :DIFF_DV] / l1)
        inv = lax.rsqrt(jnp.mean(ot * ot, axis=0, keepdims=True) + EPS)
        o_ref[...] = (((ot * inv) * g_ref[...]) * (1.0 - LAMBDA_INIT)).T.astype(BF16)
        bad = jnp.where(jnp.abs(ot) <= M_FINITE, 0.0, 1.0) + jnp.where(jnp.maximum(l0, l1) <= M_FINITE, 0.0, 1.0)
        return jnp.max(bad) > 0.0

    for w, (qt_ref, o_ref) in enumerate(((qta_ref, oa_ref), (qtb_ref, ob_ref))):
        overflowed = finalize(w, o_ref)

        @pl.when(overflowed)
        def _():
            i_q = q_block[w]
            fill_query_rows(qaug, 0, qt_ref[...])
            qaug[0, 0, 2 * DIFF_DK:] = ones_blk.astype(BF16)
            qaug[0, 1, 2 * DIFF_DK:] = ones_blk.astype(BF16)
            acc_s[w] = jnp.zeros(acc_s.shape[1:], F32)

            def online_step(j, kind):
                r = pl.multiple_of(j * KB, KB)
                kblk = kaug[pl.ds(r, KB), :]
                vblk = vaug[j]
                for half in range(2):
                    s_buf[half] = _dot(kblk, qaug[0, half])
                    if kind is not None:
                        s_buf[half] += dt_ref[kind]
                    m_old = m_s[w, half]
                    m_new = jnp.maximum(m_old, jnp.max(s_buf[half], axis=0, keepdims=True))
                    p = jnp.exp2((s_buf[half] - m_new).astype(BF16))
                    acc_s[w, half] = jnp.exp2(m_old - m_new) * acc_s[w, half] + _dot(vblk, p)
                    m_s[w, half] = m_new

            online_step(i_q, 0)

            @pl.when(i_q >= 1)
            def _():
                online_step(i_q - 1, 1)

            def far_online(j, carry):
                online_step(j, None)
                return carry

            lax.fori_loop(0, jnp.maximum(i_q - 1, 0), far_online, 0)
            finalize(w, o_ref)


def _diff_attn(far_split, qt, k, vt, dtiles, lam_params, subln_cols):
    B, S, _ = k.shape
    nq = S // ATTN_BLOCK
    half_blocks = nq // 2
    out_half = jax.ShapeDtypeStruct((B, S // 2, DIFF_WIDTH), BF16)
    return pl.pallas_call(
        _attn_kernel,
        grid=(B, N_DIFF_HEADS, half_blocks),
        in_specs=[
            pl.BlockSpec(memory_space=pltpu.SMEM),
            pl.BlockSpec((None, 2 * DIFF_DK, ATTN_BLOCK), lambda b, h, s: (b, h, s)),
            pl.BlockSpec((None, 2 * DIFF_DK, ATTN_BLOCK), lambda b, h, s: (b, h, nq - 1 - s)),
            pl.BlockSpec((None, S, 2 * DIFF_DK), lambda b, h, s: (b, 0, h)),
            pl.BlockSpec((None, nq, DIFF_DV, ATTN_BLOCK), lambda b, h, s: (b, 0, h, 0)),
            pl.BlockSpec((None, 3, ATTN_BLOCK, ATTN_BLOCK), lambda b, h, s: (h, 0, 0, 0)),
            pl.BlockSpec(lam_params.shape, lambda b, h, s: (0, 0)),
            pl.BlockSpec((DIFF_DV, ATTN_BLOCK), lambda b, h, s: (0, 0)),
        ],
        out_specs=[
            pl.BlockSpec((None, ATTN_BLOCK, DIFF_DV), lambda b, h, s: (b, s, h)),
            pl.BlockSpec((None, ATTN_BLOCK, DIFF_DV), lambda b, h, s: (b, half_blocks - 1 - s, h)),
        ],
        out_shape=[out_half, out_half],
        scratch_shapes=[
            pltpu.VMEM((S, 2 * LANES), BF16),
            pltpu.VMEM((nq, V_AUG_ROWS, ATTN_BLOCK), BF16),
            pltpu.VMEM((1, 2, 2 * LANES, ATTN_BLOCK), BF16),
            pltpu.VMEM((2, 2, 2 * LANES, ATTN_BLOCK), BF16),
            pltpu.VMEM((2, 2, 1, ATTN_BLOCK), F32),
            pltpu.VMEM((2, 2, V_AUG_ROWS, ATTN_BLOCK), F32),
            pltpu.VMEM((2, ATTN_BLOCK, ATTN_BLOCK), F32),
            pltpu.VMEM((2, 2, ATTN_BLOCK, ATTN_BLOCK), BF16),
        ],
        compiler_params=pltpu.CompilerParams(
            dimension_semantics=("arbitrary", "arbitrary", "arbitrary"), vmem_limit_bytes=VMEM_LIMIT_BYTES),
        name="diff_attn",
    )(far_split, qt, qt, k, vt, dtiles, lam_params, subln_cols)


def _mem_kv_kernel(mem_ref, g_ref, wk_t_ref, wv_ref, kt_ref, v_ref):
    mn = _rms(mem_ref[...], g_ref[...]).astype(BF16)
    kt_ref[...] = _dot_nt(wk_t_ref[...], mn).astype(BF16)
    v_ref[...] = _dot(mn, wv_ref[...]).astype(BF16)


def _mem_kv(mem, mem_norm, wk_t, wv):
    B, M, D = mem.shape
    return pl.pallas_call(
        _mem_kv_kernel,
        grid=(B,),
        in_specs=[
            pl.BlockSpec((None, M, D), lambda b: (b, 0, 0)),
            pl.BlockSpec((1, D), lambda b: (0, 0)),
            pl.BlockSpec((D, D), lambda b: (0, 0)),
            pl.BlockSpec((D, D), lambda b: (0, 0)),
        ],
        out_specs=[
            pl.BlockSpec((None, D, M), lambda b: (b, 0, 0)),
            pl.BlockSpec((None, M, D), lambda b: (b, 0, 0)),
        ],
        out_shape=[jax.ShapeDtypeStruct((B, D, M), BF16), jax.ShapeDtypeStruct((B, M, D), BF16)],
        compiler_params=pltpu.CompilerParams(
            dimension_semantics=("arbitrary",), vmem_limit_bytes=VMEM_LIMIT_BYTES),
        name="mem_kv",
    )(mem, mem_norm, wk_t, wv)


def _first_argmax_rows(v, vmax):
    row = lax.broadcasted_iota(jnp.int32, v.shape, 0)
    return jnp.min(jnp.where(v == vmax, row, v.shape[0]), axis=0, keepdims=True)


def _mix_cross_kernel(alo_ref, ahi_ref, u_ref, uprev_ref, x_ref, wout_ref, pw_ref, ps_ref, cn_ref, wq_ref, kt_ref, v_ref,
                      wo_ref, fn_ref, wrh_ref, wrl_ref, rb_ref, h2c_ref):
    i = pl.program_id(1)
    tm = u_ref.shape[0]
    D = x_ref.shape[1]
    hd = kt_ref.shape[0] // N_CROSS_HEADS

    u = u_ref[...]
    prev = jnp.where(i > 0, uprev_ref[...], 0.0)
    pos1 = (i * tm + 1 + lax.broadcasted_iota(jnp.int32, (tm, 1), 0)).astype(F32)
    ys = []
    for g, w in enumerate(POOL_WINDOWS):
        sl = slice(g * POOL_DIM, (g + 1) * POOL_DIM)
        acc = jnp.concatenate([prev[:, sl], u[:, sl]], axis=0)
        span = 1
        while span < w:
            acc = acc[span:] + acc[:-span]
            span *= 2
        first = MAX_WINDOW - (w - 1)
        mean = acc[first:first + tm] / jnp.minimum(pos1, float(w))
        d = (mean - u[:, sl]).astype(BF16)
        ys.append(_dot(d, pw_ref[g]) * ps_ref[:, sl])
    p = jnp.concatenate(ys, axis=-1).astype(BF16)

    a = jnp.where(i < pl.num_programs(1) // 2, alo_ref[...], ahi_ref[...])
    h1 = x_ref[...] + _dot(jnp.concatenate([a, p], axis=-1), wout_ref[...])

    q = _dot(_rms(h1, cn_ref[...]).astype(BF16), wq_ref[...]).astype(BF16)
    outs = []
    for c in range(N_CROSS_HEADS):
        sl = slice(c * hd, (c + 1) * hd)
        s = _dot(q[:, sl], kt_ref[sl, :]) * (hd ** -0.5)
        e = jnp.exp(s - jnp.max(s, axis=-1, keepdims=True))
        pr = (e / jnp.sum(e, axis=-1, keepdims=True)).astype(BF16)
        outs.append(_dot(pr, v_ref[:, sl]))
    o = jnp.concatenate(outs, axis=-1).astype(BF16)
    h2 = h1 + _dot(o, wo_ref[...])
    h2c_ref[:, :TOKEN_ROWS - 1, :] = h2.reshape(tm, TOKEN_ROWS - 1, LANES)

    hf = _rms(h2, fn_ref[...])
    hf_hi = hf.astype(BF16)
    hf_lo = (hf - hf_hi.astype(F32)).astype(BF16)
    lg = (_dot_nt(wrh_ref[...], hf_hi) + _dot_nt(wrl_ref[...], hf_hi) + _dot_nt(wrh_ref[...], hf_lo)
          + rb_ref[...])
    R = GROUP_ROW_STRIDE
    gl = lg[:R]
    gmax = jnp.max(gl, axis=0, keepdims=True)
    gsel = _first_argmax_rows(gl, gmax)
    g_w = 1.0 / jnp.sum(jnp.exp(gl - gmax), axis=0, keepdims=True)
    el = lg[R:2 * R]
    for g in range(1, N_EXPERT_GROUPS):
        el = jnp.where(gsel == g, lg[R * (g + 1):R * (g + 2)], el)
    row = lax.broadcasted_iota(jnp.int32, el.shape, 0)
    v0 = jnp.max(el, axis=0, keepdims=True)
    i0 = _first_argmax_rows(el, v0)
    el1 = jnp.where(row == i0, -jnp.inf, el)
    v1 = jnp.max(el1, axis=0, keepdims=True)
    i1 = _first_argmax_rows(el1, v1)
    t = jnp.exp(v1 - v0)
    w0 = 1.0 / (1.0 + t)
    w1 = t / (1.0 + t)
    inner = jnp.where(row == i0, w0, 0.0) + jnp.where(row == i1, w1, 0.0)
    blocks = [g_w * inner, jnp.where(row == 0, gsel.astype(F32), 0.0),
              jnp.zeros((ROUTER_ROWS - 2 * R, tm), F32)]
    h2c_ref[:, TOKEN_ROWS - 1:, :] = jnp.concatenate(blocks, axis=0).T.reshape(tm, 1, LANES)


def _mix_cross(a_lo, a_hi, u, x, wout, pool_w, pool_scale, cross_norm, wq, kt, v, wo, ffn_norm, wr_hi, wr_lo, rbias, tm):
    B, S, D = x.shape
    M = v.shape[1]
    prev_blocks = tm // MAX_WINDOW
    half_tiles = S // tm // 2
    const2 = lambda b, i: (0, 0)
    return pl.pallas_call(
        _mix_cross_kernel,
        grid=(B, S // tm),
        in_specs=[
            pl.BlockSpec((None, tm, DIFF_WIDTH), lambda b, i: (b, jnp.minimum(i, half_tiles - 1), 0)),
            pl.BlockSpec((None, tm, DIFF_WIDTH), lambda b, i: (b, jnp.maximum(i - half_tiles, 0), 0)),
            pl.BlockSpec((None, tm, POOL_WIDTH), lambda b, i: (b, i, 0)),
            pl.BlockSpec((None, MAX_WINDOW, POOL_WIDTH), lambda b, i: (b, jnp.maximum(i * prev_blocks - 1, 0), 0)),
            pl.BlockSpec((None, tm, D), lambda b, i: (b, i, 0)),
            pl.BlockSpec(wout.shape, const2),
            pl.BlockSpec(pool_w.shape, lambda b, i: (0, 0, 0)),
            pl.BlockSpec((1, POOL_WIDTH), const2),
            pl.BlockSpec((1, D), const2),
            pl.BlockSpec(wq.shape, const2),
            pl.BlockSpec((None, D, M), lambda b, i: (b, 0, 0)),
            pl.BlockSpec((None, M, D), lambda b, i: (b, 0, 0)),
            pl.BlockSpec(wo.shape, const2),
            pl.BlockSpec((1, D), const2),
            pl.BlockSpec(wr_hi.shape, const2),
            pl.BlockSpec(wr_lo.shape, const2),
            pl.BlockSpec(rbias.shape, const2),
        ],
        out_specs=pl.BlockSpec((None, tm, TOKEN_ROWS, LANES), lambda b, i: (b, i, 0, 0)),
        out_shape=jax.ShapeDtypeStruct((B, S, TOKEN_ROWS, LANES), F32),
        compiler_params=pltpu.CompilerParams(
            dimension_semantics=("arbitrary", "arbitrary"), vmem_limit_bytes=VMEM_LIMIT_BYTES),
        name="mix_cross",
    )(a_lo, a_hi, u, u, x, wout, pool_w, pool_scale, cross_norm, wq, kt, v, wo, ffn_norm, wr_hi, wr_lo, rbias)


def _moe_kernel(glo_ref, ghi_ref, slot_ref, h2c_hbm, wg_hbm, wu_hbm, wd_hbm, fn_ref, wg_ref, wu_ref, wd_ref, gn_ref,
                y_hbm, src_ref, xbuf, ybuf, hf_s, acc_s, wg_x, wu_x, wd_x, gsem, ssem, wsem, fence_sem):
    c = pl.program_id(0)
    ntiles = pl.num_programs(0)
    tm, D = hf_s.shape
    slot = c % 2

    @pl.when(c == 0)
    def _():
        def invert(t, carry):
            src_ref[slot_ref[t]] = t
            return carry
        lax.fori_loop(0, slot_ref.shape[0], invert, 0, unroll=8)

    def row_gather(tile, r, s):
        t = src_ref[tile * tm + r]
        return pltpu.make_async_copy(h2c_hbm.at[t], xbuf.at[s, r], gsem.at[s])

    def row_scatter(tile, r, s):
        t = src_ref[tile * tm + r]
        return pltpu.make_async_copy(ybuf.at[s, pl.ds(r, 1)], y_hbm.at[pl.ds(t, 1)], ssem.at[s])

    def start_rows(make, tile):
        def body(r, carry):
            make(tile, r, tile % 2).start()
            return carry
        lax.fori_loop(0, tm, body, 0, unroll=8)

    def wait_gather(s):
        pltpu.make_async_copy(h2c_hbm.at[pl.ds(0, tm)], xbuf.at[s], gsem.at[s]).wait()

    def wait_scatter(s):
        pltpu.make_async_copy(ybuf.at[s], y_hbm.at[pl.ds(0, tm)], ssem.at[s]).wait()

    @pl.when(c == 0)
    def _():
        start_rows(row_gather, 0)

    wait_gather(slot)

    x = xbuf[slot]
    h2 = x[:, :TOKEN_ROWS - 1, :].reshape(tm, D)
    route = x[:, TOKEN_ROWS - 1, :]
    hf_s[...] = _rms(h2, fn_ref[...]).astype(BF16)
    group = route[:, ROUTE_GROUP_LANE:ROUTE_GROUP_LANE + 1]

    def expert_output(e, wg_e, wu_e, wd_e, in_group):
        hf = hf_s[...]
        hg = _dot(hf, wg_e.astype(BF16))
        hu = _dot(hf, wu_e.astype(BF16))
        cw = jnp.where(in_group, route[:, e:e + 1], 0.0)
        act = (hg * (1.0 / (1.0 + jnp.exp(-hg))) * hu * cw).astype(BF16)
        return _dot(act, wd_e.astype(BF16))

    def first_pass(g, neighbour_slot=None):
        in_group = group == g.astype(F32)
        rows_per_expert = tm // EXPERTS_PER_GROUP
        for e in range(EXPERTS_PER_GROUP):
            if neighbour_slot is not None:
                for r in range(e * rows_per_expert, (e + 1) * rows_per_expert):
                    row_gather(c + 1, r, neighbour_slot).start(priority=r % 2)
                    row_scatter(c - 1, r, neighbour_slot).start(priority=(r + 1) % 2)
            part = expert_output(e, wg_ref[e], wu_ref[e], wd_ref[e], in_group)
            if e == 0:
                acc_s[...] = part
            else:
                acc_s[...] += part
            if neighbour_slot is not None:
                pl.semaphore_signal(fence_sem, 1)
                pl.semaphore_wait(fence_sem, 1)

    interior = (c >= 1) & (c + 1 < ntiles)
    for par in range(2):
        @pl.when(interior & (slot == par))
        def _():
            first_pass(glo_ref[c], neighbour_slot=1 - par)

    @pl.when(jnp.logical_not(interior))
    def _():
        @pl.when(c + 1 < ntiles)
        def _():
            start_rows(row_gather, c + 1)

        @pl.when(c >= 1)
        def _():
            start_rows(row_scatter, c - 1)

        first_pass(glo_ref[c])

    def extra_pass(g, carry):
        in_group = group == g.astype(F32)
        for e in range(EXPERTS_PER_GROUP):
            copies = [pltpu.make_async_copy(w_hbm.at[g * EXPERTS_PER_GROUP + e], w_x, wsem.at[n])
                      for n, (w_hbm, w_x) in enumerate(((wg_hbm, wg_x), (wu_hbm, wu_x), (wd_hbm, wd_x)))]
            for cp in copies:
                cp.start()
            for cp in copies:
                cp.wait()
            acc_s[...] += expert_output(e, wg_x[...], wu_x[...], wd_x[...], in_group)
        return carry

    lax.fori_loop(glo_ref[c] + 1, ghi_ref[c] + 1, extra_pass, 0)

    @pl.when(c >= 2)
    def _():
        wait_scatter(slot)

    ybuf[slot] = _rms(h2 + acc_s[...], gn_ref[...])

    @pl.when(c == ntiles - 1)
    def _():
        start_rows(row_scatter, c)
        wait_scatter(slot)

        @pl.when(ntiles >= 2)
        def _():
            wait_scatter(1 - slot)


def _moe(h2c, tile_glo, tile_ghi, slot, ffn_norm, wg, wu, wd, final_norm, tm):
    T = h2c.shape[0]
    D = (TOKEN_ROWS - 1) * LANES
    FF = wg.shape[-1]
    E = EXPERTS_PER_GROUP
    any_spec = pl.BlockSpec(memory_space=pl.ANY)
    grid_spec = pltpu.PrefetchScalarGridSpec(
        num_scalar_prefetch=3,
        grid=(T // tm,),
        in_specs=[
            any_spec, any_spec, any_spec, any_spec,
            pl.BlockSpec((1, D), lambda c, glo, ghi, src: (0, 0)),
            pl.BlockSpec((E, D, FF), lambda c, glo, ghi, src: (glo[c], 0, 0)),
            pl.BlockSpec((E, D, FF), lambda c, glo, ghi, src: (glo[c], 0, 0)),
            pl.BlockSpec((E, FF, D), lambda c, glo, ghi, src: (glo[c], 0, 0)),
            pl.BlockSpec((1, D), lambda c, glo, ghi, src: (0, 0)),
        ],
        out_specs=any_spec,
        scratch_shapes=[
            pltpu.SMEM((T,), jnp.int32),
            pltpu.VMEM((2, tm, TOKEN_ROWS, LANES), F32),
            pltpu.VMEM((2, tm, D), F32),
            pltpu.VMEM((tm, D), BF16),
            pltpu.VMEM((tm, D), F32),
            pltpu.VMEM((D, FF), wg.dtype),
            pltpu.VMEM((D, FF), wu.dtype),
            pltpu.VMEM((FF, D), wd.dtype),
            pltpu.SemaphoreType.DMA((2,)),
            pltpu.SemaphoreType.DMA((2,)),
            pltpu.SemaphoreType.DMA((3,)),
            pltpu.SemaphoreType.REGULAR,
        ],
    )
    return pl.pallas_call(
        _moe_kernel,
        grid_spec=grid_spec,
        out_shape=jax.ShapeDtypeStruct((T, D), F32),
        compiler_params=pltpu.CompilerParams(
            dimension_semantics=("arbitrary",), vmem_limit_bytes=VMEM_LIMIT_BYTES),
        name="moe",
    )(tile_glo, tile_ghi, slot, h2c, wg, wu, wd, ffn_norm, wg, wu, wd, final_norm)


def _group_sort_plan(group, tm):
    T = group.shape[0]
    onehot = (group[:, None] == jnp.arange(N_EXPERT_GROUPS, dtype=jnp.int32)[None, :]).astype(jnp.int32)
    ranks = jnp.cumsum(onehot, axis=0) - onehot
    counts = jnp.sum(onehot, axis=0)
    starts = jnp.cumsum(counts) - counts
    slot = jnp.sum(onehot * (ranks + starts[None, :]), axis=1)
    first = jnp.arange(0, T, tm, dtype=jnp.int32)
    group_at = lambda pos: jnp.sum((starts[None, 1:] <= pos[:, None]).astype(jnp.int32), axis=1)
    return slot, group_at(first), group_at(first + (tm - 1))


def _split3_bf16(c):
    hi = c.astype(BF16).astype(F32)
    mid = (c - hi).astype(BF16).astype(F32)
    lo = (c - hi - mid).astype(BF16).astype(F32)
    return jnp.stack([hi, mid, lo], axis=-1)


def _router_operands(router_group, router_group_bias, router_expert, router_expert_bias):
    D = router_group.shape[0]
    R = GROUP_ROW_STRIDE
    w = jnp.zeros((ROUTER_ROWS, D), F32)
    bias = jnp.zeros((ROUTER_ROWS,), F32)
    slab_pad = jnp.full((R - N_EXPERT_GROUPS,), MASK_VALUE, F32)
    w = w.at[:N_EXPERT_GROUPS].set(router_group.T)
    bias = bias.at[:R].set(jnp.concatenate([router_group_bias, slab_pad]))
    for g in range(N_EXPERT_GROUPS):
        w = w.at[R * (g + 1):R * (g + 1) + EXPERTS_PER_GROUP].set(router_expert[g].T)
        bias = bias.at[R * (g + 1):R * (g + 2)].set(jnp.concatenate([router_expert_bias[g], slab_pad]))
    w_hi = w.astype(BF16)
    w_lo = (w - w_hi.astype(F32)).astype(BF16)
    return w_hi, w_lo, bias[:, None]


def kernel(x, mem, rel_bias, attn_norm, w_in, lambda_q1, lambda_k1, lambda_q2, lambda_k2, diff_subln, pool_w,
           pool_scale, w_out, cross_norm, mem_norm, wq_cross, wkv_cross, wo_cross, ffn_norm, router_group,
           router_group_bias, router_expert, router_expert_bias, w_gate, w_up, w_down, final_norm):
    B, S, D = x.shape
    layer = 0
    w_in_l = w_in[layer]
    wqv_t = jnp.concatenate([w_in_l[:, :DIFF_WIDTH], w_in_l[:, 2 * DIFF_WIDTH:3 * DIFF_WIDTH]], axis=1).T.astype(BF16)
    wku = jnp.concatenate([w_in_l[:, DIFF_WIDTH:2 * DIFF_WIDTH], w_in_l[:, 3 * DIFF_WIDTH:]], axis=1).astype(BF16)
    qt, vt, k, u = _in_proj(x, attn_norm[layer][None], wqv_t, wku, tm=512)

    dtiles = _bias_tiles(rel_bias)
    far_split = jnp.concatenate([_split3_bf16(rel_bias[FAR_BUCKET] * LOG2E), rel_bias[0][:, None] * LOG2E],
                                axis=1)
    lam_params = jnp.stack([lambda_q1[layer], lambda_k1[layer], lambda_q2[layer], lambda_k2[layer]])
    subln_cols = jnp.broadcast_to(diff_subln[layer][:, None], (DIFF_DV, ATTN_BLOCK))
    a_lo, a_hi = _diff_attn(far_split, qt, k, vt, dtiles, lam_params, subln_cols)

    wkv = wkv_cross[layer]
    kt, v = _mem_kv(mem, mem_norm[layer][None], wkv[:, :D].T.astype(BF16), wkv[:, D:].astype(BF16))

    wr_hi, wr_lo, rbias = _router_operands(router_group[layer], router_group_bias[layer],
                                           router_expert[layer], router_expert_bias[layer])
    h2c = _mix_cross(a_lo, a_hi, u, x, w_out[layer].astype(BF16), pool_w[layer].astype(BF16), pool_scale[layer][None],
                     cross_norm[layer][None], wq_cross[layer].astype(BF16), kt, v,
                     wo_cross[layer].astype(BF16), ffn_norm[layer][None], wr_hi, wr_lo, rbias, tm=512)
    h2c = h2c.reshape(B * S, TOKEN_ROWS, LANES)

    moe_tm = 512
    slot, tile_glo, tile_ghi = _group_sort_plan(h2c[:, TOKEN_ROWS - 1, ROUTE_GROUP_LANE].astype(jnp.int32), moe_tm)
    y = _moe(h2c, tile_glo, tile_ghi, slot, ffn_norm[layer][None], w_gate[layer], w_up[layer], w_down[layer],
             final_norm[None], moe_tm)
    return y.reshape(B, S, D)
```

```python
import functools
import math

import jax
import jax.numpy as jnp
from jax import lax
from jax.experimental import pallas as pl
from jax.experimental.pallas import tpu as pltpu

F32 = jnp.float32
BF16 = jnp.bfloat16

EPS = 1e-6
CHUNK = 64
N_DIFF_HEADS = 4
DIFF_DK = 64
DIFF_DV = 128
DIFF_WIDTH = N_DIFF_HEADS * DIFF_DV
POOL_WINDOWS = (2, 4, 8, 16)
POOL_DIM = 128
POOL_WIDTH = len(POOL_WINDOWS) * POOL_DIM
MAX_WINDOW = max(POOL_WINDOWS)
REL_BUCKETS = 32
REL_MAX_DISTANCE = 128
N_CROSS_HEADS = 4
N_EXPERT_GROUPS = 4
EXPERTS_PER_GROUP = 4
N_EXPERTS = N_EXPERT_GROUPS * EXPERTS_PER_GROUP
LAMBDA_INIT = 0.8 - 0.6 * math.exp(-0.3 * 0)

LANES = 128
SUBLANES = 8
MXU_DIM = 256
VMEM_LIMIT_BYTES = 56 * 1024 * 1024

ATTN_BLOCK = 2 * MXU_DIM
FAR_BUCKET = REL_BUCKETS // 2 - 1
MASK_VALUE = -1e30
M_FINITE = float(jnp.finfo(jnp.float32).max)
LOG2E = math.log2(math.e)
V_AUG_ROWS = DIFF_DV + 16

ROUTER_ROWS = LANES // 2
GROUP_ROW_STRIDE = SUBLANES
ROUTE_GROUP_LANE = GROUP_ROW_STRIDE


def _rms(x, g):
    return x * lax.rsqrt(jnp.mean(x * x, axis=-1, keepdims=True) + EPS) * g


def _dot(a, b):
    return jnp.dot(a, b, preferred_element_type=F32)


def _dot_nt(a, b):
    return lax.dot_general(a, b, (((1,), (1,)), ((), ())), preferred_element_type=F32)


def _in_proj_kernel(x_ref, g_ref, win_ref, qt_ref, vt_ref, k_ref, u_ref, wqv_t_s, wku_s):
    @pl.when((pl.program_id(0) == 0) & (pl.program_id(1) == 0))
    def _():
        w = win_ref[...]
        wqv_t_s[:DIFF_WIDTH] = w[:, :DIFF_WIDTH].T.astype(BF16)
        wqv_t_s[DIFF_WIDTH:] = w[:, 2 * DIFF_WIDTH:3 * DIFF_WIDTH].T.astype(BF16)
        wku_s[:, :DIFF_WIDTH] = w[:, DIFF_WIDTH:2 * DIFF_WIDTH].astype(BF16)
        wku_s[:, DIFF_WIDTH:] = w[:, 3 * DIFF_WIDTH:].astype(BF16)

    hn = _rms(x_ref[...], g_ref[...]).astype(BF16)
    zt = _dot_nt(wqv_t_s[...], hn)
    qt_ref[...] = (zt[:DIFF_WIDTH] * (DIFF_DK ** -0.5 * LOG2E)).astype(BF16)
    for c in range(vt_ref.shape[0]):
        vt_ref[c] = zt[DIFF_WIDTH:, c * ATTN_BLOCK:(c + 1) * ATTN_BLOCK].astype(BF16)
    z = _dot(hn, wku_s[...])
    k_ref[...] = z[:, :DIFF_WIDTH].astype(BF16)
    u_ref[...] = z[:, DIFF_WIDTH:]


def _in_proj(x, attn_norm, w_in, tm):
    B, S, D = x.shape
    nkb = tm // ATTN_BLOCK
    return pl.pallas_call(
        _in_proj_kernel,
        grid=(B, S // tm),
        in_specs=[
            pl.BlockSpec((None, tm, D), lambda b, i: (b, i, 0)),
            pl.BlockSpec((1, D), lambda b, i: (0, 0)),
            pl.BlockSpec(w_in.shape, lambda b, i: (0, 0)),
        ],
        out_specs=[
            pl.BlockSpec((None, DIFF_WIDTH, tm), lambda b, i: (b, 0, i)),
            pl.BlockSpec((None, nkb, DIFF_WIDTH, ATTN_BLOCK), lambda b, i: (b, i, 0, 0)),
            pl.BlockSpec((None, tm, DIFF_WIDTH), lambda b, i: (b, i, 0)),
            pl.BlockSpec((None, tm, POOL_WIDTH), lambda b, i: (b, i, 0)),
        ],
        out_shape=[
            jax.ShapeDtypeStruct((B, DIFF_WIDTH, S), BF16),
            jax.ShapeDtypeStruct((B, S // ATTN_BLOCK, DIFF_WIDTH, ATTN_BLOCK), BF16),
            jax.ShapeDtypeStruct((B, S, DIFF_WIDTH), BF16),
            jax.ShapeDtypeStruct((B, S, POOL_WIDTH), F32),
        ],
        scratch_shapes=[pltpu.VMEM((2 * DIFF_WIDTH, D), BF16), pltpu.VMEM((D, DIFF_WIDTH + POOL_WIDTH), BF16)],
        compiler_params=pltpu.CompilerParams(
            dimension_semantics=("arbitrary", "arbitrary"), vmem_limit_bytes=VMEM_LIMIT_BYTES),
        name="in_proj",
    )(x, attn_norm, w_in)


def _rel_bucket(rel):
    nb = REL_BUCKETS // 2
    max_exact = nb // 2
    ret = (rel > 0).astype(jnp.int32) * nb
    n = jnp.abs(rel)
    nf = jnp.maximum(n, 1).astype(jnp.float32)
    large = max_exact + (jnp.log(nf / max_exact) / math.log(REL_MAX_DISTANCE / max_exact)
                         * (nb - max_exact)).astype(jnp.int32)
    large = jnp.minimum(large, nb - 1)
    return ret + jnp.where(n < max_exact, n, large)


def _bias_tile_kernel(band_ref, out_ref):
    n = ATTN_BLOCK
    kk = lax.broadcasted_iota(jnp.int32, (n, n), 0)
    qq = lax.broadcasted_iota(jnp.int32, (n, n), 1)
    for d in range(2):
        band = jnp.broadcast_to(band_ref[d], (n, 2 * n))
        tile = pltpu.roll(band, n + 1, axis=1, stride=1, stride_axis=0)[:, :n]
        if d == 0:
            tile = jnp.where(kk // CHUNK <= qq // CHUNK, tile, MASK_VALUE)
        out_ref[d] = tile
    out_ref[2] = jnp.full((n, n), MASK_VALUE, F32)


def _bias_tiles(rel_bias):
    n = ATTN_BLOCK
    m = jnp.arange(2 * n, dtype=jnp.int32)
    rel = jnp.stack([n - 1 - m, -1 - m])
    band = (rel_bias[_rel_bucket(rel)] - rel_bias[FAR_BUCKET]) * LOG2E
    band = band.transpose(2, 0, 1)[:, :, None, :]
    return pl.pallas_call(
        _bias_tile_kernel,
        grid=(N_DIFF_HEADS,),
        in_specs=[pl.BlockSpec((None, 2, 1, 2 * n), lambda h: (h, 0, 0, 0))],
        out_specs=pl.BlockSpec((None, 3, n, n), lambda h: (h, 0, 0, 0)),
        out_shape=jax.ShapeDtypeStruct((N_DIFF_HEADS, 3, n, n), F32),
        name="bias_tiles",
    )(band)


def _attn_kernel(far_ref, qta_ref, qtb_ref, k_ref, vt_ref, dt_ref, lam_ref, g_ref, oa_ref, ob_ref,
                 kaug, vaug, qaug, qref, m_s, acc_s, s_buf, p_buf):
    h = pl.program_id(1)
    step = pl.program_id(2)
    nkb = vt_ref.shape[0]
    QB = ATTN_BLOCK
    KB = ATTN_BLOCK
    q_block = (step, nkb - 1 - step)

    @pl.when(step == 0)
    def _():
        lane = lax.broadcasted_iota(jnp.int32, (KB, LANES), 1)
        pad = jnp.where(lane == 0, far_ref[h, 0],
                        jnp.where(lane == 1, far_ref[h, 1],
                                  jnp.where(lane == 2, far_ref[h, 2],
                                            jnp.where(lane < 6, 1.0, 0.0)))).astype(BF16)
        vrow = lax.broadcasted_iota(jnp.int32, (V_AUG_ROWS - DIFF_DV, KB), 0)
        ones_row = jnp.where(vrow == 0, 1.0, 0.0).astype(BF16)

        def fill(c, carry):
            r = pl.multiple_of(c * KB, KB)
            kaug[pl.ds(r, KB), :LANES] = k_ref[pl.ds(r, KB), :]
            kaug[pl.ds(r, KB), LANES:] = pad
            vaug[c, :DIFF_DV] = vt_ref[c]
            vaug[c, DIFF_DV:] = ones_row
            return carry

        lax.fori_loop(0, nkb, fill, 0)

    zeros = jnp.zeros((DIFF_DK, QB), BF16)
    row = lax.broadcasted_iota(jnp.int32, (LANES, QB), 0)
    ones_blk = jnp.where(row < 3, 1.0, 0.0)

    def fill_query_rows(buf, w, q):
        buf[w, 0, :DIFF_DK] = q[:DIFF_DK]
        buf[w, 0, DIFF_DK:2 * DIFF_DK] = zeros
        buf[w, 1, :DIFF_DK] = zeros
        buf[w, 1, DIFF_DK:2 * DIFF_DK] = q[DIFF_DK:]

    for w, qt_ref in enumerate((qta_ref, qtb_ref)):
        q = qt_ref[...]
        fill_query_rows(qref, w, q)
        rd = pl.multiple_of(q_block[w] * KB, KB)
        k_own_t = kaug[pl.ds(rd, KB), :LANES].astype(F32).T
        qk = q.astype(F32) * k_own_t
        for half in range(2):
            m = jnp.sum(qk[half * DIFF_DK:(half + 1) * DIFF_DK], axis=0, keepdims=True) + far_ref[h, 3]
            m_s[w, half] = m
            nm = -m
            hi = nm.astype(BF16).astype(F32)
            mid = (nm - hi).astype(BF16).astype(F32)
            lo = (nm - hi - mid).astype(BF16).astype(F32)
            qref[w, half, 2 * DIFF_DK:] = jnp.where(
                row == 3, hi, jnp.where(row == 4, mid, jnp.where(row == 5, lo, ones_blk))).astype(BF16)
    acc_s[...] = jnp.zeros(acc_s.shape, F32)

    i_a, i_b = q_block
    near_a = (0, jnp.maximum(i_a - 1, 0))
    biased = ((0, i_a), near_a, (1, i_b), (1, i_b - 1))
    bias_tile = (0, jnp.where(i_a >= 1, 1, 2), 0, 1)
    n_far_a = jnp.maximum(i_a - 1, 0)
    n_far = n_far_a + i_b - 1

    def far_block(f):
        w = (f >= n_far_a).astype(jnp.int32)
        return w, f - w * n_far_a

    def probabilities(par, blk, bias):
        w, j = blk
        r = pl.multiple_of(j * KB, KB)
        kblk = kaug[pl.ds(r, KB), :]
        for half in range(2):
            s = _dot(kblk, qref[w, half])
            if bias is not None:
                s = s + dt_ref[bias]
            p_buf[par, half] = jnp.exp2(s.astype(BF16))

    def weighted_values(par, blk):
        w, j = blk
        vblk = vaug[j]
        for half in range(2):
            acc_s[w, half] += _dot(vblk, p_buf[par, half])

    probabilities(0, biased[0], bias_tile[0])
    for u in range(1, 4):
        weighted_values((u - 1) % 2, biased[u - 1])
        probabilities(u % 2, biased[u], bias_tile[u])

    def far_step(par, f):
        w_prev, j_prev = far_block(f - 1)
        first = f == 0
        weighted_values(1 - par, (jnp.where(first, biased[3][0], w_prev), jnp.where(first, biased[3][1], j_prev)))
        probabilities(par, far_block(f), None)

    def far_pair(g, carry):
        far_step(0, 2 * g)
        far_step(1, 2 * g + 1)
        return carry

    lax.fori_loop(0, n_far // 2, far_pair, 0)

    @pl.when(n_far % 2 == 1)
    def _():
        far_step(0, n_far - 1)

    w_last, j_last = far_block(n_far - 1)
    last = (jnp.where(n_far == 0, biased[3][0], w_last), jnp.where(n_far == 0, biased[3][1], j_last))
    for par in range(2):
        @pl.when((n_far + 1) % 2 == par)
        def _():
            weighted_values(par, last)

    lam = (jnp.exp(jnp.sum(lam_ref[0:1] * lam_ref[1:2], axis=-1, keepdims=True))
           - jnp.exp(jnp.sum(lam_ref[2:3] * lam_ref[3:4], axis=-1, keepdims=True)) + LAMBDA_INIT)

    def finalize(w, o_ref):
        l0 = acc_s[w, 0, DIFF_DV:DIFF_DV + 1]
        l1 = acc_s[w, 1, DIFF_DV:DIFF_DV + 1]
        ot = acc_s[w, 0, :DIFF_DV] / l0 - lam * (acc_s[w, 1, :DIFF_DV] / l1)
        inv = lax.rsqrt(jnp.mean(ot * ot, axis=0, keepdims=True) + EPS)
        o_ref[...] = (((ot * inv) * g_ref[...]) * (1.0 - LAMBDA_INIT)).T.astype(BF16)
        bad = jnp.where(jnp.abs(ot) <= M_FINITE, 0.0, 1.0) + jnp.where(jnp.maximum(l0, l1) <= M_FINITE, 0.0, 1.0)
        return jnp.max(bad) > 0.0

    for w, (qt_ref, o_ref) in enumerate(((qta_ref, oa_ref), (qtb_ref, ob_ref))):
        overflowed = finalize(w, o_ref)

        @pl.when(overflowed)
        def _():
            i_q = q_block[w]
            fill_query_rows(qaug, 0, qt_ref[...])
            qaug[0, 0, 2 * DIFF_DK:] = ones_blk.astype(BF16)
            qaug[0, 1, 2 * DIFF_DK:] = ones_blk.astype(BF16)
            acc_s[w] = jnp.zeros(acc_s.shape[1:], F32)

            def online_step(j, kind):
                r = pl.multiple_of(j * KB, KB)
                kblk = kaug[pl.ds(r, KB), :]
                vblk = vaug[j]
                for half in range(2):
                    s_buf[half] = _dot(kblk, qaug[0, half])
                    if kind is not None:
                        s_buf[half] += dt_ref[kind]
                    m_old = m_s[w, half]
                    m_new = jnp.maximum(m_old, jnp.max(s_buf[half], axis=0, keepdims=True))
                    p = jnp.exp2((s_buf[half] - m_new).astype(BF16))
                    acc_s[w, half] = jnp.exp2(m_old - m_new) * acc_s[w, half] + _dot(vblk, p)
                    m_s[w, half] = m_new

            online_step(i_q, 0)

            @pl.when(i_q >= 1)
            def _():
                online_step(i_q - 1, 1)

            def far_online(j, carry):
                online_step(j, None)
                return carry

            lax.fori_loop(0, jnp.maximum(i_q - 1, 0), far_online, 0)
            finalize(w, o_ref)


def _diff_attn(far_split, qt, k, vt, dtiles, lam_params, subln_cols):
    B, S, _ = k.shape
    nq = S // ATTN_BLOCK
    half_blocks = nq // 2
    out_half = jax.ShapeDtypeStruct((B, S // 2, DIFF_WIDTH), BF16)
    return pl.pallas_call(
        _attn_kernel,
        grid=(B, N_DIFF_HEADS, half_blocks),
        in_specs=[
            pl.BlockSpec(memory_space=pltpu.SMEM),
            pl.BlockSpec((None, 2 * DIFF_DK, ATTN_BLOCK), lambda b, h, s: (b, h, s)),
            pl.BlockSpec((None, 2 * DIFF_DK, ATTN_BLOCK), lambda b, h, s: (b, h, nq - 1 - s)),
            pl.BlockSpec((None, S, 2 * DIFF_DK), lambda b, h, s: (b, 0, h)),
            pl.BlockSpec((None, nq, DIFF_DV, ATTN_BLOCK), lambda b, h, s: (b, 0, h, 0)),
            pl.BlockSpec((None, 3, ATTN_BLOCK, ATTN_BLOCK), lambda b, h, s: (h, 0, 0, 0)),
            pl.BlockSpec(lam_params.shape, lambda b, h, s: (0, 0)),
            pl.BlockSpec((DIFF_DV, ATTN_BLOCK), lambda b, h, s: (0, 0)),
        ],
        out_specs=[
            pl.BlockSpec((None, ATTN_BLOCK, DIFF_DV), lambda b, h, s: (b, s, h)),
            pl.BlockSpec((None, ATTN_BLOCK, DIFF_DV), lambda b, h, s: (b, half_blocks - 1 - s, h)),
        ],
        out_shape=[out_half, out_half],
        scratch_shapes=[
            pltpu.VMEM((S, 2 * LANES), BF16),
            pltpu.VMEM((nq, V_AUG_ROWS, ATTN_BLOCK), BF16),
            pltpu.VMEM((1, 2, 2 * LANES, ATTN_BLOCK), BF16),
            pltpu.VMEM((2, 2, 2 * LANES, ATTN_BLOCK), BF16),
            pltpu.VMEM((2, 2, 1, ATTN_BLOCK), F32),
            pltpu.VMEM((2, 2, V_AUG_ROWS, ATTN_BLOCK), F32),
            pltpu.VMEM((2, ATTN_BLOCK, ATTN_BLOCK), F32),
            pltpu.VMEM((2, 2, ATTN_BLOCK, ATTN_BLOCK), BF16),
        ],
        compiler_params=pltpu.CompilerParams(
            dimension_semantics=("arbitrary", "arbitrary", "arbitrary"), vmem_limit_bytes=VMEM_LIMIT_BYTES),
        name="diff_attn",
    )(far_split, qt, qt, k, vt, dtiles, lam_params, subln_cols)


def _mem_kv_kernel(mem_ref, g_ref, wk_t_ref, wv_ref, kt_ref, v_ref):
    mn = _rms(mem_ref[...], g_ref[...]).astype(BF16)
    kt_ref[...] = _dot_nt(wk_t_ref[...], mn).astype(BF16)
    v_ref[...] = _dot(mn, wv_ref[...]).astype(BF16)


def _mem_kv(mem, mem_norm, wk_t, wv):
    B, M, D = mem.shape
    return pl.pallas_call(
        _mem_kv_kernel,
        grid=(B,),
        in_specs=[
            pl.BlockSpec((None, M, D), lambda b: (b, 0, 0)),
            pl.BlockSpec((1, D), lambda b: (0, 0)),
            pl.BlockSpec((D, D), lambda b: (0, 0)),
            pl.BlockSpec((D, D), lambda b: (0, 0)),
        ],
        out_specs=[
            pl.BlockSpec((None, D, M), lambda b: (b, 0, 0)),
            pl.BlockSpec((None, M, D), lambda b: (b, 0, 0)),
        ],
        out_shape=[jax.ShapeDtypeStruct((B, D, M), BF16), jax.ShapeDtypeStruct((B, M, D), BF16)],
        compiler_params=pltpu.CompilerParams(
            dimension_semantics=("arbitrary",), vmem_limit_bytes=VMEM_LIMIT_BYTES),
        name="mem_kv",
    )(mem, mem_norm, wk_t, wv)


def _first_argmax_rows(v, vmax):
    row = lax.broadcasted_iota(jnp.int32, v.shape, 0)
    return jnp.min(jnp.where(v == vmax, row, v.shape[0]), axis=0, keepdims=True)


def _mix_cross_kernel(alo_ref, ahi_ref, u_ref, uprev_ref, x_ref, wout_ref, pw_ref, ps_ref, cn_ref, wq_ref, kt_ref, v_ref,
                      wo_ref, fn_ref, wrh_ref, wrl_ref, rb_ref, h2c_ref):
    i = pl.program_id(1)
    tm = u_ref.shape[0]
    D = x_ref.shape[1]
    hd = kt_ref.shape[0] // N_CROSS_HEADS

    u = u_ref[...]
    prev = jnp.where(i > 0, uprev_ref[...], 0.0)
    pos1 = (i * tm + 1 + lax.broadcasted_iota(jnp.int32, (tm, 1), 0)).astype(F32)
    ys = []
    for g, w in enumerate(POOL_WINDOWS):
        sl = slice(g * POOL_DIM, (g + 1) * POOL_DIM)
        acc = jnp.concatenate([prev[:, sl], u[:, sl]], axis=0)
        span = 1
        while span < w:
            acc = acc[span:] + acc[:-span]
            span *= 2
        first = MAX_WINDOW - (w - 1)
        mean = acc[first:first + tm] / jnp.minimum(pos1, float(w))
        d = (mean - u[:, sl]).astype(BF16)
        ys.append(_dot(d, pw_ref[g]) * ps_ref[:, sl])
    p = jnp.concatenate(ys, axis=-1).astype(BF16)

    a = jnp.where(i < pl.num_programs(1) // 2, alo_ref[...], ahi_ref[...])
    h1 = x_ref[...] + _dot(jnp.concatenate([a, p], axis=-1), wout_ref[...])

    q = _dot(_rms(h1, cn_ref[...]).astype(BF16), wq_ref[...]).astype(BF16)
    outs = []
    for c in range(N_CROSS_HEADS):
        sl = slice(c * hd, (c + 1) * hd)
        s = _dot(q[:, sl], kt_ref[sl, :]) * (hd ** -0.5)
        e = jnp.exp(s - jnp.max(s, axis=-1, keepdims=True))
        pr = (e / jnp.sum(e, axis=-1, keepdims=True)).astype(BF16)
        outs.append(_dot(pr, v_ref[:, sl]))
    o = jnp.concatenate(outs, axis=-1).astype(BF16)
    h2 = h1 + _dot(o, wo_ref[...])
    h2c_ref[:, :D] = h2

    hf = _rms(h2, fn_ref[...])
    hf_hi = hf.astype(BF16)
    hf_lo = (hf - hf_hi.astype(F32)).astype(BF16)
    lg = (_dot_nt(wrh_ref[...], hf_hi) + _dot_nt(wrl_ref[...], hf_hi) + _dot_nt(wrh_ref[...], hf_lo)
          + rb_ref[...])
    R = GROUP_ROW_STRIDE
    gl = lg[:R]
    gmax = jnp.max(gl, axis=0, keepdims=True)
    gsel = _first_argmax_rows(gl, gmax)
    g_w = 1.0 / jnp.sum(jnp.exp(gl - gmax), axis=0, keepdims=True)
    el = lg[R:2 * R]
    for g in range(1, N_EXPERT_GROUPS):
        el = jnp.where(gsel == g, lg[R * (g + 1):R * (g + 2)], el)
    row = lax.broadcasted_iota(jnp.int32, el.shape, 0)
    v0 = jnp.max(el, axis=0, keepdims=True)
    i0 = _first_argmax_rows(el, v0)
    el1 = jnp.where(row == i0, -jnp.inf, el)
    v1 = jnp.max(el1, axis=0, keepdims=True)
    i1 = _first_argmax_rows(el1, v1)
    t = jnp.exp(v1 - v0)
    w0 = 1.0 / (1.0 + t)
    w1 = t / (1.0 + t)
    inner = jnp.where(row == i0, w0, 0.0) + jnp.where(row == i1, w1, 0.0)
    blocks = [g_w * inner, jnp.where(row == 0, gsel.astype(F32), 0.0),
              jnp.zeros((LANES - 2 * R, tm), F32)]
    h2c_ref[:, D:] = jnp.concatenate(blocks, axis=0).T


def _mix_cross(a_lo, a_hi, u, x, wout, pool_w, pool_scale, cross_norm, wq, kt, v, wo, ffn_norm, wr_hi, wr_lo, rbias, tm):
    B, S, D = x.shape
    M = v.shape[1]
    prev_blocks = tm // MAX_WINDOW
    half_tiles = S // tm // 2
    const2 = lambda b, i: (0, 0)
    return pl.pallas_call(
        _mix_cross_kernel,
        grid=(B, S // tm),
        in_specs=[
            pl.BlockSpec((None, tm, DIFF_WIDTH), lambda b, i: (b, jnp.minimum(i, half_tiles - 1), 0)),
            pl.BlockSpec((None, tm, DIFF_WIDTH), lambda b, i: (b, jnp.maximum(i - half_tiles, 0), 0)),
            pl.BlockSpec((None, tm, POOL_WIDTH), lambda b, i: (b, i, 0)),
            pl.BlockSpec((None, MAX_WINDOW, POOL_WIDTH), lambda b, i: (b, jnp.maximum(i * prev_blocks - 1, 0), 0)),
            pl.BlockSpec((None, tm, D), lambda b, i: (b, i, 0)),
            pl.BlockSpec(wout.shape, const2),
            pl.BlockSpec(pool_w.shape, lambda b, i: (0, 0, 0)),
            pl.BlockSpec((1, POOL_WIDTH), const2),
            pl.BlockSpec((1, D), const2),
            pl.BlockSpec(wq.shape, const2),
            pl.BlockSpec((None, D, M), lambda b, i: (b, 0, 0)),
            pl.BlockSpec((None, M, D), lambda b, i: (b, 0, 0)),
            pl.BlockSpec(wo.shape, const2),
            pl.BlockSpec((1, D), const2),
            pl.BlockSpec(wr_hi.shape, const2),
            pl.BlockSpec(wr_lo.shape, const2),
            pl.BlockSpec(rbias.shape, const2),
        ],
        out_specs=pl.BlockSpec((None, tm, D + LANES), lambda b, i: (b, i, 0)),
        out_shape=jax.ShapeDtypeStruct((B, S, D + LANES), F32),
        compiler_params=pltpu.CompilerParams(
            dimension_semantics=("arbitrary", "arbitrary"), vmem_limit_bytes=VMEM_LIMIT_BYTES),
        name="mix_cross",
    )(a_lo, a_hi, u, u, x, wout, pool_w, pool_scale, cross_norm, wq, kt, v, wo, ffn_norm, wr_hi, wr_lo, rbias)


def _moe_kernel(glo_ref, ghi_ref, slot_ref, h2c_hbm, wg_hbm, wu_hbm, wd_hbm, fn_ref, wg_ref, wu_ref, wd_ref, gn_ref,
                y_hbm, src_ref, xbuf, ybuf, hf_s, acc_s, wg_x, wu_x, wd_x, gsem, ssem, wsem, fence_sem):
    c = pl.program_id(0)
    ntiles = pl.num_programs(0)
    tm, D = hf_s.shape
    slot = c % 2

    @pl.when(c == 0)
    def _():
        def invert(t, carry):
            src_ref[slot_ref[t]] = t
            return carry
        lax.fori_loop(0, slot_ref.shape[0], invert, 0, unroll=8)

    def row_gather(tile, r, s):
        t = src_ref[tile * tm + r]
        return pltpu.make_async_copy(h2c_hbm.at[pl.ds(t, 1)], xbuf.at[s, pl.ds(r, 1)], gsem.at[s])

    def row_scatter(tile, r, s):
        t = src_ref[tile * tm + r]
        return pltpu.make_async_copy(ybuf.at[s, pl.ds(r, 1)], y_hbm.at[pl.ds(t, 1)], ssem.at[s])

    def start_rows(make, tile):
        def body(r, carry):
            make(tile, r, tile % 2).start()
            return carry
        lax.fori_loop(0, tm, body, 0, unroll=8)

    def wait_gather(s):
        pltpu.make_async_copy(h2c_hbm.at[pl.ds(0, tm)], xbuf.at[s], gsem.at[s]).wait()

    def wait_scatter(s):
        pltpu.make_async_copy(ybuf.at[s], y_hbm.at[pl.ds(0, tm)], ssem.at[s]).wait()

    @pl.when(c == 0)
    def _():
        start_rows(row_gather, 0)

    wait_gather(slot)

    x = xbuf[slot]
    h2 = x[:, :D]
    hf_s[...] = _rms(h2, fn_ref[...]).astype(BF16)
    group = x[:, D + ROUTE_GROUP_LANE:D + ROUTE_GROUP_LANE + 1]

    def expert_pass(g, wg, wu, wd, first, neighbour_slot=None):
        hf = hf_s[...]
        in_group = group == g.astype(F32)
        n_segments = EXPERTS_PER_GROUP
        rows_per_segment = tm // n_segments
        experts_per_segment = EXPERTS_PER_GROUP // n_segments
        for e in range(EXPERTS_PER_GROUP):
            seg, first_in_seg = divmod(e, experts_per_segment)
            if neighbour_slot is not None and first_in_seg == 0:
                for r in range(seg * rows_per_segment, (seg + 1) * rows_per_segment):
                    row_gather(c + 1, r, neighbour_slot).start(priority=r % 2)
                    row_scatter(c - 1, r, neighbour_slot).start(priority=(r + 1) % 2)
            hg = _dot(hf, wg[e].astype(BF16))
            hu = _dot(hf, wu[e].astype(BF16))
            cw = jnp.where(in_group, x[:, D + e:D + e + 1], 0.0)
            act = (hg * (1.0 / (1.0 + jnp.exp(-hg))) * hu * cw).astype(BF16)
            part = _dot(act, wd[e].astype(BF16))
            if first and e == 0:
                acc_s[...] = part
            else:
                acc_s[...] += part
            if neighbour_slot is not None and first_in_seg == experts_per_segment - 1:
                pl.semaphore_signal(fence_sem, 1)
                pl.semaphore_wait(fence_sem, 1)

    interior = (c >= 1) & (c + 1 < ntiles)
    for par in range(2):
        @pl.when(interior & (slot == par))
        def _():
            expert_pass(glo_ref[c], wg_ref, wu_ref, wd_ref, True, neighbour_slot=1 - par)

    @pl.when(jnp.logical_not(interior))
    def _():
        @pl.when(c + 1 < ntiles)
        def _():
            start_rows(row_gather, c + 1)

        @pl.when(c >= 1)
        def _():
            start_rows(row_scatter, c - 1)

        expert_pass(glo_ref[c], wg_ref, wu_ref, wd_ref, True)

    def extra_pass(g, carry):
        copies = [pltpu.make_async_copy(w_hbm.at[pl.ds(g * EXPERTS_PER_GROUP, EXPERTS_PER_GROUP)], w_x, wsem.at[n])
                  for n, (w_hbm, w_x) in enumerate(((wg_hbm, wg_x), (wu_hbm, wu_x), (wd_hbm, wd_x)))]
        for cp in copies:
            cp.start()
        for cp in copies:
            cp.wait()
        expert_pass(g, wg_x, wu_x, wd_x, False)
        return carry

    lax.fori_loop(glo_ref[c] + 1, ghi_ref[c] + 1, extra_pass, 0)

    @pl.when(c >= 2)
    def _():
        wait_scatter(slot)

    ybuf[slot] = _rms(h2 + acc_s[...], gn_ref[...])

    @pl.when(c == ntiles - 1)
    def _():
        start_rows(row_scatter, c)
        wait_scatter(slot)

        @pl.when(ntiles >= 2)
        def _():
            wait_scatter(1 - slot)


def _moe(h2c, tile_glo, tile_ghi, slot, ffn_norm, wg, wu, wd, final_norm, tm):
    T, DL = h2c.shape
    D = DL - LANES
    FF = wg.shape[-1]
    E = EXPERTS_PER_GROUP
    any_spec = pl.BlockSpec(memory_space=pl.ANY)
    grid_spec = pltpu.PrefetchScalarGridSpec(
        num_scalar_prefetch=3,
        grid=(T // tm,),
        in_specs=[
            any_spec, any_spec, any_spec, any_spec,
            pl.BlockSpec((1, D), lambda c, glo, ghi, src: (0, 0)),
            pl.BlockSpec((E, D, FF), lambda c, glo, ghi, src: (glo[c], 0, 0)),
            pl.BlockSpec((E, D, FF), lambda c, glo, ghi, src: (glo[c], 0, 0)),
            pl.BlockSpec((E, FF, D), lambda c, glo, ghi, src: (glo[c], 0, 0)),
            pl.BlockSpec((1, D), lambda c, glo, ghi, src: (0, 0)),
        ],
        out_specs=any_spec,
        scratch_shapes=[
            pltpu.SMEM((T,), jnp.int32),
            pltpu.VMEM((2, tm, DL), F32),
            pltpu.VMEM((2, tm, D), F32),
            pltpu.VMEM((tm, D), BF16),
            pltpu.VMEM((tm, D), F32),
            pltpu.VMEM((E, D, FF), wg.dtype),
            pltpu.VMEM((E, D, FF), wu.dtype),
            pltpu.VMEM((E, FF, D), wd.dtype),
            pltpu.SemaphoreType.DMA((2,)),
            pltpu.SemaphoreType.DMA((2,)),
            pltpu.SemaphoreType.DMA((3,)),
            pltpu.SemaphoreType.REGULAR,
        ],
    )
    return pl.pallas_call(
        _moe_kernel,
        grid_spec=grid_spec,
        out_shape=jax.ShapeDtypeStruct((T, D), F32),
        compiler_params=pltpu.CompilerParams(
            dimension_semantics=("arbitrary",), vmem_limit_bytes=VMEM_LIMIT_BYTES),
        name="moe",
    )(tile_glo, tile_ghi, slot, h2c, wg, wu, wd, ffn_norm, wg, wu, wd, final_norm)


def _group_sort_plan(group, tm):
    T = group.shape[0]
    onehot = (group[:, None] == jnp.arange(N_EXPERT_GROUPS, dtype=jnp.int32)[None, :]).astype(jnp.int32)
    ranks = jnp.cumsum(onehot, axis=0) - onehot
    counts = jnp.sum(onehot, axis=0)
    starts = jnp.cumsum(counts) - counts
    slot = jnp.sum(onehot * (ranks + starts[None, :]), axis=1)
    first = jnp.arange(0, T, tm, dtype=jnp.int32)
    group_at = lambda pos: jnp.sum((starts[None, 1:] <= pos[:, None]).astype(jnp.int32), axis=1)
    return slot, group_at(first), group_at(first + (tm - 1))


def _split3_bf16(c):
    hi = c.astype(BF16).astype(F32)
    mid = (c - hi).astype(BF16).astype(F32)
    lo = (c - hi - mid).astype(BF16).astype(F32)
    return jnp.stack([hi, mid, lo], axis=-1)


def _router_operands(router_group, router_group_bias, router_expert, router_expert_bias):
    D = router_group.shape[0]
    R = GROUP_ROW_STRIDE
    w = jnp.zeros((ROUTER_ROWS, D), F32)
    bias = jnp.zeros((ROUTER_ROWS,), F32)
    slab_pad = jnp.full((R - N_EXPERT_GROUPS,), MASK_VALUE, F32)
    w = w.at[:N_EXPERT_GROUPS].set(router_group.T)
    bias = bias.at[:R].set(jnp.concatenate([router_group_bias, slab_pad]))
    for g in range(N_EXPERT_GROUPS):
        w = w.at[R * (g + 1):R * (g + 1) + EXPERTS_PER_GROUP].set(router_expert[g].T)
        bias = bias.at[R * (g + 1):R * (g + 2)].set(jnp.concatenate([router_expert_bias[g], slab_pad]))
    w_hi = w.astype(BF16)
    w_lo = (w - w_hi.astype(F32)).astype(BF16)
    return w_hi, w_lo, bias[:, None]


def kernel(x, mem, rel_bias, attn_norm, w_in, lambda_q1, lambda_k1, lambda_q2, lambda_k2, diff_subln, pool_w,
           pool_scale, w_out, cross_norm, mem_norm, wq_cross, wkv_cross, wo_cross, ffn_norm, router_group,
           router_group_bias, router_expert, router_expert_bias, w_gate, w_up, w_down, final_norm):
    B, S, D = x.shape
    layer = 0
    qt, vt, k, u = _in_proj(x, attn_norm[layer][None], w_in[layer], tm=512)

    dtiles = _bias_tiles(rel_bias)
    far_split = jnp.concatenate([_split3_bf16(rel_bias[FAR_BUCKET] * LOG2E), rel_bias[0][:, None] * LOG2E],
                                axis=1)
    lam_params = jnp.stack([lambda_q1[layer], lambda_k1[layer], lambda_q2[layer], lambda_k2[layer]])
    subln_cols = jnp.broadcast_to(diff_subln[layer][:, None], (DIFF_DV, ATTN_BLOCK))
    a_lo, a_hi = _diff_attn(far_split, qt, k, vt, dtiles, lam_params, subln_cols)

    wkv = wkv_cross[layer]
    kt, v = _mem_kv(mem, mem_norm[layer][None], wkv[:, :D].T.astype(BF16), wkv[:, D:].astype(BF16))

    wr_hi, wr_lo, rbias = _router_operands(router_group[layer], router_group_bias[layer],
                                           router_expert[layer], router_expert_bias[layer])
    h2c = _mix_cross(a_lo, a_hi, u, x, w_out[layer].astype(BF16), pool_w[layer].astype(BF16), pool_scale[layer][None],
                     cross_norm[layer][None], wq_cross[layer].astype(BF16), kt, v,
                     wo_cross[layer].astype(BF16), ffn_norm[layer][None], wr_hi, wr_lo, rbias, tm=512)
    h2c = h2c.reshape(B * S, D + LANES)

    moe_tm = 512
    slot, tile_glo, tile_ghi = _group_sort_plan(h2c[:, D + ROUTE_GROUP_LANE].astype(jnp.int32), moe_tm)
    y = _moe(h2c, tile_glo, tile_ghi, slot, ffn_norm[layer][None], w_gate[layer], w_up[layer], w_down[layer],
             final_norm[None], moe_tm)
    return y.reshape(B, S, D)
```

```python
import functools
import math

import jax
import jax.numpy as jnp
from jax import lax
from jax.experimental import pallas as pl
from jax.experimental.pallas import tpu as pltpu

F32 = jnp.float32
BF16 = jnp.bfloat16

EPS = 1e-6
CHUNK = 64
N_DIFF_HEADS = 4
DIFF_DK = 64
DIFF_DV = 128
DIFF_WIDTH = N_DIFF_HEADS * DIFF_DV
POOL_WINDOWS = (2, 4, 8, 16)
POOL_DIM = 128
POOL_WIDTH = len(POOL_WINDOWS) * POOL_DIM
MAX_WINDOW = max(POOL_WINDOWS)
REL_BUCKETS = 32
REL_MAX_DISTANCE = 128
N_CROSS_HEADS = 4
N_EXPERT_GROUPS = 4
EXPERTS_PER_GROUP = 4
N_EXPERTS = N_EXPERT_GROUPS * EXPERTS_PER_GROUP
LAMBDA_INIT = 0.8 - 0.6 * math.exp(-0.3 * 0)

LANES = 128
SUBLANES = 8
MXU_DIM = 256
VMEM_LIMIT_BYTES = 56 * 1024 * 1024

ATTN_BLOCK = 2 * MXU_DIM
FAR_BUCKET = REL_BUCKETS // 2 - 1
MASK_VALUE = -1e30
M_FINITE = float(jnp.finfo(jnp.float32).max)
LOG2E = math.log2(math.e)
V_AUG_ROWS = DIFF_DV + 16

ROUTER_ROWS = LANES // 2
GROUP_ROW_STRIDE = SUBLANES
ROUTE_GROUP_LANE = GROUP_ROW_STRIDE


def _rms(x, g):
    return x * lax.rsqrt(jnp.mean(x * x, axis=-1, keepdims=True) + EPS) * g


def _dot(a, b):
    return jnp.dot(a, b, preferred_element_type=F32)


def _dot_nt(a, b):
    return lax.dot_general(a, b, (((1,), (1,)), ((), ())), preferred_element_type=F32)


def _in_proj_kernel(x_ref, g_ref, win_ref, qt_ref, vt_ref, k_ref, u_ref, wqv_t_s, wku_s):
    @pl.when((pl.program_id(0) == 0) & (pl.program_id(1) == 0))
    def _():
        w = win_ref[...]
        wqv_t_s[:DIFF_WIDTH] = w[:, :DIFF_WIDTH].T.astype(BF16)
        wqv_t_s[DIFF_WIDTH:] = w[:, 2 * DIFF_WIDTH:3 * DIFF_WIDTH].T.astype(BF16)
        wku_s[:, :DIFF_WIDTH] = w[:, DIFF_WIDTH:2 * DIFF_WIDTH].astype(BF16)
        wku_s[:, DIFF_WIDTH:] = w[:, 3 * DIFF_WIDTH:].astype(BF16)

    hn = _rms(x_ref[...], g_ref[...]).astype(BF16)
    zt = _dot_nt(wqv_t_s[...], hn)
    qt_ref[...] = (zt[:DIFF_WIDTH] * (DIFF_DK ** -0.5 * LOG2E)).astype(BF16)
    for c in range(vt_ref.shape[0]):
        vt_ref[c] = zt[DIFF_WIDTH:, c * ATTN_BLOCK:(c + 1) * ATTN_BLOCK].astype(BF16)
    z = _dot(hn, wku_s[...])
    k_ref[...] = z[:, :DIFF_WIDTH].astype(BF16)
    u_ref[...] = z[:, DIFF_WIDTH:]


def _in_proj(x, attn_norm, w_in, tm):
    B, S, D = x.shape
    nkb = tm // ATTN_BLOCK
    return pl.pallas_call(
        _in_proj_kernel,
        grid=(B, S // tm),
        in_specs=[
            pl.BlockSpec((None, tm, D), lambda b, i: (b, i, 0)),
            pl.BlockSpec((1, D), lambda b, i: (0, 0)),
            pl.BlockSpec(w_in.shape, lambda b, i: (0, 0)),
        ],
        out_specs=[
            pl.BlockSpec((None, DIFF_WIDTH, tm), lambda b, i: (b, 0, i)),
            pl.BlockSpec((None, nkb, DIFF_WIDTH, ATTN_BLOCK), lambda b, i: (b, i, 0, 0)),
            pl.BlockSpec((None, tm, DIFF_WIDTH), lambda b, i: (b, i, 0)),
            pl.BlockSpec((None, tm, POOL_WIDTH), lambda b, i: (b, i, 0)),
        ],
        out_shape=[
            jax.ShapeDtypeStruct((B, DIFF_WIDTH, S), BF16),
            jax.ShapeDtypeStruct((B, S // ATTN_BLOCK, DIFF_WIDTH, ATTN_BLOCK), BF16),
            jax.ShapeDtypeStruct((B, S, DIFF_WIDTH), BF16),
            jax.ShapeDtypeStruct((B, S, POOL_WIDTH), F32),
        ],
        scratch_shapes=[pltpu.VMEM((2 * DIFF_WIDTH, D), BF16), pltpu.VMEM((D, DIFF_WIDTH + POOL_WIDTH), BF16)],
        compiler_params=pltpu.CompilerParams(
            dimension_semantics=("arbitrary", "arbitrary"), vmem_limit_bytes=VMEM_LIMIT_BYTES),
        name="in_proj",
    )(x, attn_norm, w_in)


def _rel_bucket(rel):
    nb = REL_BUCKETS // 2
    max_exact = nb // 2
    ret = (rel > 0).astype(jnp.int32) * nb
    n = jnp.abs(rel)
    nf = jnp.maximum(n, 1).astype(jnp.float32)
    large = max_exact + (jnp.log(nf / max_exact) / math.log(REL_MAX_DISTANCE / max_exact)
                         * (nb - max_exact)).astype(jnp.int32)
    large = jnp.minimum(large, nb - 1)
    return ret + jnp.where(n < max_exact, n, large)


def _bias_tile_kernel(band_ref, out_ref):
    n = ATTN_BLOCK
    kk = lax.broadcasted_iota(jnp.int32, (n, n), 0)
    qq = lax.broadcasted_iota(jnp.int32, (n, n), 1)
    for d in range(2):
        band = jnp.broadcast_to(band_ref[d], (n, 2 * n))
        tile = pltpu.roll(band, n + 1, axis=1, stride=1, stride_axis=0)[:, :n]
        if d == 0:
            tile = jnp.where(kk // CHUNK <= qq // CHUNK, tile, MASK_VALUE)
        out_ref[d] = tile
    out_ref[2] = jnp.full((n, n), MASK_VALUE, F32)


def _bias_tiles(rel_bias):
    n = ATTN_BLOCK
    m = jnp.arange(2 * n, dtype=jnp.int32)
    rel = jnp.stack([n - 1 - m, -1 - m])
    band = (rel_bias[_rel_bucket(rel)] - rel_bias[FAR_BUCKET]) * LOG2E
    band = band.transpose(2, 0, 1)[:, :, None, :]
    return pl.pallas_call(
        _bias_tile_kernel,
        grid=(N_DIFF_HEADS,),
        in_specs=[pl.BlockSpec((None, 2, 1, 2 * n), lambda h: (h, 0, 0, 0))],
        out_specs=pl.BlockSpec((None, 3, n, n), lambda h: (h, 0, 0, 0)),
        out_shape=jax.ShapeDtypeStruct((N_DIFF_HEADS, 3, n, n), F32),
        name="bias_tiles",
    )(band)


def _attn_kernel(far_ref, qta_ref, qtb_ref, k_ref, vt_ref, dt_ref, lam_ref, g_ref, oa_ref, ob_ref,
                 kaug, vaug, qaug, qref, m_s, acc_s, s_buf, p_buf):
    h = pl.program_id(1)
    step = pl.program_id(2)
    nkb = vt_ref.shape[0]
    QB = ATTN_BLOCK
    KB = ATTN_BLOCK
    q_block = (step, nkb - 1 - step)

    @pl.when(step == 0)
    def _():
        lane = lax.broadcasted_iota(jnp.int32, (KB, LANES), 1)
        pad = jnp.where(lane == 0, far_ref[h, 0],
                        jnp.where(lane == 1, far_ref[h, 1],
                                  jnp.where(lane == 2, far_ref[h, 2],
                                            jnp.where(lane < 6, 1.0, 0.0)))).astype(BF16)
        vrow = lax.broadcasted_iota(jnp.int32, (V_AUG_ROWS - DIFF_DV, KB), 0)
        ones_row = jnp.where(vrow == 0, 1.0, 0.0).astype(BF16)

        def fill(c, carry):
            r = pl.multiple_of(c * KB, KB)
            kaug[pl.ds(r, KB), :LANES] = k_ref[pl.ds(r, KB), :]
            kaug[pl.ds(r, KB), LANES:] = pad
            vaug[c, :DIFF_DV] = vt_ref[c]
            vaug[c, DIFF_DV:] = ones_row
            return carry

        lax.fori_loop(0, nkb, fill, 0)

    zeros = jnp.zeros((DIFF_DK, QB), BF16)
    row = lax.broadcasted_iota(jnp.int32, (LANES, QB), 0)
    ones_blk = jnp.where(row < 3, 1.0, 0.0)

    def fill_query_rows(buf, w, q):
        buf[w, 0, :DIFF_DK] = q[:DIFF_DK]
        buf[w, 0, DIFF_DK:2 * DIFF_DK] = zeros
        buf[w, 1, :DIFF_DK] = zeros
        buf[w, 1, DIFF_DK:2 * DIFF_DK] = q[DIFF_DK:]

    for w, qt_ref in enumerate((qta_ref, qtb_ref)):
        q = qt_ref[...]
        fill_query_rows(qref, w, q)
        rd = pl.multiple_of(q_block[w] * KB, KB)
        k_own_t = kaug[pl.ds(rd, KB), :LANES].astype(F32).T
        qk = q.astype(F32) * k_own_t
        for half in range(2):
            m = jnp.sum(qk[half * DIFF_DK:(half + 1) * DIFF_DK], axis=0, keepdims=True) + far_ref[h, 3]
            m_s[w, half] = m
            nm = -m
            hi = nm.astype(BF16).astype(F32)
            mid = (nm - hi).astype(BF16).astype(F32)
            lo = (nm - hi - mid).astype(BF16).astype(F32)
            qref[w, half, 2 * DIFF_DK:] = jnp.where(
                row == 3, hi, jnp.where(row == 4, mid, jnp.where(row == 5, lo, ones_blk))).astype(BF16)
    acc_s[...] = jnp.zeros(acc_s.shape, F32)

    i_a, i_b = q_block
    near_a = (0, jnp.maximum(i_a - 1, 0))
    biased = ((0, i_a), near_a, (1, i_b), (1, i_b - 1))
    bias_tile = (0, jnp.where(i_a >= 1, 1, 2), 0, 1)
    n_far_a = jnp.maximum(i_a - 1, 0)
    n_far = n_far_a + i_b - 1

    def far_block(f):
        w = (f >= n_far_a).astype(jnp.int32)
        return w, f - w * n_far_a

    def probabilities(par, blk, bias):
        w, j = blk
        r = pl.multiple_of(j * KB, KB)
        kblk = kaug[pl.ds(r, KB), :]
        for half in range(2):
            s = _dot(kblk, qref[w, half])
            if bias is not None:
                s = s + dt_ref[bias]
            p_buf[par, half] = jnp.exp2(s.astype(BF16))

    def weighted_values(par, blk):
        w, j = blk
        vblk = vaug[j]
        for half in range(2):
            acc_s[w, half] += _dot(vblk, p_buf[par, half])

    probabilities(0, biased[0], bias_tile[0])
    for u in range(1, 4):
        weighted_values((u - 1) % 2, biased[u - 1])
        probabilities(u % 2, biased[u], bias_tile[u])

    def far_step(par, f):
        w_prev, j_prev = far_block(f - 1)
        first = f == 0
        weighted_values(1 - par, (jnp.where(first, biased[3][0], w_prev), jnp.where(first, biased[3][1], j_prev)))
        probabilities(par, far_block(f), None)

    def far_pair(g, carry):
        far_step(0, 2 * g)
        far_step(1, 2 * g + 1)
        return carry

    lax.fori_loop(0, n_far // 2, far_pair, 0)

    @pl.when(n_far % 2 == 1)
    def _():
        far_step(0, n_far - 1)

    w_last, j_last = far_block(n_far - 1)
    last = (jnp.where(n_far == 0, biased[3][0], w_last), jnp.where(n_far == 0, biased[3][1], j_last))
    for par in range(2):
        @pl.when((n_far + 1) % 2 == par)
        def _():
            weighted_values(par, last)

    lam = (jnp.exp(jnp.sum(lam_ref[0:1] * lam_ref[1:2], axis=-1, keepdims=True))
           - jnp.exp(jnp.sum(lam_ref[2:3] * lam_ref[3:4], axis=-1, keepdims=True)) + LAMBDA_INIT)

    def finalize(w, o_ref):
        l0 = acc_s[w, 0, DIFF_DV:DIFF_DV + 1]
        l1 = acc_s[w, 1, DIFF_DV:DIFF_DV + 1]
        ot = acc_s[w, 0, :DIFF_DV] / l0 - lam * (acc_s[w, 1, :DIFF_DV] / l1)
        inv = lax.rsqrt(jnp.mean(ot * ot, axis=0, keepdims=True) + EPS)
        o_ref[...] = (((ot * inv) * g_ref[...]) * (1.0 - LAMBDA_INIT)).T.astype(BF16)
        bad = jnp.where(jnp.abs(ot) <= M_FINITE, 0.0, 1.0) + jnp.where(jnp.maximum(l0, l1) <= M_FINITE, 0.0, 1.0)
        return jnp.max(bad) > 0.0

    for w, (qt_ref, o_ref) in enumerate(((qta_ref, oa_ref), (qtb_ref, ob_ref))):
        overflowed = finalize(w, o_ref)

        @pl.when(overflowed)
        def _():
            i_q = q_block[w]
            fill_query_rows(qaug, 0, qt_ref[...])
            qaug[0, 0, 2 * DIFF_DK:] = ones_blk.astype(BF16)
            qaug[0, 1, 2 * DIFF_DK:] = ones_blk.astype(BF16)
            acc_s[w] = jnp.zeros(acc_s.shape[1:], F32)

            def online_step(j, kind):
                r = pl.multiple_of(j * KB, KB)
                kblk = kaug[pl.ds(r, KB), :]
                vblk = vaug[j]
                for half in range(2):
                    s_buf[half] = _dot(kblk, qaug[0, half])
                    if kind is not None:
                        s_buf[half] += dt_ref[kind]
                    m_old = m_s[w, half]
                    m_new = jnp.maximum(m_old, jnp.max(s_buf[half], axis=0, keepdims=True))
                    p = jnp.exp2((s_buf[half] - m_new).astype(BF16))
                    acc_s[w, half] = jnp.exp2(m_old - m_new) * acc_s[w, half] + _dot(vblk, p)
                    m_s[w, half] = m_new

            online_step(i_q, 0)

            @pl.when(i_q >= 1)
            def _():
                online_step(i_q - 1, 1)

            def far_online(j, carry):
                online_step(j, None)
                return carry

            lax.fori_loop(0, jnp.maximum(i_q - 1, 0), far_online, 0)
            finalize(w, o_ref)


def _diff_attn(far_split, qt, k, vt, dtiles, lam_params, subln_cols):
    B, S, _ = k.shape
    nq = S // ATTN_BLOCK
    half_blocks = nq // 2
    out_half = jax.ShapeDtypeStruct((B, S // 2, DIFF_WIDTH), BF16)
    return pl.pallas_call(
        _attn_kernel,
        grid=(B, N_DIFF_HEADS, half_blocks),
        in_specs=[
            pl.BlockSpec(memory_space=pltpu.SMEM),
            pl.BlockSpec((None, 2 * DIFF_DK, ATTN_BLOCK), lambda b, h, s: (b, h, s)),
            pl.BlockSpec((None, 2 * DIFF_DK, ATTN_BLOCK), lambda b, h, s: (b, h, nq - 1 - s)),
            pl.BlockSpec((None, S, 2 * DIFF_DK), lambda b, h, s: (b, 0, h)),
            pl.BlockSpec((None, nq, DIFF_DV, ATTN_BLOCK), lambda b, h, s: (b, 0, h, 0)),
            pl.BlockSpec((None, 3, ATTN_BLOCK, ATTN_BLOCK), lambda b, h, s: (h, 0, 0, 0)),
            pl.BlockSpec(lam_params.shape, lambda b, h, s: (0, 0)),
            pl.BlockSpec((DIFF_DV, ATTN_BLOCK), lambda b, h, s: (0, 0)),
        ],
        out_specs=[
            pl.BlockSpec((None, ATTN_BLOCK, DIFF_DV), lambda b, h, s: (b, s, h)),
            pl.BlockSpec((None, ATTN_BLOCK, DIFF_DV), lambda b, h, s: (b, half_blocks - 1 - s, h)),
        ],
        out_shape=[out_half, out_half],
        scratch_shapes=[
            pltpu.VMEM((S, 2 * LANES), BF16),
            pltpu.VMEM((nq, V_AUG_ROWS, ATTN_BLOCK), BF16),
            pltpu.VMEM((1, 2, 2 * LANES, ATTN_BLOCK), BF16),
            pltpu.VMEM((2, 2, 2 * LANES, ATTN_BLOCK), BF16),
            pltpu.VMEM((2, 2, 1, ATTN_BLOCK), F32),
            pltpu.VMEM((2, 2, V_AUG_ROWS, ATTN_BLOCK), F32),
            pltpu.VMEM((2, ATTN_BLOCK, ATTN_BLOCK), F32),
            pltpu.VMEM((2, 2, ATTN_BLOCK, ATTN_BLOCK), BF16),
        ],
        compiler_params=pltpu.CompilerParams(
            dimension_semantics=("arbitrary", "arbitrary", "arbitrary"), vmem_limit_bytes=VMEM_LIMIT_BYTES),
        name="diff_attn",
    )(far_split, qt, qt, k, vt, dtiles, lam_params, subln_cols)


def _mem_kv_kernel(mem_ref, g_ref, wk_t_ref, wv_ref, kt_ref, v_ref):
    mn = _rms(mem_ref[...], g_ref[...]).astype(BF16)
    kt_ref[...] = _dot_nt(wk_t_ref[...], mn).astype(BF16)
    v_ref[...] = _dot(mn, wv_ref[...]).astype(BF16)


def _mem_kv(mem, mem_norm, wk_t, wv):
    B, M, D = mem.shape
    return pl.pallas_call(
        _mem_kv_kernel,
        grid=(B,),
        in_specs=[
            pl.BlockSpec((None, M, D), lambda b: (b, 0, 0)),
            pl.BlockSpec((1, D), lambda b: (0, 0)),
            pl.BlockSpec((D, D), lambda b: (0, 0)),
            pl.BlockSpec((D, D), lambda b: (0, 0)),
        ],
        out_specs=[
            pl.BlockSpec((None, D, M), lambda b: (b, 0, 0)),
            pl.BlockSpec((None, M, D), lambda b: (b, 0, 0)),
        ],
        out_shape=[jax.ShapeDtypeStruct((B, D, M), BF16), jax.ShapeDtypeStruct((B, M, D), BF16)],
        compiler_params=pltpu.CompilerParams(
            dimension_semantics=("arbitrary",), vmem_limit_bytes=VMEM_LIMIT_BYTES),
        name="mem_kv",
    )(mem, mem_norm, wk_t, wv)


def _first_argmax_rows(v, vmax):
    row = lax.broadcasted_iota(jnp.int32, v.shape, 0)
    return jnp.min(jnp.where(v == vmax, row, v.shape[0]), axis=0, keepdims=True)


def _mix_cross_kernel(alo_ref, ahi_ref, u_ref, uprev_ref, x_ref, wout_ref, pw_ref, ps_ref, cn_ref, wq_ref, kt_ref, v_ref,
                      wo_ref, fn_ref, wrh_ref, wrl_ref, rb_ref, h2c_ref):
    i = pl.program_id(1)
    tm = u_ref.shape[0]
    D = x_ref.shape[1]
    hd = kt_ref.shape[0] // N_CROSS_HEADS

    u = u_ref[...]
    prev = jnp.where(i > 0, uprev_ref[...], 0.0)
    pos1 = (i * tm + 1 + lax.broadcasted_iota(jnp.int32, (tm, 1), 0)).astype(F32)
    ys = []
    for g, w in enumerate(POOL_WINDOWS):
        sl = slice(g * POOL_DIM, (g + 1) * POOL_DIM)
        acc = jnp.concatenate([prev[:, sl], u[:, sl]], axis=0)
        span = 1
        while span < w:
            acc = acc[span:] + acc[:-span]
            span *= 2
        first = MAX_WINDOW - (w - 1)
        mean = acc[first:first + tm] / jnp.minimum(pos1, float(w))
        d = (mean - u[:, sl]).astype(BF16)
        ys.append(_dot(d, pw_ref[g]) * ps_ref[:, sl])
    p = jnp.concatenate(ys, axis=-1).astype(BF16)

    a = jnp.where(i < pl.num_programs(1) // 2, alo_ref[...], ahi_ref[...])
    h1 = x_ref[...] + _dot(jnp.concatenate([a, p], axis=-1), wout_ref[...])

    q = _dot(_rms(h1, cn_ref[...]).astype(BF16), wq_ref[...]).astype(BF16)
    outs = []
    for c in range(N_CROSS_HEADS):
        sl = slice(c * hd, (c + 1) * hd)
        s = _dot(q[:, sl], kt_ref[sl, :]) * (hd ** -0.5)
        e = jnp.exp(s - jnp.max(s, axis=-1, keepdims=True))
        pr = (e / jnp.sum(e, axis=-1, keepdims=True)).astype(BF16)
        outs.append(_dot(pr, v_ref[:, sl]))
    o = jnp.concatenate(outs, axis=-1).astype(BF16)
    h2 = h1 + _dot(o, wo_ref[...])
    h2c_ref[:, :D] = h2

    hf = _rms(h2, fn_ref[...])
    hf_hi = hf.astype(BF16)
    hf_lo = (hf - hf_hi.astype(F32)).astype(BF16)
    lg = (_dot_nt(wrh_ref[...], hf_hi) + _dot_nt(wrl_ref[...], hf_hi) + _dot_nt(wrh_ref[...], hf_lo)
          + rb_ref[...])
    R = GROUP_ROW_STRIDE
    gl = lg[:R]
    gmax = jnp.max(gl, axis=0, keepdims=True)
    gsel = _first_argmax_rows(gl, gmax)
    g_w = 1.0 / jnp.sum(jnp.exp(gl - gmax), axis=0, keepdims=True)
    el = lg[R:2 * R]
    for g in range(1, N_EXPERT_GROUPS):
        el = jnp.where(gsel == g, lg[R * (g + 1):R * (g + 2)], el)
    row = lax.broadcasted_iota(jnp.int32, el.shape, 0)
    v0 = jnp.max(el, axis=0, keepdims=True)
    i0 = _first_argmax_rows(el, v0)
    el1 = jnp.where(row == i0, -jnp.inf, el)
    v1 = jnp.max(el1, axis=0, keepdims=True)
    i1 = _first_argmax_rows(el1, v1)
    t = jnp.exp(v1 - v0)
    w0 = 1.0 / (1.0 + t)
    w1 = t / (1.0 + t)
    inner = jnp.where(row == i0, w0, 0.0) + jnp.where(row == i1, w1, 0.0)
    blocks = [g_w * inner, jnp.where(row == 0, gsel.astype(F32), 0.0),
              jnp.zeros((LANES - 2 * R, tm), F32)]
    h2c_ref[:, D:] = jnp.concatenate(blocks, axis=0).T


def _mix_cross(a_lo, a_hi, u, x, wout, pool_w, pool_scale, cross_norm, wq, kt, v, wo, ffn_norm, wr_hi, wr_lo, rbias, tm):
    B, S, D = x.shape
    M = v.shape[1]
    prev_blocks = tm // MAX_WINDOW
    half_tiles = S // tm // 2
    const2 = lambda b, i: (0, 0)
    return pl.pallas_call(
        _mix_cross_kernel,
        grid=(B, S // tm),
        in_specs=[
            pl.BlockSpec((None, tm, DIFF_WIDTH), lambda b, i: (b, jnp.minimum(i, half_tiles - 1), 0)),
            pl.BlockSpec((None, tm, DIFF_WIDTH), lambda b, i: (b, jnp.maximum(i - half_tiles, 0), 0)),
            pl.BlockSpec((None, tm, POOL_WIDTH), lambda b, i: (b, i, 0)),
            pl.BlockSpec((None, MAX_WINDOW, POOL_WIDTH), lambda b, i: (b, jnp.maximum(i * prev_blocks - 1, 0), 0)),
            pl.BlockSpec((None, tm, D), lambda b, i: (b, i, 0)),
            pl.BlockSpec(wout.shape, const2),
            pl.BlockSpec(pool_w.shape, lambda b, i: (0, 0, 0)),
            pl.BlockSpec((1, POOL_WIDTH), const2),
            pl.BlockSpec((1, D), const2),
            pl.BlockSpec(wq.shape, const2),
            pl.BlockSpec((None, D, M), lambda b, i: (b, 0, 0)),
            pl.BlockSpec((None, M, D), lambda b, i: (b, 0, 0)),
            pl.BlockSpec(wo.shape, const2),
            pl.BlockSpec((1, D), const2),
            pl.BlockSpec(wr_hi.shape, const2),
            pl.BlockSpec(wr_lo.shape, const2),
            pl.BlockSpec(rbias.shape, const2),
        ],
        out_specs=pl.BlockSpec((None, tm, D + LANES), lambda b, i: (b, i, 0)),
        out_shape=jax.ShapeDtypeStruct((B, S, D + LANES), F32),
        compiler_params=pltpu.CompilerParams(
            dimension_semantics=("arbitrary", "arbitrary"), vmem_limit_bytes=VMEM_LIMIT_BYTES),
        name="mix_cross",
    )(a_lo, a_hi, u, u, x, wout, pool_w, pool_scale, cross_norm, wq, kt, v, wo, ffn_norm, wr_hi, wr_lo, rbias)


def _moe_kernel(glo_ref, ghi_ref, slot_ref, h2c_hbm, wg_hbm, wu_hbm, wd_hbm, fn_ref, wg_ref, wu_ref, wd_ref, gn_ref,
                y_hbm, src_ref, xbuf, ybuf, hf_s, acc_s, wg_x, wu_x, wd_x, gsem, ssem, wsem, fence_sem):
    c = pl.program_id(0)
    ntiles = pl.num_programs(0)
    tm, D = hf_s.shape
    slot = c % 2

    @pl.when(c == 0)
    def _():
        def invert(t, carry):
            src_ref[slot_ref[t]] = t
            return carry
        lax.fori_loop(0, slot_ref.shape[0], invert, 0, unroll=8)

    def row_gather(tile, r, s):
        t = src_ref[tile * tm + r]
        return pltpu.make_async_copy(h2c_hbm.at[pl.ds(t, 1)], xbuf.at[s, pl.ds(r, 1)], gsem.at[s])

    def row_scatter(tile, r, s):
        t = src_ref[tile * tm + r]
        return pltpu.make_async_copy(ybuf.at[s, pl.ds(r, 1)], y_hbm.at[pl.ds(t, 1)], ssem.at[s])

    def start_rows(make, tile):
        def body(r, carry):
            make(tile, r, tile % 2).start()
            return carry
        lax.fori_loop(0, tm, body, 0, unroll=8)

    def wait_gather(s):
        pltpu.make_async_copy(h2c_hbm.at[pl.ds(0, tm)], xbuf.at[s], gsem.at[s]).wait()

    def wait_scatter(s):
        pltpu.make_async_copy(ybuf.at[s], y_hbm.at[pl.ds(0, tm)], ssem.at[s]).wait()

    @pl.when(c == 0)
    def _():
        start_rows(row_gather, 0)

    wait_gather(slot)

    x = xbuf[slot]
    h2 = x[:, :D]
    hf_s[...] = _rms(h2, fn_ref[...]).astype(BF16)
    group = x[:, D + ROUTE_GROUP_LANE:D + ROUTE_GROUP_LANE + 1]

    def expert_pass(g, wg, wu, wd, first, neighbour_slot=None):
        hf = hf_s[...]
        in_group = group == g.astype(F32)
        n_segments = EXPERTS_PER_GROUP
        rows_per_segment = tm // n_segments
        experts_per_segment = EXPERTS_PER_GROUP // n_segments
        for e in range(EXPERTS_PER_GROUP):
            seg, first_in_seg = divmod(e, experts_per_segment)
            if neighbour_slot is not None and first_in_seg == 0:
                for r in range(seg * rows_per_segment, (seg + 1) * rows_per_segment):
                    row_gather(c + 1, r, neighbour_slot).start(priority=r % 2)
                    row_scatter(c - 1, r, neighbour_slot).start(priority=(r + 1) % 2)
            hg = _dot(hf, wg[e].astype(BF16))
            hu = _dot(hf, wu[e].astype(BF16))
            cw = jnp.where(in_group, x[:, D + e:D + e + 1], 0.0)
            act = (hg * (1.0 / (1.0 + jnp.exp(-hg))) * hu * cw).astype(BF16)
            part = _dot(act, wd[e].astype(BF16))
            if first and e == 0:
                acc_s[...] = part
            else:
                acc_s[...] += part
            if neighbour_slot is not None and first_in_seg == experts_per_segment - 1:
                pl.semaphore_signal(fence_sem, 1)
                pl.semaphore_wait(fence_sem, 1)

    interior = (c >= 1) & (c + 1 < ntiles)
    for par in range(2):
        @pl.when(interior & (slot == par))
        def _():
            expert_pass(glo_ref[c], wg_ref, wu_ref, wd_ref, True, neighbour_slot=1 - par)

    @pl.when(jnp.logical_not(interior))
    def _():
        @pl.when(c + 1 < ntiles)
        def _():
            start_rows(row_gather, c + 1)

        @pl.when(c >= 1)
        def _():
            start_rows(row_scatter, c - 1)

        expert_pass(glo_ref[c], wg_ref, wu_ref, wd_ref, True)

    def extra_pass(g, carry):
        copies = [pltpu.make_async_copy(w_hbm.at[pl.ds(g * EXPERTS_PER_GROUP, EXPERTS_PER_GROUP)], w_x, wsem.at[n])
                  for n, (w_hbm, w_x) in enumerate(((wg_hbm, wg_x), (wu_hbm, wu_x), (wd_hbm, wd_x)))]
        for cp in copies:
            cp.start()
        for cp in copies:
            cp.wait()
        expert_pass(g, wg_x, wu_x, wd_x, False)
        return carry

    lax.fori_loop(glo_ref[c] + 1, ghi_ref[c] + 1, extra_pass, 0)

    @pl.when(c >= 2)
    def _():
        wait_scatter(slot)

    ybuf[slot] = _rms(h2 + acc_s[...], gn_ref[...])

    @pl.when(c == ntiles - 1)
    def _():
        start_rows(row_scatter, c)
        wait_scatter(slot)

        @pl.when(ntiles >= 2)
        def _():
            wait_scatter(1 - slot)


def _moe(h2c, tile_glo, tile_ghi, slot, ffn_norm, wg, wu, wd, final_norm, tm):
    T, DL = h2c.shape
    D = DL - LANES
    FF = wg.shape[-1]
    E = EXPERTS_PER_GROUP
    any_spec = pl.BlockSpec(memory_space=pl.ANY)
    grid_spec = pltpu.PrefetchScalarGridSpec(
        num_scalar_prefetch=3,
        grid=(T // tm,),
        in_specs=[
            any_spec, any_spec, any_spec, any_spec,
            pl.BlockSpec((1, D), lambda c, glo, ghi, src: (0, 0)),
            pl.BlockSpec((E, D, FF), lambda c, glo, ghi, src: (glo[c], 0, 0)),
            pl.BlockSpec((E, D, FF), lambda c, glo, ghi, src: (glo[c], 0, 0)),
            pl.BlockSpec((E, FF, D), lambda c, glo, ghi, src: (glo[c], 0, 0)),
            pl.BlockSpec((1, D), lambda c, glo, ghi, src: (0, 0)),
        ],
        out_specs=any_spec,
        scratch_shapes=[
            pltpu.SMEM((T,), jnp.int32),
            pltpu.VMEM((2, tm, DL), F32),
            pltpu.VMEM((2, tm, D), F32),
            pltpu.VMEM((tm, D), BF16),
            pltpu.VMEM((tm, D), F32),
            pltpu.VMEM((E, D, FF), wg.dtype),
            pltpu.VMEM((E, D, FF), wu.dtype),
            pltpu.VMEM((E, FF, D), wd.dtype),
            pltpu.SemaphoreType.DMA((2,)),
            pltpu.SemaphoreType.DMA((2,)),
            pltpu.SemaphoreType.DMA((3,)),
            pltpu.SemaphoreType.REGULAR,
        ],
    )
    return pl.pallas_call(
        _moe_kernel,
        grid_spec=grid_spec,
        out_shape=jax.ShapeDtypeStruct((T, D), F32),
        compiler_params=pltpu.CompilerParams(
            dimension_semantics=("arbitrary",), vmem_limit_bytes=VMEM_LIMIT_BYTES),
        name="moe",
    )(tile_glo, tile_ghi, slot, h2c, wg, wu, wd, ffn_norm, wg, wu, wd, final_norm)


def _group_sort_plan(group, tm):
    T = group.shape[0]
    onehot = (group[:, None] == jnp.arange(N_EXPERT_GROUPS, dtype=jnp.int32)[None, :]).astype(jnp.int32)
    ranks = jnp.cumsum(onehot, axis=0) - onehot
    counts = jnp.sum(onehot, axis=0)
    starts = jnp.cumsum(counts) - counts
    slot = jnp.sum(onehot * (ranks + starts[None, :]), axis=1)
    first = jnp.arange(0, T, tm, dtype=jnp.int32)
    group_at = lambda pos: jnp.sum((starts[None, 1:] <= pos[:, None]).astype(jnp.int32), axis=1)
    return slot, group_at(first), group_at(first + (tm - 1))


def _split3_bf16(c):
    hi = c.astype(BF16).astype(F32)
    mid = (c - hi).astype(BF16).astype(F32)
    lo = (c - hi - mid).astype(BF16).astype(F32)
    return jnp.stack([hi, mid, lo], axis=-1)


def _router_operands(router_group, router_group_bias, router_expert, router_expert_bias):
    D = router_group.shape[0]
    R = GROUP_ROW_STRIDE
    w = jnp.zeros((ROUTER_ROWS, D), F32)
    bias = jnp.zeros((ROUTER_ROWS,), F32)
    slab_pad = jnp.full((R - N_EXPERT_GROUPS,), MASK_VALUE, F32)
    w = w.at[:N_EXPERT_GROUPS].set(router_group.T)
    bias = bias.at[:R].set(jnp.concatenate([router_group_bias, slab_pad]))
    for g in range(N_EXPERT_GROUPS):
        w = w.at[R * (g + 1):R * (g + 1) + EXPERTS_PER_GROUP].set(router_expert[g].T)
        bias = bias.at[R * (g + 1):R * (g + 2)].set(jnp.concatenate([router_expert_bias[g], slab_pad]))
    w_hi = w.astype(BF16)
    w_lo = (w - w_hi.astype(F32)).astype(BF16)
    return w_hi, w_lo, bias[:, None]


def kernel(x, mem, rel_bias, attn_norm, w_in, lambda_q1, lambda_k1, lambda_q2, lambda_k2, diff_subln, pool_w,
           pool_scale, w_out, cross_norm, mem_norm, wq_cross, wkv_cross, wo_cross, ffn_norm, router_group,
           router_group_bias, router_expert, router_expert_bias, w_gate, w_up, w_down, final_norm):
    B, S, D = x.shape
    layer = 0
    qt, vt, k, u = _in_proj(x, attn_norm[layer][None], w_in[layer], tm=1024)

    dtiles = _bias_tiles(rel_bias)
    far_split = jnp.concatenate([_split3_bf16(rel_bias[FAR_BUCKET] * LOG2E), rel_bias[0][:, None] * LOG2E],
                                axis=1)
    lam_params = jnp.stack([lambda_q1[layer], lambda_k1[layer], lambda_q2[layer], lambda_k2[layer]])
    subln_cols = jnp.broadcast_to(diff_subln[layer][:, None], (DIFF_DV, ATTN_BLOCK))
    a_lo, a_hi = _diff_attn(far_split, qt, k, vt, dtiles, lam_params, subln_cols)

    wkv = wkv_cross[layer]
    kt, v = _mem_kv(mem, mem_norm[layer][None], wkv[:, :D].T.astype(BF16), wkv[:, D:].astype(BF16))

    wr_hi, wr_lo, rbias = _router_operands(router_group[layer], router_group_bias[layer],
                                           router_expert[layer], router_expert_bias[layer])
    h2c = _mix_cross(a_lo, a_hi, u, x, w_out[layer].astype(BF16), pool_w[layer].astype(BF16), pool_scale[layer][None],
                     cross_norm[layer][None], wq_cross[layer].astype(BF16), kt, v,
                     wo_cross[layer].astype(BF16), ffn_norm[layer][None], wr_hi, wr_lo, rbias, tm=1024)
    h2c = h2c.reshape(B * S, D + LANES)

    moe_tm = 512
    slot, tile_glo, tile_ghi = _group_sort_plan(h2c[:, D + ROUTE_GROUP_LANE].astype(jnp.int32), moe_tm)
    y = _moe(h2c, tile_glo, tile_ghi, slot, ffn_norm[layer][None], w_gate[layer], w_up[layer], w_down[layer],
             final_norm[None], moe_tm)
    return y.reshape(B, S, D)
```

```python
import functools
import math

import jax
import jax.numpy as jnp
from jax import lax
from jax.experimental import pallas as pl
from jax.experimental.pallas import tpu as pltpu

F32 = jnp.float32
BF16 = jnp.bfloat16

EPS = 1e-6
CHUNK = 64
N_DIFF_HEADS = 4
DIFF_DK = 64
DIFF_DV = 128
DIFF_WIDTH = N_DIFF_HEADS * DIFF_DV
POOL_WINDOWS = (2, 4, 8, 16)
POOL_DIM = 128
POOL_WIDTH = len(POOL_WINDOWS) * POOL_DIM
MAX_WINDOW = max(POOL_WINDOWS)
REL_BUCKETS = 32
REL_MAX_DISTANCE = 128
N_CROSS_HEADS = 4
N_EXPERT_GROUPS = 4
EXPERTS_PER_GROUP = 4
N_EXPERTS = N_EXPERT_GROUPS * EXPERTS_PER_GROUP
LAMBDA_INIT = 0.8 - 0.6 * math.exp(-0.3 * 0)

LANES = 128
SUBLANES = 8
MXU_DIM = 256
VMEM_LIMIT_BYTES = 56 * 1024 * 1024

ATTN_BLOCK = 2 * MXU_DIM
FAR_BUCKET = REL_BUCKETS // 2 - 1
MASK_VALUE = -1e30
M_FINITE = float(jnp.finfo(jnp.float32).max)
LOG2E = math.log2(math.e)
V_AUG_ROWS = DIFF_DV + 16

ROUTER_ROWS = LANES // 2
GROUP_ROW_STRIDE = SUBLANES
ROUTE_GROUP_LANE = GROUP_ROW_STRIDE


def _rms(x, g):
    return x * lax.rsqrt(jnp.mean(x * x, axis=-1, keepdims=True) + EPS) * g


def _dot(a, b):
    return jnp.dot(a, b, preferred_element_type=F32)


def _dot_nt(a, b):
    return lax.dot_general(a, b, (((1,), (1,)), ((), ())), preferred_element_type=F32)


def _in_proj_kernel(x_ref, g_ref, win_ref, qt_ref, vt_ref, k_ref, u_ref, wqv_t_s, wku_s):
    @pl.when((pl.program_id(0) == 0) & (pl.program_id(1) == 0))
    def _():
        w = win_ref[...]
        wqv_t_s[:DIFF_WIDTH] = w[:, :DIFF_WIDTH].T.astype(BF16)
        wqv_t_s[DIFF_WIDTH:] = w[:, 2 * DIFF_WIDTH:3 * DIFF_WIDTH].T.astype(BF16)
        wku_s[:, :DIFF_WIDTH] = w[:, DIFF_WIDTH:2 * DIFF_WIDTH].astype(BF16)
        wku_s[:, DIFF_WIDTH:] = w[:, 3 * DIFF_WIDTH:].astype(BF16)

    hn = _rms(x_ref[...], g_ref[...]).astype(BF16)
    zt = _dot_nt(wqv_t_s[...], hn)
    qt_ref[...] = (zt[:DIFF_WIDTH] * (DIFF_DK ** -0.5 * LOG2E)).astype(BF16)
    for c in range(vt_ref.shape[0]):
        vt_ref[c] = zt[DIFF_WIDTH:, c * ATTN_BLOCK:(c + 1) * ATTN_BLOCK].astype(BF16)
    z = _dot(hn, wku_s[...])
    k_ref[...] = z[:, :DIFF_WIDTH].astype(BF16)
    u_ref[...] = z[:, DIFF_WIDTH:]


def _in_proj(x, attn_norm, w_in, tm):
    B, S, D = x.shape
    nkb = tm // ATTN_BLOCK
    return pl.pallas_call(
        _in_proj_kernel,
        grid=(B, S // tm),
        in_specs=[
            pl.BlockSpec((None, tm, D), lambda b, i: (b, i, 0)),
            pl.BlockSpec((1, D), lambda b, i: (0, 0)),
            pl.BlockSpec(w_in.shape, lambda b, i: (0, 0)),
        ],
        out_specs=[
            pl.BlockSpec((None, DIFF_WIDTH, tm), lambda b, i: (b, 0, i)),
            pl.BlockSpec((None, nkb, DIFF_WIDTH, ATTN_BLOCK), lambda b, i: (b, i, 0, 0)),
            pl.BlockSpec((None, tm, DIFF_WIDTH), lambda b, i: (b, i, 0)),
            pl.BlockSpec((None, tm, POOL_WIDTH), lambda b, i: (b, i, 0)),
        ],
        out_shape=[
            jax.ShapeDtypeStruct((B, DIFF_WIDTH, S), BF16),
            jax.ShapeDtypeStruct((B, S // ATTN_BLOCK, DIFF_WIDTH, ATTN_BLOCK), BF16),
            jax.ShapeDtypeStruct((B, S, DIFF_WIDTH), BF16),
            jax.ShapeDtypeStruct((B, S, POOL_WIDTH), F32),
        ],
        scratch_shapes=[pltpu.VMEM((2 * DIFF_WIDTH, D), BF16), pltpu.VMEM((D, DIFF_WIDTH + POOL_WIDTH), BF16)],
        compiler_params=pltpu.CompilerParams(
            dimension_semantics=("arbitrary", "arbitrary"), vmem_limit_bytes=VMEM_LIMIT_BYTES),
        name="in_proj",
    )(x, attn_norm, w_in)


def _rel_bucket(rel):
    nb = REL_BUCKETS // 2
    max_exact = nb // 2
    ret = (rel > 0).astype(jnp.int32) * nb
    n = jnp.abs(rel)
    nf = jnp.maximum(n, 1).astype(jnp.float32)
    large = max_exact + (jnp.log(nf / max_exact) / math.log(REL_MAX_DISTANCE / max_exact)
                         * (nb - max_exact)).astype(jnp.int32)
    large = jnp.minimum(large, nb - 1)
    return ret + jnp.where(n < max_exact, n, large)


def _bias_tile_kernel(band_ref, out_ref):
    n = ATTN_BLOCK
    kk = lax.broadcasted_iota(jnp.int32, (n, n), 0)
    qq = lax.broadcasted_iota(jnp.int32, (n, n), 1)
    for d in range(2):
        band = jnp.broadcast_to(band_ref[d], (n, 2 * n))
        tile = pltpu.roll(band, n + 1, axis=1, stride=1, stride_axis=0)[:, :n]
        if d == 0:
            tile = jnp.where(kk // CHUNK <= qq // CHUNK, tile, MASK_VALUE)
        out_ref[d] = tile
    out_ref[2] = jnp.full((n, n), MASK_VALUE, F32)


def _bias_tiles(rel_bias):
    n = ATTN_BLOCK
    m = jnp.arange(2 * n, dtype=jnp.int32)
    rel = jnp.stack([n - 1 - m, -1 - m])
    band = (rel_bias[_rel_bucket(rel)] - rel_bias[FAR_BUCKET]) * LOG2E
    band = band.transpose(2, 0, 1)[:, :, None, :]
    return pl.pallas_call(
        _bias_tile_kernel,
        grid=(N_DIFF_HEADS,),
        in_specs=[pl.BlockSpec((None, 2, 1, 2 * n), lambda h: (h, 0, 0, 0))],
        out_specs=pl.BlockSpec((None, 3, n, n), lambda h: (h, 0, 0, 0)),
        out_shape=jax.ShapeDtypeStruct((N_DIFF_HEADS, 3, n, n), F32),
        name="bias_tiles",
    )(band)


def _attn_kernel(far_ref, qta_ref, qtb_ref, k_ref, vt_ref, dt_ref, lam_ref, g_ref, oa_ref, ob_ref,
                 kaug, vaug, qaug, qref, m_s, acc_s, s_buf, p_buf):
    h = pl.program_id(1)
    step = pl.program_id(2)
    nkb = vt_ref.shape[0]
    QB = ATTN_BLOCK
    KB = ATTN_BLOCK
    q_block = (step, nkb - 1 - step)

    @pl.when(step == 0)
    def _():
        lane = lax.broadcasted_iota(jnp.int32, (KB, DIFF_DK), 1)
        pad = jnp.where(lane == 0, far_ref[h, 0],
                        jnp.where(lane == 1, far_ref[h, 1],
                                  jnp.where(lane == 2, far_ref[h, 2],
                                            jnp.where(lane < 6, 1.0, 0.0)))).astype(BF16)
        vrow = lax.broadcasted_iota(jnp.int32, (V_AUG_ROWS - DIFF_DV, KB), 0)
        ones_row = jnp.where(vrow == 0, 1.0, 0.0).astype(BF16)

        def fill(c, carry):
            r = pl.multiple_of(c * KB, KB)
            for half in range(2):
                kaug[pl.ds(r, KB), half * LANES:half * LANES + DIFF_DK] = \
                    k_ref[pl.ds(r, KB), half * DIFF_DK:(half + 1) * DIFF_DK]
                kaug[pl.ds(r, KB), half * LANES + DIFF_DK:(half + 1) * LANES] = pad
            vaug[c, :DIFF_DV] = vt_ref[c]
            vaug[c, DIFF_DV:] = ones_row
            return carry

        lax.fori_loop(0, nkb, fill, 0)

    row = lax.broadcasted_iota(jnp.int32, (DIFF_DK, QB), 0)
    ones_blk = jnp.where(row < 3, 1.0, 0.0)

    def fill_query_rows(buf, w, q):
        buf[w, 0, :DIFF_DK] = q[:DIFF_DK]
        buf[w, 1, :DIFF_DK] = q[DIFF_DK:]

    for w, qt_ref in enumerate((qta_ref, qtb_ref)):
        q = qt_ref[...]
        fill_query_rows(qref, w, q)
        rd = pl.multiple_of(q_block[w] * KB, KB)
        k_own_t = kaug[pl.ds(rd, KB), :].astype(F32).T
        for half in range(2):
            qk = q[half * DIFF_DK:(half + 1) * DIFF_DK].astype(F32) * k_own_t[half * LANES:half * LANES + DIFF_DK]
            m = jnp.sum(qk, axis=0, keepdims=True) + far_ref[h, 3]
            m_s[w, half] = m
            nm = -m
            hi = nm.astype(BF16).astype(F32)
            mid = (nm - hi).astype(BF16).astype(F32)
            lo = (nm - hi - mid).astype(BF16).astype(F32)
            qref[w, half, DIFF_DK:] = jnp.where(
                row == 3, hi, jnp.where(row == 4, mid, jnp.where(row == 5, lo, ones_blk))).astype(BF16)
    acc_s[...] = jnp.zeros(acc_s.shape, F32)

    i_a, i_b = q_block
    near_a = (0, jnp.maximum(i_a - 1, 0))
    biased = ((0, i_a), near_a, (1, i_b), (1, i_b - 1))
    bias_tile = (0, jnp.where(i_a >= 1, 1, 2), 0, 1)
    n_far_a = jnp.maximum(i_a - 1, 0)
    n_far = n_far_a + i_b - 1

    def far_block(f):
        w = (f >= n_far_a).astype(jnp.int32)
        return w, f - w * n_far_a

    def probabilities(par, blk, bias):
        w, j = blk
        r = pl.multiple_of(j * KB, KB)
        for half in range(2):
            s = _dot(kaug[pl.ds(r, KB), half * LANES:(half + 1) * LANES], qref[w, half])
            if bias is not None:
                s = s + dt_ref[bias]
            p_buf[par, half] = jnp.exp2(s.astype(BF16))

    def weighted_values(par, blk):
        w, j = blk
        vblk = vaug[j]
        for half in range(2):
            acc_s[w, half] += _dot(vblk, p_buf[par, half])

    probabilities(0, biased[0], bias_tile[0])
    for u in range(1, 4):
        weighted_values((u - 1) % 2, biased[u - 1])
        probabilities(u % 2, biased[u], bias_tile[u])

    def far_step(par, f):
        w_prev, j_prev = far_block(f - 1)
        first = f == 0
        weighted_values(1 - par, (jnp.where(first, biased[3][0], w_prev), jnp.where(first, biased[3][1], j_prev)))
        probabilities(par, far_block(f), None)

    def far_pair(g, carry):
        far_step(0, 2 * g)
        far_step(1, 2 * g + 1)
        return carry

    lax.fori_loop(0, n_far // 2, far_pair, 0)

    @pl.when(n_far % 2 == 1)
    def _():
        far_step(0, n_far - 1)

    w_last, j_last = far_block(n_far - 1)
    last = (jnp.where(n_far == 0, biased[3][0], w_last), jnp.where(n_far == 0, biased[3][1], j_last))
    for par in range(2):
        @pl.when((n_far + 1) % 2 == par)
        def _():
            weighted_values(par, last)

    lam = (jnp.exp(jnp.sum(lam_ref[0:1] * lam_ref[1:2], axis=-1, keepdims=True))
           - jnp.exp(jnp.sum(lam_ref[2:3] * lam_ref[3:4], axis=-1, keepdims=True)) + LAMBDA_INIT)

    def finalize(w, o_ref):
        l0 = acc_s[w, 0, DIFF_DV:DIFF_DV + 1]
        l1 = acc_s[w, 1, DIFF_DV:DIFF_DV + 1]
        ot = acc_s[w, 0, :DIFF_DV] / l0 - lam * (acc_s[w, 1, :DIFF_DV] / l1)
        inv = lax.rsqrt(jnp.mean(ot * ot, axis=0, keepdims=True) + EPS)
        o_ref[...] = (((ot * inv) * g_ref[...]) * (1.0 - LAMBDA_INIT)).T.astype(BF16)
        bad = jnp.where(jnp.abs(ot) <= M_FINITE, 0.0, 1.0) + jnp.where(jnp.maximum(l0, l1) <= M_FINITE, 0.0, 1.0)
        return jnp.max(bad) > 0.0

    for w, (qt_ref, o_ref) in enumerate(((qta_ref, oa_ref), (qtb_ref, ob_ref))):
        overflowed = finalize(w, o_ref)

        @pl.when(overflowed)
        def _():
            i_q = q_block[w]
            fill_query_rows(qaug, 0, qt_ref[...])
            qaug[0, 0, DIFF_DK:] = ones_blk.astype(BF16)
            qaug[0, 1, DIFF_DK:] = ones_blk.astype(BF16)
            acc_s[w] = jnp.zeros(acc_s.shape[1:], F32)

            def online_step(j, kind):
                r = pl.multiple_of(j * KB, KB)
                vblk = vaug[j]
                for half in range(2):
                    s_buf[half] = _dot(kaug[pl.ds(r, KB), half * LANES:(half + 1) * LANES], qaug[0, half])
                    if kind is not None:
                        s_buf[half] += dt_ref[kind]
                    m_old = m_s[w, half]
                    m_new = jnp.maximum(m_old, jnp.max(s_buf[half], axis=0, keepdims=True))
                    p = jnp.exp2((s_buf[half] - m_new).astype(BF16))
                    acc_s[w, half] = jnp.exp2(m_old - m_new) * acc_s[w, half] + _dot(vblk, p)
                    m_s[w, half] = m_new

            online_step(i_q, 0)

            @pl.when(i_q >= 1)
            def _():
                online_step(i_q - 1, 1)

            def far_online(j, carry):
                online_step(j, None)
                return carry

            lax.fori_loop(0, jnp.maximum(i_q - 1, 0), far_online, 0)
            finalize(w, o_ref)


def _diff_attn(far_split, qt, k, vt, dtiles, lam_params, subln_cols):
    B, S, _ = k.shape
    nq = S // ATTN_BLOCK
    half_blocks = nq // 2
    out_half = jax.ShapeDtypeStruct((B, S // 2, DIFF_WIDTH), BF16)
    return pl.pallas_call(
        _attn_kernel,
        grid=(B, N_DIFF_HEADS, half_blocks),
        in_specs=[
            pl.BlockSpec(memory_space=pltpu.SMEM),
            pl.BlockSpec((None, 2 * DIFF_DK, ATTN_BLOCK), lambda b, h, s: (b, h, s)),
            pl.BlockSpec((None, 2 * DIFF_DK, ATTN_BLOCK), lambda b, h, s: (b, h, nq - 1 - s)),
            pl.BlockSpec((None, S, 2 * DIFF_DK), lambda b, h, s: (b, 0, h)),
            pl.BlockSpec((None, nq, DIFF_DV, ATTN_BLOCK), lambda b, h, s: (b, 0, h, 0)),
            pl.BlockSpec((None, 3, ATTN_BLOCK, ATTN_BLOCK), lambda b, h, s: (h, 0, 0, 0)),
            pl.BlockSpec(lam_params.shape, lambda b, h, s: (0, 0)),
            pl.BlockSpec((DIFF_DV, ATTN_BLOCK), lambda b, h, s: (0, 0)),
        ],
        out_specs=[
            pl.BlockSpec((None, ATTN_BLOCK, DIFF_DV), lambda b, h, s: (b, s, h)),
            pl.BlockSpec((None, ATTN_BLOCK, DIFF_DV), lambda b, h, s: (b, half_blocks - 1 - s, h)),
        ],
        out_shape=[out_half, out_half],
        scratch_shapes=[
            pltpu.VMEM((S, 2 * LANES), BF16),
            pltpu.VMEM((nq, V_AUG_ROWS, ATTN_BLOCK), BF16),
            pltpu.VMEM((1, 2, LANES, ATTN_BLOCK), BF16),
            pltpu.VMEM((2, 2, LANES, ATTN_BLOCK), BF16),
            pltpu.VMEM((2, 2, 1, ATTN_BLOCK), F32),
            pltpu.VMEM((2, 2, V_AUG_ROWS, ATTN_BLOCK), F32),
            pltpu.VMEM((2, ATTN_BLOCK, ATTN_BLOCK), F32),
            pltpu.VMEM((2, 2, ATTN_BLOCK, ATTN_BLOCK), BF16),
        ],
        compiler_params=pltpu.CompilerParams(
            dimension_semantics=("arbitrary", "arbitrary", "arbitrary"), vmem_limit_bytes=VMEM_LIMIT_BYTES),
        name="diff_attn",
    )(far_split, qt, qt, k, vt, dtiles, lam_params, subln_cols)


def _mem_kv_kernel(mem_ref, g_ref, wk_t_ref, wv_ref, kt_ref, v_ref):
    mn = _rms(mem_ref[...], g_ref[...]).astype(BF16)
    kt_ref[...] = _dot_nt(wk_t_ref[...], mn).astype(BF16)
    v_ref[...] = _dot(mn, wv_ref[...]).astype(BF16)


def _mem_kv(mem, mem_norm, wk_t, wv):
    B, M, D = mem.shape
    return pl.pallas_call(
        _mem_kv_kernel,
        grid=(B,),
        in_specs=[
            pl.BlockSpec((None, M, D), lambda b: (b, 0, 0)),
            pl.BlockSpec((1, D), lambda b: (0, 0)),
            pl.BlockSpec((D, D), lambda b: (0, 0)),
            pl.BlockSpec((D, D), lambda b: (0, 0)),
        ],
        out_specs=[
            pl.BlockSpec((None, D, M), lambda b: (b, 0, 0)),
            pl.BlockSpec((None, M, D), lambda b: (b, 0, 0)),
        ],
        out_shape=[jax.ShapeDtypeStruct((B, D, M), BF16), jax.ShapeDtypeStruct((B, M, D), BF16)],
        compiler_params=pltpu.CompilerParams(
            dimension_semantics=("arbitrary",), vmem_limit_bytes=VMEM_LIMIT_BYTES),
        name="mem_kv",
    )(mem, mem_norm, wk_t, wv)


def _first_argmax_rows(v, vmax):
    row = lax.broadcasted_iota(jnp.int32, v.shape, 0)
    return jnp.min(jnp.where(v == vmax, row, v.shape[0]), axis=0, keepdims=True)


def _mix_cross_kernel(alo_ref, ahi_ref, u_ref, uprev_ref, x_ref, wout_ref, pw_ref, ps_ref, cn_ref, wq_ref, kt_ref, v_ref,
                      wo_ref, fn_ref, wrh_ref, wrl_ref, rb_ref, h2c_ref):
    i = pl.program_id(1)
    tm = u_ref.shape[0]
    D = x_ref.shape[1]
    hd = kt_ref.shape[0] // N_CROSS_HEADS

    u = u_ref[...]
    prev = jnp.where(i > 0, uprev_ref[...], 0.0)
    pos1 = (i * tm + 1 + lax.broadcasted_iota(jnp.int32, (tm, 1), 0)).astype(F32)
    ys = []
    for g, w in enumerate(POOL_WINDOWS):
        sl = slice(g * POOL_DIM, (g + 1) * POOL_DIM)
        acc = jnp.concatenate([prev[:, sl], u[:, sl]], axis=0)
        span = 1
        while span < w:
            acc = acc[span:] + acc[:-span]
            span *= 2
        first = MAX_WINDOW - (w - 1)
        mean = acc[first:first + tm] / jnp.minimum(pos1, float(w))
        d = (mean - u[:, sl]).astype(BF16)
        ys.append(_dot(d, pw_ref[g]) * ps_ref[:, sl])
    p = jnp.concatenate(ys, axis=-1).astype(BF16)

    a = jnp.where(i < pl.num_programs(1) // 2, alo_ref[...], ahi_ref[...])
    h1 = x_ref[...] + _dot(jnp.concatenate([a, p], axis=-1), wout_ref[...])

    q = _dot(_rms(h1, cn_ref[...]).astype(BF16), wq_ref[...]).astype(BF16)
    outs = []
    for c in range(N_CROSS_HEADS):
        sl = slice(c * hd, (c + 1) * hd)
        s = _dot(q[:, sl], kt_ref[sl, :]) * (hd ** -0.5)
        e = jnp.exp(s - jnp.max(s, axis=-1, keepdims=True))
        pr = (e / jnp.sum(e, axis=-1, keepdims=True)).astype(BF16)
        outs.append(_dot(pr, v_ref[:, sl]))
    o = jnp.concatenate(outs, axis=-1).astype(BF16)
    h2 = h1 + _dot(o, wo_ref[...])
    h2c_ref[:, :D] = h2

    hf = _rms(h2, fn_ref[...])
    hf_hi = hf.astype(BF16)
    hf_lo = (hf - hf_hi.astype(F32)).astype(BF16)
    lg = (_dot_nt(wrh_ref[...], hf_hi) + _dot_nt(wrl_ref[...], hf_hi) + _dot_nt(wrh_ref[...], hf_lo)
          + rb_ref[...])
    R = GROUP_ROW_STRIDE
    gl = lg[:R]
    gmax = jnp.max(gl, axis=0, keepdims=True)
    gsel = _first_argmax_rows(gl, gmax)
    g_w = 1.0 / jnp.sum(jnp.exp(gl - gmax), axis=0, keepdims=True)
    el = lg[R:2 * R]
    for g in range(1, N_EXPERT_GROUPS):
        el = jnp.where(gsel == g, lg[R * (g + 1):R * (g + 2)], el)
    row = lax.broadcasted_iota(jnp.int32, el.shape, 0)
    v0 = jnp.max(el, axis=0, keepdims=True)
    i0 = _first_argmax_rows(el, v0)
    el1 = jnp.where(row == i0, -jnp.inf, el)
    v1 = jnp.max(el1, axis=0, keepdims=True)
    i1 = _first_argmax_rows(el1, v1)
    t = jnp.exp(v1 - v0)
    w0 = 1.0 / (1.0 + t)
    w1 = t / (1.0 + t)
    inner = jnp.where(row == i0, w0, 0.0) + jnp.where(row == i1, w1, 0.0)
    blocks = [g_w * inner, jnp.where(row == 0, gsel.astype(F32), 0.0),
              jnp.zeros((LANES - 2 * R, tm), F32)]
    h2c_ref[:, D:] = jnp.concatenate(blocks, axis=0).T


def _mix_cross(a_lo, a_hi, u, x, wout, pool_w, pool_scale, cross_norm, wq, kt, v, wo, ffn_norm, wr_hi, wr_lo, rbias, tm):
    B, S, D = x.shape
    M = v.shape[1]
    prev_blocks = tm // MAX_WINDOW
    half_tiles = S // tm // 2
    const2 = lambda b, i: (0, 0)
    return pl.pallas_call(
        _mix_cross_kernel,
        grid=(B, S // tm),
        in_specs=[
            pl.BlockSpec((None, tm, DIFF_WIDTH), lambda b, i: (b, jnp.minimum(i, half_tiles - 1), 0)),
            pl.BlockSpec((None, tm, DIFF_WIDTH), lambda b, i: (b, jnp.maximum(i - half_tiles, 0), 0)),
            pl.BlockSpec((None, tm, POOL_WIDTH), lambda b, i: (b, i, 0)),
            pl.BlockSpec((None, MAX_WINDOW, POOL_WIDTH), lambda b, i: (b, jnp.maximum(i * prev_blocks - 1, 0), 0)),
            pl.BlockSpec((None, tm, D), lambda b, i: (b, i, 0)),
            pl.BlockSpec(wout.shape, const2),
            pl.BlockSpec(pool_w.shape, lambda b, i: (0, 0, 0)),
            pl.BlockSpec((1, POOL_WIDTH), const2),
            pl.BlockSpec((1, D), const2),
            pl.BlockSpec(wq.shape, const2),
            pl.BlockSpec((None, D, M), lambda b, i: (b, 0, 0)),
            pl.BlockSpec((None, M, D), lambda b, i: (b, 0, 0)),
            pl.BlockSpec(wo.shape, const2),
            pl.BlockSpec((1, D), const2),
            pl.BlockSpec(wr_hi.shape, const2),
            pl.BlockSpec(wr_lo.shape, const2),
            pl.BlockSpec(rbias.shape, const2),
        ],
        out_specs=pl.BlockSpec((None, tm, D + LANES), lambda b, i: (b, i, 0)),
        out_shape=jax.ShapeDtypeStruct((B, S, D + LANES), F32),
        compiler_params=pltpu.CompilerParams(
            dimension_semantics=("arbitrary", "arbitrary"), vmem_limit_bytes=VMEM_LIMIT_BYTES),
        name="mix_cross",
    )(a_lo, a_hi, u, u, x, wout, pool_w, pool_scale, cross_norm, wq, kt, v, wo, ffn_norm, wr_hi, wr_lo, rbias)


def _moe_kernel(glo_ref, ghi_ref, slot_ref, h2c_hbm, wg_hbm, wu_hbm, wd_hbm, fn_ref, wg_ref, wu_ref, wd_ref, gn_ref,
                y_hbm, src_ref, xbuf, ybuf, hf_s, acc_s, wg_x, wu_x, wd_x, gsem, ssem, wsem, fence_sem):
    c = pl.program_id(0)
    ntiles = pl.num_programs(0)
    tm, D = hf_s.shape
    slot = c % 2

    @pl.when(c == 0)
    def _():
        def invert(t, carry):
            src_ref[slot_ref[t]] = t
            return carry
        lax.fori_loop(0, slot_ref.shape[0], invert, 0, unroll=8)

    def row_gather(tile, r, s):
        t = src_ref[tile * tm + r]
        return pltpu.make_async_copy(h2c_hbm.at[pl.ds(t, 1)], xbuf.at[s, pl.ds(r, 1)], gsem.at[s])

    def row_scatter(tile, r, s):
        t = src_ref[tile * tm + r]
        return pltpu.make_async_copy(ybuf.at[s, pl.ds(r, 1)], y_hbm.at[pl.ds(t, 1)], ssem.at[s])

    def start_rows(make, tile):
        def body(r, carry):
            make(tile, r, tile % 2).start()
            return carry
        lax.fori_loop(0, tm, body, 0, unroll=8)

    def wait_gather(s):
        pltpu.make_async_copy(h2c_hbm.at[pl.ds(0, tm)], xbuf.at[s], gsem.at[s]).wait()

    def wait_scatter(s):
        pltpu.make_async_copy(ybuf.at[s], y_hbm.at[pl.ds(0, tm)], ssem.at[s]).wait()

    @pl.when(c == 0)
    def _():
        start_rows(row_gather, 0)

    wait_gather(slot)

    x = xbuf[slot]
    h2 = x[:, :D]
    hf_s[...] = _rms(h2, fn_ref[...]).astype(BF16)
    group = x[:, D + ROUTE_GROUP_LANE:D + ROUTE_GROUP_LANE + 1]

    def expert_pass(g, wg, wu, wd, first, neighbour_slot=None):
        hf = hf_s[...]
        in_group = group == g.astype(F32)
        n_segments = EXPERTS_PER_GROUP
        rows_per_segment = tm // n_segments
        experts_per_segment = EXPERTS_PER_GROUP // n_segments
        for e in range(EXPERTS_PER_GROUP):
            seg, first_in_seg = divmod(e, experts_per_segment)
            if neighbour_slot is not None and first_in_seg == 0:
                for r in range(seg * rows_per_segment, (seg + 1) * rows_per_segment):
                    row_gather(c + 1, r, neighbour_slot).start(priority=r % 2)
                    row_scatter(c - 1, r, neighbour_slot).start(priority=(r + 1) % 2)
            hg = _dot(hf, wg[e].astype(BF16))
            hu = _dot(hf, wu[e].astype(BF16))
            cw = jnp.where(in_group, x[:, D + e:D + e + 1], 0.0)
            act = (hg * (1.0 / (1.0 + jnp.exp(-hg))) * hu * cw).astype(BF16)
            part = _dot(act, wd[e].astype(BF16))
            if first and e == 0:
                acc_s[...] = part
            else:
                acc_s[...] += part
            if neighbour_slot is not None and first_in_seg == experts_per_segment - 1:
                pl.semaphore_signal(fence_sem, 1)
                pl.semaphore_wait(fence_sem, 1)

    interior = (c >= 1) & (c + 1 < ntiles)
    for par in range(2):
        @pl.when(interior & (slot == par))
        def _():
            expert_pass(glo_ref[c], wg_ref, wu_ref, wd_ref, True, neighbour_slot=1 - par)

    @pl.when(jnp.logical_not(interior))
    def _():
        @pl.when(c + 1 < ntiles)
        def _():
            start_rows(row_gather, c + 1)

        @pl.when(c >= 1)
        def _():
            start_rows(row_scatter, c - 1)

        expert_pass(glo_ref[c], wg_ref, wu_ref, wd_ref, True)

    def extra_pass(g, carry):
        copies = [pltpu.make_async_copy(w_hbm.at[pl.ds(g * EXPERTS_PER_GROUP, EXPERTS_PER_GROUP)], w_x, wsem.at[n])
                  for n, (w_hbm, w_x) in enumerate(((wg_hbm, wg_x), (wu_hbm, wu_x), (wd_hbm, wd_x)))]
        for cp in copies:
            cp.start()
        for cp in copies:
            cp.wait()
        expert_pass(g, wg_x, wu_x, wd_x, False)
        return carry

    lax.fori_loop(glo_ref[c] + 1, ghi_ref[c] + 1, extra_pass, 0)

    @pl.when(c >= 2)
    def _():
        wait_scatter(slot)

    ybuf[slot] = _rms(h2 + acc_s[...], gn_ref[...])

    @pl.when(c == ntiles - 1)
    def _():
        start_rows(row_scatter, c)
        wait_scatter(slot)

        @pl.when(ntiles >= 2)
        def _():
            wait_scatter(1 - slot)


def _moe(h2c, tile_glo, tile_ghi, slot, ffn_norm, wg, wu, wd, final_norm, tm):
    T, DL = h2c.shape
    D = DL - LANES
    FF = wg.shape[-1]
    E = EXPERTS_PER_GROUP
    any_spec = pl.BlockSpec(memory_space=pl.ANY)
    grid_spec = pltpu.PrefetchScalarGridSpec(
        num_scalar_prefetch=3,
        grid=(T // tm,),
        in_specs=[
            any_spec, any_spec, any_spec, any_spec,
            pl.BlockSpec((1, D), lambda c, glo, ghi, src: (0, 0)),
            pl.BlockSpec((E, D, FF), lambda c, glo, ghi, src: (glo[c], 0, 0)),
            pl.BlockSpec((E, D, FF), lambda c, glo, ghi, src: (glo[c], 0, 0)),
            pl.BlockSpec((E, FF, D), lambda c, glo, ghi, src: (glo[c], 0, 0)),
            pl.BlockSpec((1, D), lambda c, glo, ghi, src: (0, 0)),
        ],
        out_specs=any_spec,
        scratch_shapes=[
            pltpu.SMEM((T,), jnp.int32),
            pltpu.VMEM((2, tm, DL), F32),
            pltpu.VMEM((2, tm, D), F32),
            pltpu.VMEM((tm, D), BF16),
            pltpu.VMEM((tm, D), F32),
            pltpu.VMEM((E, D, FF), wg.dtype),
            pltpu.VMEM((E, D, FF), wu.dtype),
            pltpu.VMEM((E, FF, D), wd.dtype),
            pltpu.SemaphoreType.DMA((2,)),
            pltpu.SemaphoreType.DMA((2,)),
            pltpu.SemaphoreType.DMA((3,)),
            pltpu.SemaphoreType.REGULAR,
        ],
    )
    return pl.pallas_call(
        _moe_kernel,
        grid_spec=grid_spec,
        out_shape=jax.ShapeDtypeStruct((T, D), F32),
        compiler_params=pltpu.CompilerParams(
            dimension_semantics=("arbitrary",), vmem_limit_bytes=VMEM_LIMIT_BYTES),
        name="moe",
    )(tile_glo, tile_ghi, slot, h2c, wg, wu, wd, ffn_norm, wg, wu, wd, final_norm)


def _group_sort_plan(group, tm):
    T = group.shape[0]
    onehot = (group[:, None] == jnp.arange(N_EXPERT_GROUPS, dtype=jnp.int32)[None, :]).astype(jnp.int32)
    ranks = jnp.cumsum(onehot, axis=0) - onehot
    counts = jnp.sum(onehot, axis=0)
    starts = jnp.cumsum(counts) - counts
    slot = jnp.sum(onehot * (ranks + starts[None, :]), axis=1)
    first = jnp.arange(0, T, tm, dtype=jnp.int32)
    group_at = lambda pos: jnp.sum((starts[None, 1:] <= pos[:, None]).astype(jnp.int32), axis=1)
    return slot, group_at(first), group_at(first + (tm - 1))


def _split3_bf16(c):
    hi = c.astype(BF16).astype(F32)
    mid = (c - hi).astype(BF16).astype(F32)
    lo = (c - hi - mid).astype(BF16).astype(F32)
    return jnp.stack([hi, mid, lo], axis=-1)


def _router_operands(router_group, router_group_bias, router_expert, router_expert_bias):
    D = router_group.shape[0]
    R = GROUP_ROW_STRIDE
    w = jnp.zeros((ROUTER_ROWS, D), F32)
    bias = jnp.zeros((ROUTER_ROWS,), F32)
    slab_pad = jnp.full((R - N_EXPERT_GROUPS,), MASK_VALUE, F32)
    w = w.at[:N_EXPERT_GROUPS].set(router_group.T)
    bias = bias.at[:R].set(jnp.concatenate([router_group_bias, slab_pad]))
    for g in range(N_EXPERT_GROUPS):
        w = w.at[R * (g + 1):R * (g + 1) + EXPERTS_PER_GROUP].set(router_expert[g].T)
        bias = bias.at[R * (g + 1):R * (g + 2)].set(jnp.concatenate([router_expert_bias[g], slab_pad]))
    w_hi = w.astype(BF16)
    w_lo = (w - w_hi.astype(F32)).astype(BF16)
    return w_hi, w_lo, bias[:, None]


def kernel(x, mem, rel_bias, attn_norm, w_in, lambda_q1, lambda_k1, lambda_q2, lambda_k2, diff_subln, pool_w,
           pool_scale, w_out, cross_norm, mem_norm, wq_cross, wkv_cross, wo_cross, ffn_norm, router_group,
           router_group_bias, router_expert, router_expert_bias, w_gate, w_up, w_down, final_norm):
    B, S, D = x.shape
    layer = 0
    qt, vt, k, u = _in_proj(x, attn_norm[layer][None], w_in[layer], tm=1024)

    dtiles = _bias_tiles(rel_bias)
    far_split = jnp.concatenate([_split3_bf16(rel_bias[FAR_BUCKET] * LOG2E), rel_bias[0][:, None] * LOG2E],
                                axis=1)
    lam_params = jnp.stack([lambda_q1[layer], lambda_k1[layer], lambda_q2[layer], lambda_k2[layer]])
    subln_cols = jnp.broadcast_to(diff_subln[layer][:, None], (DIFF_DV, ATTN_BLOCK))
    a_lo, a_hi = _diff_attn(far_split, qt, k, vt, dtiles, lam_params, subln_cols)

    wkv = wkv_cross[layer]
    kt, v = _mem_kv(mem, mem_norm[layer][None], wkv[:, :D].T.astype(BF16), wkv[:, D:].astype(BF16))

    wr_hi, wr_lo, rbias = _router_operands(router_group[layer], router_group_bias[layer],
                                           router_expert[layer], router_expert_bias[layer])
    h2c = _mix_cross(a_lo, a_hi, u, x, w_out[layer].astype(BF16), pool_w[layer].astype(BF16), pool_scale[layer][None],
                     cross_norm[layer][None], wq_cross[layer].astype(BF16), kt, v,
                     wo_cross[layer].astype(BF16), ffn_norm[layer][None], wr_hi, wr_lo, rbias, tm=1024)
    h2c = h2c.reshape(B * S, D + LANES)

    moe_tm = 512
    slot, tile_glo, tile_ghi = _group_sort_plan(h2c[:, D + ROUTE_GROUP_LANE].astype(jnp.int32), moe_tm)
    y = _moe(h2c, tile_glo, tile_ghi, slot, ffn_norm[layer][None], w_gate[layer], w_up[layer], w_down[layer],
             final_norm[None], moe_tm)
    return y.reshape(B, S, D)
```

```python
import functools
import math

import jax
import jax.numpy as jnp
from jax import lax
from jax.experimental import pallas as pl
from jax.experimental.pallas import tpu as pltpu

F32 = jnp.float32
BF16 = jnp.bfloat16

EPS = 1e-6
CHUNK = 64
N_DIFF_HEADS = 4
DIFF_DK = 64
DIFF_DV = 128
DIFF_WIDTH = N_DIFF_HEADS * DIFF_DV
POOL_WINDOWS = (2, 4, 8, 16)
POOL_DIM = 128
POOL_WIDTH = len(POOL_WINDOWS) * POOL_DIM
MAX_WINDOW = max(POOL_WINDOWS)
REL_BUCKETS = 32
REL_MAX_DISTANCE = 128
N_CROSS_HEADS = 4
N_EXPERT_GROUPS = 4
EXPERTS_PER_GROUP = 4
N_EXPERTS = N_EXPERT_GROUPS * EXPERTS_PER_GROUP
LAMBDA_INIT = 0.8 - 0.6 * math.exp(-0.3 * 0)

LANES = 128
SUBLANES = 8
MXU_DIM = 256
VMEM_LIMIT_BYTES = 56 * 1024 * 1024

ATTN_BLOCK = 2 * MXU_DIM
FAR_BUCKET = REL_BUCKETS // 2 - 1
MASK_VALUE = -1e30
M_FINITE = float(jnp.finfo(jnp.float32).max)
LOG2E = math.log2(math.e)
V_AUG_ROWS = DIFF_DV + 16

ROUTER_ROWS = LANES // 2
GROUP_ROW_STRIDE = SUBLANES
ROUTE_GROUP_LANE = GROUP_ROW_STRIDE


def _rms(x, g):
    return x * lax.rsqrt(jnp.mean(x * x, axis=-1, keepdims=True) + EPS) * g


def _dot(a, b):
    return jnp.dot(a, b, preferred_element_type=F32)


def _dot_nt(a, b):
    return lax.dot_general(a, b, (((1,), (1,)), ((), ())), preferred_element_type=F32)


def _in_proj_kernel(x_ref, g_ref, win_ref, qt_ref, vt_ref, k_ref, u_ref, wqv_t_s, wku_s):
    @pl.when((pl.program_id(0) == 0) & (pl.program_id(1) == 0))
    def _():
        w = win_ref[...]
        wqv_t_s[:DIFF_WIDTH] = w[:, :DIFF_WIDTH].T.astype(BF16)
        wqv_t_s[DIFF_WIDTH:] = w[:, 2 * DIFF_WIDTH:3 * DIFF_WIDTH].T.astype(BF16)
        wku_s[:, :DIFF_WIDTH] = w[:, DIFF_WIDTH:2 * DIFF_WIDTH].astype(BF16)
        wku_s[:, DIFF_WIDTH:] = w[:, 3 * DIFF_WIDTH:].astype(BF16)

    hn = _rms(x_ref[...], g_ref[...]).astype(BF16)
    zt = _dot_nt(wqv_t_s[...], hn)
    qt_ref[...] = (zt[:DIFF_WIDTH] * (DIFF_DK ** -0.5 * LOG2E)).astype(BF16)
    for c in range(vt_ref.shape[0]):
        vt_ref[c] = zt[DIFF_WIDTH:, c * ATTN_BLOCK:(c + 1) * ATTN_BLOCK].astype(BF16)
    z = _dot(hn, wku_s[...])
    k_ref[...] = z[:, :DIFF_WIDTH].astype(BF16)
    u_ref[...] = z[:, DIFF_WIDTH:]


def _in_proj(x, attn_norm, w_in, tm):
    B, S, D = x.shape
    nkb = tm // ATTN_BLOCK
    return pl.pallas_call(
        _in_proj_kernel,
        grid=(B, S // tm),
        in_specs=[
            pl.BlockSpec((None, tm, D), lambda b, i: (b, i, 0)),
            pl.BlockSpec((1, D), lambda b, i: (0, 0)),
            pl.BlockSpec(w_in.shape, lambda b, i: (0, 0)),
        ],
        out_specs=[
            pl.BlockSpec((None, DIFF_WIDTH, tm), lambda b, i: (b, 0, i)),
            pl.BlockSpec((None, nkb, DIFF_WIDTH, ATTN_BLOCK), lambda b, i: (b, i, 0, 0)),
            pl.BlockSpec((None, tm, DIFF_WIDTH), lambda b, i: (b, i, 0)),
            pl.BlockSpec((None, tm, POOL_WIDTH), lambda b, i: (b, i, 0)),
        ],
        out_shape=[
            jax.ShapeDtypeStruct((B, DIFF_WIDTH, S), BF16),
            jax.ShapeDtypeStruct((B, S // ATTN_BLOCK, DIFF_WIDTH, ATTN_BLOCK), BF16),
            jax.ShapeDtypeStruct((B, S, DIFF_WIDTH), BF16),
            jax.ShapeDtypeStruct((B, S, POOL_WIDTH), F32),
        ],
        scratch_shapes=[pltpu.VMEM((2 * DIFF_WIDTH, D), BF16), pltpu.VMEM((D, DIFF_WIDTH + POOL_WIDTH), BF16)],
        compiler_params=pltpu.CompilerParams(
            dimension_semantics=("arbitrary", "arbitrary"), vmem_limit_bytes=VMEM_LIMIT_BYTES),
        name="in_proj",
    )(x, attn_norm, w_in)


def _rel_bucket(rel):
    nb = REL_BUCKETS // 2
    max_exact = nb // 2
    ret = (rel > 0).astype(jnp.int32) * nb
    n = jnp.abs(rel)
    nf = jnp.maximum(n, 1).astype(jnp.float32)
    large = max_exact + (jnp.log(nf / max_exact) / math.log(REL_MAX_DISTANCE / max_exact)
                         * (nb - max_exact)).astype(jnp.int32)
    large = jnp.minimum(large, nb - 1)
    return ret + jnp.where(n < max_exact, n, large)


def _bias_tile_kernel(band_ref, out_ref):
    n = ATTN_BLOCK
    kk = lax.broadcasted_iota(jnp.int32, (n, n), 0)
    qq = lax.broadcasted_iota(jnp.int32, (n, n), 1)
    for d in range(2):
        band = jnp.broadcast_to(band_ref[d], (n, 2 * n))
        tile = pltpu.roll(band, n + 1, axis=1, stride=1, stride_axis=0)[:, :n]
        if d == 0:
            tile = jnp.where(kk // CHUNK <= qq // CHUNK, tile, MASK_VALUE)
        out_ref[d] = tile
    out_ref[2] = jnp.full((n, n), MASK_VALUE, F32)


def _bias_tiles(rel_bias):
    n = ATTN_BLOCK
    m = jnp.arange(2 * n, dtype=jnp.int32)
    rel = jnp.stack([n - 1 - m, -1 - m])
    band = (rel_bias[_rel_bucket(rel)] - rel_bias[FAR_BUCKET]) * LOG2E
    band = band.transpose(2, 0, 1)[:, :, None, :]
    return pl.pallas_call(
        _bias_tile_kernel,
        grid=(N_DIFF_HEADS,),
        in_specs=[pl.BlockSpec((None, 2, 1, 2 * n), lambda h: (h, 0, 0, 0))],
        out_specs=pl.BlockSpec((None, 3, n, n), lambda h: (h, 0, 0, 0)),
        out_shape=jax.ShapeDtypeStruct((N_DIFF_HEADS, 3, n, n), F32),
        name="bias_tiles",
    )(band)


def _attn_kernel(far_ref, qta_ref, qtb_ref, k_ref, vt_ref, dt_ref, lam_ref, g_ref, oa_ref, ob_ref,
                 kaug, vaug, qaug, qref, m_s, acc_s, s_buf, p_buf):
    h = pl.program_id(1)
    step = pl.program_id(2)
    nkb = vt_ref.shape[0]
    QB = ATTN_BLOCK
    KB = ATTN_BLOCK
    q_block = (step, nkb - 1 - step)

    @pl.when(step == 0)
    def _():
        lane = lax.broadcasted_iota(jnp.int32, (KB, LANES), 1)
        pad = jnp.where(lane == 0, far_ref[h, 0],
                        jnp.where(lane == 1, far_ref[h, 1],
                                  jnp.where(lane == 2, far_ref[h, 2],
                                            jnp.where(lane < 6, 1.0, 0.0)))).astype(BF16)
        vrow = lax.broadcasted_iota(jnp.int32, (V_AUG_ROWS - DIFF_DV, KB), 0)
        ones_row = jnp.where(vrow == 0, 1.0, 0.0).astype(BF16)

        def fill(c, carry):
            r = pl.multiple_of(c * KB, KB)
            kaug[pl.ds(r, KB), :LANES] = k_ref[pl.ds(r, KB), :]
            kaug[pl.ds(r, KB), LANES:] = pad
            vaug[c, :DIFF_DV] = vt_ref[c]
            vaug[c, DIFF_DV:] = ones_row
            return carry

        lax.fori_loop(0, nkb, fill, 0)

    zeros = jnp.zeros((DIFF_DK, QB), BF16)
    row = lax.broadcasted_iota(jnp.int32, (LANES, QB), 0)
    ones_blk = jnp.where(row < 3, 1.0, 0.0)

    def fill_query_rows(buf, w, q):
        buf[w, 0, :DIFF_DK] = q[:DIFF_DK]
        buf[w, 0, DIFF_DK:2 * DIFF_DK] = zeros
        buf[w, 1, :DIFF_DK] = zeros
        buf[w, 1, DIFF_DK:2 * DIFF_DK] = q[DIFF_DK:]

    for w, qt_ref in enumerate((qta_ref, qtb_ref)):
        q = qt_ref[...]
        fill_query_rows(qref, w, q)
        rd = pl.multiple_of(q_block[w] * KB, KB)
        k_own_t = kaug[pl.ds(rd, KB), :LANES].astype(F32).T
        qk = q.astype(F32) * k_own_t
        for half in range(2):
            m = jnp.sum(qk[half * DIFF_DK:(half + 1) * DIFF_DK], axis=0, keepdims=True) + far_ref[h, 3]
            m_s[w, half] = m
            nm = -m
            hi = nm.astype(BF16).astype(F32)
            mid = (nm - hi).astype(BF16).astype(F32)
            lo = (nm - hi - mid).astype(BF16).astype(F32)
            qref[w, half, 2 * DIFF_DK:] = jnp.where(
                row == 3, hi, jnp.where(row == 4, mid, jnp.where(row == 5, lo, ones_blk))).astype(BF16)
    acc_s[...] = jnp.zeros(acc_s.shape, F32)

    i_a, i_b = q_block
    near_a = (0, jnp.maximum(i_a - 1, 0))
    biased = ((0, i_a), near_a, (1, i_b), (1, i_b - 1))
    bias_tile = (0, jnp.where(i_a >= 1, 1, 2), 0, 1)
    n_far_a = jnp.maximum(i_a - 1, 0)
    n_far = n_far_a + i_b - 1

    def far_block(f):
        w = (f >= n_far_a).astype(jnp.int32)
        return w, f - w * n_far_a

    def probabilities(par, blk, bias):
        w, j = blk
        r = pl.multiple_of(j * KB, KB)
        kblk = kaug[pl.ds(r, KB), :]
        for half in range(2):
            s = _dot(kblk, qref[w, half])
            if bias is not None:
                s = s + dt_ref[bias]
            p_buf[par, half] = jnp.exp2(s.astype(BF16))

    def weighted_values(par, blk):
        w, j = blk
        vblk = vaug[j]
        for half in range(2):
            acc_s[w, half] += _dot(vblk, p_buf[par, half])

    probabilities(0, biased[0], bias_tile[0])
    for u in range(1, 4):
        weighted_values((u - 1) % 2, biased[u - 1])
        probabilities(u % 2, biased[u], bias_tile[u])

    def far_step(par, f):
        w_prev, j_prev = far_block(f - 1)
        first = f == 0
        weighted_values(1 - par, (jnp.where(first, biased[3][0], w_prev), jnp.where(first, biased[3][1], j_prev)))
        probabilities(par, far_block(f), None)

    def far_pair(g, carry):
        far_step(0, 2 * g)
        far_step(1, 2 * g + 1)
        return carry

    lax.fori_loop(0, n_far // 2, far_pair, 0)

    @pl.when(n_far % 2 == 1)
    def _():
        far_step(0, n_far - 1)

    w_last, j_last = far_block(n_far - 1)
    last = (jnp.where(n_far == 0, biased[3][0], w_last), jnp.where(n_far == 0, biased[3][1], j_last))
    for par in range(2):
        @pl.when((n_far + 1) % 2 == par)
        def _():
            weighted_values(par, last)

    lam = (jnp.exp(jnp.sum(lam_ref[0:1] * lam_ref[1:2], axis=-1, keepdims=True))
           - jnp.exp(jnp.sum(lam_ref[2:3] * lam_ref[3:4], axis=-1, keepdims=True)) + LAMBDA_INIT)

    def finalize(w, o_ref):
        l0 = acc_s[w, 0, DIFF_DV:DIFF_DV + 1]
        l1 = acc_s[w, 1, DIFF_DV:DIFF_DV + 1]
        ot = acc_s[w, 0, :DIFF_DV] / l0 - lam * (acc_s[w, 1, :DIFF_DV] / l1)
        inv = lax.rsqrt(jnp.mean(ot * ot, axis=0, keepdims=True) + EPS)
        o_ref[...] = (((ot * inv) * g_ref[...]) * (1.0 - LAMBDA_INIT)).T.astype(BF16)
        bad = jnp.where(jnp.abs(ot) <= M_FINITE, 0.0, 1.0) + jnp.where(jnp.maximum(l0, l1) <= M_FINITE, 0.0, 1.0)
        return jnp.max(bad) > 0.0

    for w, (qt_ref, o_ref) in enumerate(((qta_ref, oa_ref), (qtb_ref, ob_ref))):
        overflowed = finalize(w, o_ref)

        @pl.when(overflowed)
        def _():
            i_q = q_block[w]
            fill_query_rows(qaug, 0, qt_ref[...])
            qaug[0, 0, 2 * DIFF_DK:] = ones_blk.astype(BF16)
            qaug[0, 1, 2 * DIFF_DK:] = ones_blk.astype(BF16)
            acc_s[w] = jnp.zeros(acc_s.shape[1:], F32)

            def online_step(j, kind):
                r = pl.multiple_of(j * KB, KB)
                kblk = kaug[pl.ds(r, KB), :]
                vblk = vaug[j]
                for half in range(2):
                    s_buf[half] = _dot(kblk, qaug[0, half])
                    if kind is not None:
                        s_buf[half] += dt_ref[kind]
                    m_old = m_s[w, half]
                    m_new = jnp.maximum(m_old, jnp.max(s_buf[half], axis=0, keepdims=True))
                    p = jnp.exp2((s_buf[half] - m_new).astype(BF16))
                    acc_s[w, half] = jnp.exp2(m_old - m_new) * acc_s[w, half] + _dot(vblk, p)
                    m_s[w, half] = m_new

            online_step(i_q, 0)

            @pl.when(i_q >= 1)
            def _():
                online_step(i_q - 1, 1)

            def far_online(j, carry):
                online_step(j, None)
                return carry

            lax.fori_loop(0, jnp.maximum(i_q - 1, 0), far_online, 0)
            finalize(w, o_ref)


def _diff_attn(far_split, qt, k, vt, dtiles, lam_params, subln_cols):
    B, S, _ = k.shape
    nq = S // ATTN_BLOCK
    half_blocks = nq // 2
    out_half = jax.ShapeDtypeStruct((B, S // 2, DIFF_WIDTH), BF16)
    return pl.pallas_call(
        _attn_kernel,
        grid=(B, N_DIFF_HEADS, half_blocks),
        in_specs=[
            pl.BlockSpec(memory_space=pltpu.SMEM),
            pl.BlockSpec((None, 2 * DIFF_DK, ATTN_BLOCK), lambda b, h, s: (b, h, s)),
            pl.BlockSpec((None, 2 * DIFF_DK, ATTN_BLOCK), lambda b, h, s: (b, h, nq - 1 - s)),
            pl.BlockSpec((None, S, 2 * DIFF_DK), lambda b, h, s: (b, 0, h)),
            pl.BlockSpec((None, nq, DIFF_DV, ATTN_BLOCK), lambda b, h, s: (b, 0, h, 0)),
            pl.BlockSpec((None, 3, ATTN_BLOCK, ATTN_BLOCK), lambda b, h, s: (h, 0, 0, 0)),
            pl.BlockSpec(lam_params.shape, lambda b, h, s: (0, 0)),
            pl.BlockSpec((DIFF_DV, ATTN_BLOCK), lambda b, h, s: (0, 0)),
        ],
        out_specs=[
            pl.BlockSpec((None, ATTN_BLOCK, DIFF_DV), lambda b, h, s: (b, s, h)),
            pl.BlockSpec((None, ATTN_BLOCK, DIFF_DV), lambda b, h, s: (b, half_blocks - 1 - s, h)),
        ],
        out_shape=[out_half, out_half],
        scratch_shapes=[
            pltpu.VMEM((S, 2 * LANES), BF16),
            pltpu.VMEM((nq, V_AUG_ROWS, ATTN_BLOCK), BF16),
            pltpu.VMEM((1, 2, 2 * LANES, ATTN_BLOCK), BF16),
            pltpu.VMEM((2, 2, 2 * LANES, ATTN_BLOCK), BF16),
            pltpu.VMEM((2, 2, 1, ATTN_BLOCK), F32),
            pltpu.VMEM((2, 2, V_AUG_ROWS, ATTN_BLOCK), F32),
            pltpu.VMEM((2, ATTN_BLOCK, ATTN_BLOCK), F32),
            pltpu.VMEM((2, 2, ATTN_BLOCK, ATTN_BLOCK), BF16),
        ],
        compiler_params=pltpu.CompilerParams(
            dimension_semantics=("arbitrary", "arbitrary", "arbitrary"), vmem_limit_bytes=VMEM_LIMIT_BYTES),
        name="diff_attn",
    )(far_split, qt, qt, k, vt, dtiles, lam_params, subln_cols)


def _mem_kv_kernel(mem_ref, g_ref, wk_t_ref, wv_ref, kt_ref, v_ref):
    mn = _rms(mem_ref[...], g_ref[...]).astype(BF16)
    kt_ref[...] = _dot_nt(wk_t_ref[...], mn).astype(BF16)
    v_ref[...] = _dot(mn, wv_ref[...]).astype(BF16)


def _mem_kv(mem, mem_norm, wk_t, wv):
    B, M, D = mem.shape
    return pl.pallas_call(
        _mem_kv_kernel,
        grid=(B,),
        in_specs=[
            pl.BlockSpec((None, M, D), lambda b: (b, 0, 0)),
            pl.BlockSpec((1, D), lambda b: (0, 0)),
            pl.BlockSpec((D, D), lambda b: (0, 0)),
            pl.BlockSpec((D, D), lambda b: (0, 0)),
        ],
        out_specs=[
            pl.BlockSpec((None, D, M), lambda b: (b, 0, 0)),
            pl.BlockSpec((None, M, D), lambda b: (b, 0, 0)),
        ],
        out_shape=[jax.ShapeDtypeStruct((B, D, M), BF16), jax.ShapeDtypeStruct((B, M, D), BF16)],
        compiler_params=pltpu.CompilerParams(
            dimension_semantics=("arbitrary",), vmem_limit_bytes=VMEM_LIMIT_BYTES),
        name="mem_kv",
    )(mem, mem_norm, wk_t, wv)


def _first_argmax_rows(v, vmax):
    row = lax.broadcasted_iota(jnp.int32, v.shape, 0)
    return jnp.min(jnp.where(v == vmax, row, v.shape[0]), axis=0, keepdims=True)


def _mix_cross_kernel(alo_ref, ahi_ref, u_ref, uprev_ref, x_ref, wout_ref, pw_ref, ps_ref, cn_ref, wq_ref, kt_ref, v_ref,
                      wo_ref, fn_ref, wrh_ref, wrl_ref, rb_ref, h2c_ref):
    i = pl.program_id(1)
    tm = u_ref.shape[0]
    D = x_ref.shape[1]
    hd = kt_ref.shape[0] // N_CROSS_HEADS

    u = u_ref[...]
    prev = jnp.where(i > 0, uprev_ref[...], 0.0)
    pos1 = (i * tm + 1 + lax.broadcasted_iota(jnp.int32, (tm, 1), 0)).astype(F32)
    ys = []
    for g, w in enumerate(POOL_WINDOWS):
        sl = slice(g * POOL_DIM, (g + 1) * POOL_DIM)
        acc = jnp.concatenate([prev[:, sl], u[:, sl]], axis=0)
        span = 1
        while span < w:
            acc = acc[span:] + acc[:-span]
            span *= 2
        first = MAX_WINDOW - (w - 1)
        mean = acc[first:first + tm] / jnp.minimum(pos1, float(w))
        d = (mean - u[:, sl]).astype(BF16)
        ys.append(_dot(d, pw_ref[g]) * ps_ref[:, sl])
    p = jnp.concatenate(ys, axis=-1).astype(BF16)

    a = jnp.where(i < pl.num_programs(1) // 2, alo_ref[...], ahi_ref[...])
    h1 = x_ref[...] + _dot(jnp.concatenate([a, p], axis=-1), wout_ref[...])

    q = _dot(_rms(h1, cn_ref[...]).astype(BF16), wq_ref[...]).astype(BF16)
    outs = []
    for c in range(N_CROSS_HEADS):
        sl = slice(c * hd, (c + 1) * hd)
        s = _dot(q[:, sl], kt_ref[sl, :]) * (hd ** -0.5)
        e = jnp.exp(s - jnp.max(s, axis=-1, keepdims=True))
        pr = (e / jnp.sum(e, axis=-1, keepdims=True)).astype(BF16)
        outs.append(_dot(pr, v_ref[:, sl]))
    o = jnp.concatenate(outs, axis=-1).astype(BF16)
    h2 = h1 + _dot(o, wo_ref[...])
    h2c_ref[:, :D] = h2

    hf = _rms(h2, fn_ref[...])
    hf_hi = hf.astype(BF16)
    hf_lo = (hf - hf_hi.astype(F32)).astype(BF16)
    lg = (_dot_nt(wrh_ref[...], hf_hi) + _dot_nt(wrl_ref[...], hf_hi) + _dot_nt(wrh_ref[...], hf_lo)
          + rb_ref[...])
    R = GROUP_ROW_STRIDE
    gl = lg[:R]
    gmax = jnp.max(gl, axis=0, keepdims=True)
    gsel = _first_argmax_rows(gl, gmax)
    g_w = 1.0 / jnp.sum(jnp.exp(gl - gmax), axis=0, keepdims=True)
    el = lg[R:2 * R]
    for g in range(1, N_EXPERT_GROUPS):
        el = jnp.where(gsel == g, lg[R * (g + 1):R * (g + 2)], el)
    row = lax.broadcasted_iota(jnp.int32, el.shape, 0)
    v0 = jnp.max(el, axis=0, keepdims=True)
    i0 = _first_argmax_rows(el, v0)
    el1 = jnp.where(row == i0, -jnp.inf, el)
    v1 = jnp.max(el1, axis=0, keepdims=True)
    i1 = _first_argmax_rows(el1, v1)
    t = jnp.exp(v1 - v0)
    w0 = 1.0 / (1.0 + t)
    w1 = t / (1.0 + t)
    inner = jnp.where(row == i0, w0, 0.0) + jnp.where(row == i1, w1, 0.0)
    blocks = [g_w * inner, jnp.where(row == 0, gsel.astype(F32), 0.0),
              jnp.zeros((LANES - 2 * R, tm), F32)]
    h2c_ref[:, D:] = jnp.concatenate(blocks, axis=0).T


def _mix_cross(a_lo, a_hi, u, x, wout, pool_w, pool_scale, cross_norm, wq, kt, v, wo, ffn_norm, wr_hi, wr_lo, rbias, tm):
    B, S, D = x.shape
    M = v.shape[1]
    prev_blocks = tm // MAX_WINDOW
    half_tiles = S // tm // 2
    const2 = lambda b, i: (0, 0)
    return pl.pallas_call(
        _mix_cross_kernel,
        grid=(B, S // tm),
        in_specs=[
            pl.BlockSpec((None, tm, DIFF_WIDTH), lambda b, i: (b, jnp.minimum(i, half_tiles - 1), 0)),
            pl.BlockSpec((None, tm, DIFF_WIDTH), lambda b, i: (b, jnp.maximum(i - half_tiles, 0), 0)),
            pl.BlockSpec((None, tm, POOL_WIDTH), lambda b, i: (b, i, 0)),
            pl.BlockSpec((None, MAX_WINDOW, POOL_WIDTH), lambda b, i: (b, jnp.maximum(i * prev_blocks - 1, 0), 0)),
            pl.BlockSpec((None, tm, D), lambda b, i: (b, i, 0)),
            pl.BlockSpec(wout.shape, const2),
            pl.BlockSpec(pool_w.shape, lambda b, i: (0, 0, 0)),
            pl.BlockSpec((1, POOL_WIDTH), const2),
            pl.BlockSpec((1, D), const2),
            pl.BlockSpec(wq.shape, const2),
            pl.BlockSpec((None, D, M), lambda b, i: (b, 0, 0)),
            pl.BlockSpec((None, M, D), lambda b, i: (b, 0, 0)),
            pl.BlockSpec(wo.shape, const2),
            pl.BlockSpec((1, D), const2),
            pl.BlockSpec(wr_hi.shape, const2),
            pl.BlockSpec(wr_lo.shape, const2),
            pl.BlockSpec(rbias.shape, const2),
        ],
        out_specs=pl.BlockSpec((None, tm, D + LANES), lambda b, i: (b, i, 0)),
        out_shape=jax.ShapeDtypeStruct((B, S, D + LANES), F32),
        compiler_params=pltpu.CompilerParams(
            dimension_semantics=("arbitrary", "arbitrary"), vmem_limit_bytes=VMEM_LIMIT_BYTES),
        name="mix_cross",
    )(a_lo, a_hi, u, u, x, wout, pool_w, pool_scale, cross_norm, wq, kt, v, wo, ffn_norm, wr_hi, wr_lo, rbias)


def _moe_kernel(glo_ref, ghi_ref, slot_ref, h2c_hbm, wg_hbm, wu_hbm, wd_hbm, fn_ref, wg_ref, wu_ref, wd_ref, gn_ref,
                y_hbm, src_ref, xbuf, ybuf, hf_s, acc_s, wg_x, wu_x, wd_x, gsem, ssem, wsem, fence_sem):
    c = pl.program_id(0)
    ntiles = pl.num_programs(0)
    tm, D = hf_s.shape
    slot = c % 2

    @pl.when(c == 0)
    def _():
        def invert(t, carry):
            src_ref[slot_ref[t]] = t
            return carry
        lax.fori_loop(0, slot_ref.shape[0], invert, 0, unroll=8)

    def row_gather(tile, r, s):
        t = src_ref[tile * tm + r]
        return pltpu.make_async_copy(h2c_hbm.at[pl.ds(t, 1)], xbuf.at[s, pl.ds(r, 1)], gsem.at[s])

    def row_scatter(tile, r, s):
        t = src_ref[tile * tm + r]
        return pltpu.make_async_copy(ybuf.at[s, pl.ds(r, 1)], y_hbm.at[pl.ds(t, 1)], ssem.at[s])

    def start_rows(make, tile):
        def body(r, carry):
            make(tile, r, tile % 2).start()
            return carry
        lax.fori_loop(0, tm, body, 0, unroll=8)

    def wait_gather(s):
        pltpu.make_async_copy(h2c_hbm.at[pl.ds(0, tm)], xbuf.at[s], gsem.at[s]).wait()

    def wait_scatter(s):
        pltpu.make_async_copy(ybuf.at[s], y_hbm.at[pl.ds(0, tm)], ssem.at[s]).wait()

    @pl.when(c == 0)
    def _():
        start_rows(row_gather, 0)

    wait_gather(slot)

    x = xbuf[slot]
    h2 = x[:, :D]
    hf_s[...] = _rms(h2, fn_ref[...]).astype(BF16)
    group = x[:, D + ROUTE_GROUP_LANE:D + ROUTE_GROUP_LANE + 1]

    def expert_pass(g, wg, wu, wd, first, neighbour_slot=None):
        hf = hf_s[...]
        in_group = group == g.astype(F32)
        n_segments = EXPERTS_PER_GROUP
        rows_per_segment = tm // n_segments
        experts_per_segment = EXPERTS_PER_GROUP // n_segments
        for e in range(EXPERTS_PER_GROUP):
            seg, first_in_seg = divmod(e, experts_per_segment)
            if neighbour_slot is not None and first_in_seg == 0:
                half_segments = n_segments // 2
                first_row = (seg % half_segments) * 2 * rows_per_segment
                for r in range(first_row, first_row + 2 * rows_per_segment):
                    if seg < half_segments:
                        row_gather(c + 1, r, neighbour_slot).start(priority=r % 2)
                    else:
                        row_scatter(c - 1, r, neighbour_slot).start(priority=r % 2)
            hg = _dot(hf, wg[e].astype(BF16))
            hu = _dot(hf, wu[e].astype(BF16))
            cw = jnp.where(in_group, x[:, D + e:D + e + 1], 0.0)
            act = (hg * (1.0 / (1.0 + jnp.exp(-hg))) * hu * cw).astype(BF16)
            part = _dot(act, wd[e].astype(BF16))
            if first and e == 0:
                acc_s[...] = part
            else:
                acc_s[...] += part
            if neighbour_slot is not None and first_in_seg == experts_per_segment - 1:
                pl.semaphore_signal(fence_sem, 1)
                pl.semaphore_wait(fence_sem, 1)

    interior = (c >= 1) & (c + 1 < ntiles)
    for par in range(2):
        @pl.when(interior & (slot == par))
        def _():
            expert_pass(glo_ref[c], wg_ref, wu_ref, wd_ref, True, neighbour_slot=1 - par)

    @pl.when(jnp.logical_not(interior))
    def _():
        @pl.when(c + 1 < ntiles)
        def _():
            start_rows(row_gather, c + 1)

        @pl.when(c >= 1)
        def _():
            start_rows(row_scatter, c - 1)

        expert_pass(glo_ref[c], wg_ref, wu_ref, wd_ref, True)

    def extra_pass(g, carry):
        copies = [pltpu.make_async_copy(w_hbm.at[pl.ds(g * EXPERTS_PER_GROUP, EXPERTS_PER_GROUP)], w_x, wsem.at[n])
                  for n, (w_hbm, w_x) in enumerate(((wg_hbm, wg_x), (wu_hbm, wu_x), (wd_hbm, wd_x)))]
        for cp in copies:
            cp.start()
        for cp in copies:
            cp.wait()
        expert_pass(g, wg_x, wu_x, wd_x, False)
        return carry

    lax.fori_loop(glo_ref[c] + 1, ghi_ref[c] + 1, extra_pass, 0)

    @pl.when(c >= 2)
    def _():
        wait_scatter(slot)

    ybuf[slot] = _rms(h2 + acc_s[...], gn_ref[...])

    @pl.when(c == ntiles - 1)
    def _():
        start_rows(row_scatter, c)
        wait_scatter(slot)

        @pl.when(ntiles >= 2)
        def _():
            wait_scatter(1 - slot)


def _moe(h2c, tile_glo, tile_ghi, slot, ffn_norm, wg, wu, wd, final_norm, tm):
    T, DL = h2c.shape
    D = DL - LANES
    FF = wg.shape[-1]
    E = EXPERTS_PER_GROUP
    any_spec = pl.BlockSpec(memory_space=pl.ANY)
    grid_spec = pltpu.PrefetchScalarGridSpec(
        num_scalar_prefetch=3,
        grid=(T // tm,),
        in_specs=[
            any_spec, any_spec, any_spec, any_spec,
            pl.BlockSpec((1, D), lambda c, glo, ghi, src: (0, 0)),
            pl.BlockSpec((E, D, FF), lambda c, glo, ghi, src: (glo[c], 0, 0)),
            pl.BlockSpec((E, D, FF), lambda c, glo, ghi, src: (glo[c], 0, 0)),
            pl.BlockSpec((E, FF, D), lambda c, glo, ghi, src: (glo[c], 0, 0)),
            pl.BlockSpec((1, D), lambda c, glo, ghi, src: (0, 0)),
        ],
        out_specs=any_spec,
        scratch_shapes=[
            pltpu.SMEM((T,), jnp.int32),
            pltpu.VMEM((2, tm, DL), F32),
            pltpu.VMEM((2, tm, D), F32),
            pltpu.VMEM((tm, D), BF16),
            pltpu.VMEM((tm, D), F32),
            pltpu.VMEM((E, D, FF), wg.dtype),
            pltpu.VMEM((E, D, FF), wu.dtype),
            pltpu.VMEM((E, FF, D), wd.dtype),
            pltpu.SemaphoreType.DMA((2,)),
            pltpu.SemaphoreType.DMA((2,)),
            pltpu.SemaphoreType.DMA((3,)),
            pltpu.SemaphoreType.REGULAR,
        ],
    )
    return pl.pallas_call(
        _moe_kernel,
        grid_spec=grid_spec,
        out_shape=jax.ShapeDtypeStruct((T, D), F32),
        compiler_params=pltpu.CompilerParams(
            dimension_semantics=("arbitrary",), vmem_limit_bytes=VMEM_LIMIT_BYTES),
        name="moe",
    )(tile_glo, tile_ghi, slot, h2c, wg, wu, wd, ffn_norm, wg, wu, wd, final_norm)


def _group_sort_plan(group, tm):
    T = group.shape[0]
    onehot = (group[:, None] == jnp.arange(N_EXPERT_GROUPS, dtype=jnp.int32)[None, :]).astype(jnp.int32)
    ranks = jnp.cumsum(onehot, axis=0) - onehot
    counts = jnp.sum(onehot, axis=0)
    starts = jnp.cumsum(counts) - counts
    slot = jnp.sum(onehot * (ranks + starts[None, :]), axis=1)
    first = jnp.arange(0, T, tm, dtype=jnp.int32)
    group_at = lambda pos: jnp.sum((starts[None, 1:] <= pos[:, None]).astype(jnp.int32), axis=1)
    return slot, group_at(first), group_at(first + (tm - 1))


def _split3_bf16(c):
    hi = c.astype(BF16).astype(F32)
    mid = (c - hi).astype(BF16).astype(F32)
    lo = (c - hi - mid).astype(BF16).astype(F32)
    return jnp.stack([hi, mid, lo], axis=-1)


def _router_operands(router_group, router_group_bias, router_expert, router_expert_bias):
    D = router_group.shape[0]
    R = GROUP_ROW_STRIDE
    w = jnp.zeros((ROUTER_ROWS, D), F32)
    bias = jnp.zeros((ROUTER_ROWS,), F32)
    slab_pad = jnp.full((R - N_EXPERT_GROUPS,), MASK_VALUE, F32)
    w = w.at[:N_EXPERT_GROUPS].set(router_group.T)
    bias = bias.at[:R].set(jnp.concatenate([router_group_bias, slab_pad]))
    for g in range(N_EXPERT_GROUPS):
        w = w.at[R * (g + 1):R * (g + 1) + EXPERTS_PER_GROUP].set(router_expert[g].T)
        bias = bias.at[R * (g + 1):R * (g + 2)].set(jnp.concatenate([router_expert_bias[g], slab_pad]))
    w_hi = w.astype(BF16)
    w_lo = (w - w_hi.astype(F32)).astype(BF16)
    return w_hi, w_lo, bias[:, None]


def kernel(x, mem, rel_bias, attn_norm, w_in, lambda_q1, lambda_k1, lambda_q2, lambda_k2, diff_subln, pool_w,
           pool_scale, w_out, cross_norm, mem_norm, wq_cross, wkv_cross, wo_cross, ffn_norm, router_group,
           router_group_bias, router_expert, router_expert_bias, w_gate, w_up, w_down, final_norm):
    B, S, D = x.shape
    layer = 0
    qt, vt, k, u = _in_proj(x, attn_norm[layer][None], w_in[layer], tm=1024)

    dtiles = _bias_tiles(rel_bias)
    far_split = jnp.concatenate([_split3_bf16(rel_bias[FAR_BUCKET] * LOG2E), rel_bias[0][:, None] * LOG2E],
                                axis=1)
    lam_params = jnp.stack([lambda_q1[layer], lambda_k1[layer], lambda_q2[layer], lambda_k2[layer]])
    subln_cols = jnp.broadcast_to(diff_subln[layer][:, None], (DIFF_DV, ATTN_BLOCK))
    a_lo, a_hi = _diff_attn(far_split, qt, k, vt, dtiles, lam_params, subln_cols)

    wkv = wkv_cross[layer]
    kt, v = _mem_kv(mem, mem_norm[layer][None], wkv[:, :D].T.astype(BF16), wkv[:, D:].astype(BF16))

    wr_hi, wr_lo, rbias = _router_operands(router_group[layer], router_group_bias[layer],
                                           router_expert[layer], router_expert_bias[layer])
    h2c = _mix_cross(a_lo, a_hi, u, x, w_out[layer].astype(BF16), pool_w[layer].astype(BF16), pool_scale[layer][None],
                     cross_norm[layer][None], wq_cross[layer].astype(BF16), kt, v,
                     wo_cross[layer].astype(BF16), ffn_norm[layer][None], wr_hi, wr_lo, rbias, tm=1024)
    h2c = h2c.reshape(B * S, D + LANES)

    moe_tm = 512
    slot, tile_glo, tile_ghi = _group_sort_plan(h2c[:, D + ROUTE_GROUP_LANE].astype(jnp.int32), moe_tm)
    y = _moe(h2c, tile_glo, tile_ghi, slot, ffn_norm[layer][None], w_gate[layer], w_up[layer], w_down[layer],
             final_norm[None], moe_tm)
    return y.reshape(B, S, D)
```

```python
import functools
import math

import jax
import jax.numpy as jnp
from jax import lax
from jax.experimental import pallas as pl
from jax.experimental.pallas import tpu as pltpu

F32 = jnp.float32
BF16 = jnp.bfloat16

EPS = 1e-6
CHUNK = 64
N_DIFF_HEADS = 4
DIFF_DK = 64
DIFF_DV = 128
DIFF_WIDTH = N_DIFF_HEADS * DIFF_DV
POOL_WINDOWS = (2, 4, 8, 16)
POOL_DIM = 128
POOL_WIDTH = len(POOL_WINDOWS) * POOL_DIM
MAX_WINDOW = max(POOL_WINDOWS)
REL_BUCKETS = 32
REL_MAX_DISTANCE = 128
N_CROSS_HEADS = 4
N_EXPERT_GROUPS = 4
EXPERTS_PER_GROUP = 4
N_EXPERTS = N_EXPERT_GROUPS * EXPERTS_PER_GROUP
LAMBDA_INIT = 0.8 - 0.6 * math.exp(-0.3 * 0)

LANES = 128
SUBLANES = 8
MXU_DIM = 256
VMEM_LIMIT_BYTES = 56 * 1024 * 1024

ATTN_BLOCK = 2 * MXU_DIM
FAR_BUCKET = REL_BUCKETS // 2 - 1
MASK_VALUE = -1e30
M_FINITE = float(jnp.finfo(jnp.float32).max)
LOG2E = math.log2(math.e)
V_AUG_ROWS = DIFF_DV + 16

ROUTER_ROWS = LANES // 2
GROUP_ROW_STRIDE = SUBLANES
ROUTE_GROUP_LANE = GROUP_ROW_STRIDE


def _rms(x, g):
    return x * lax.rsqrt(jnp.mean(x * x, axis=-1, keepdims=True) + EPS) * g


def _dot(a, b):
    return jnp.dot(a, b, preferred_element_type=F32)


def _dot_nt(a, b):
    return lax.dot_general(a, b, (((1,), (1,)), ((), ())), preferred_element_type=F32)


def _in_proj_kernel(x_ref, g_ref, win_ref, qt_ref, vt_ref, k_ref, u_ref, wqv_t_s, wku_s):
    @pl.when((pl.program_id(0) == 0) & (pl.program_id(1) == 0))
    def _():
        w = win_ref[...]
        wqv_t_s[:DIFF_WIDTH] = w[:, :DIFF_WIDTH].T.astype(BF16)
        wqv_t_s[DIFF_WIDTH:] = w[:, 2 * DIFF_WIDTH:3 * DIFF_WIDTH].T.astype(BF16)
        wku_s[:, :DIFF_WIDTH] = w[:, DIFF_WIDTH:2 * DIFF_WIDTH].astype(BF16)
        wku_s[:, DIFF_WIDTH:] = w[:, 3 * DIFF_WIDTH:].astype(BF16)

    hn = _rms(x_ref[...], g_ref[...]).astype(BF16)
    zt = _dot_nt(wqv_t_s[...], hn)
    qt_ref[...] = (zt[:DIFF_WIDTH] * (DIFF_DK ** -0.5 * LOG2E)).astype(BF16)
    for c in range(vt_ref.shape[0]):
        vt_ref[c] = zt[DIFF_WIDTH:, c * ATTN_BLOCK:(c + 1) * ATTN_BLOCK].astype(BF16)
    z = _dot(hn, wku_s[...])
    k_ref[...] = z[:, :DIFF_WIDTH].astype(BF16)
    u_ref[...] = z[:, DIFF_WIDTH:]


def _in_proj(x, attn_norm, w_in, tm):
    B, S, D = x.shape
    nkb = tm // ATTN_BLOCK
    return pl.pallas_call(
        _in_proj_kernel,
        grid=(B, S // tm),
        in_specs=[
            pl.BlockSpec((None, tm, D), lambda b, i: (b, i, 0)),
            pl.BlockSpec((1, D), lambda b, i: (0, 0)),
            pl.BlockSpec(w_in.shape, lambda b, i: (0, 0)),
        ],
        out_specs=[
            pl.BlockSpec((None, DIFF_WIDTH, tm), lambda b, i: (b, 0, i)),
            pl.BlockSpec((None, nkb, DIFF_WIDTH, ATTN_BLOCK), lambda b, i: (b, i, 0, 0)),
            pl.BlockSpec((None, tm, DIFF_WIDTH), lambda b, i: (b, i, 0)),
            pl.BlockSpec((None, tm, POOL_WIDTH), lambda b, i: (b, i, 0)),
        ],
        out_shape=[
            jax.ShapeDtypeStruct((B, DIFF_WIDTH, S), BF16),
            jax.ShapeDtypeStruct((B, S // ATTN_BLOCK, DIFF_WIDTH, ATTN_BLOCK), BF16),
            jax.ShapeDtypeStruct((B, S, DIFF_WIDTH), BF16),
            jax.ShapeDtypeStruct((B, S, POOL_WIDTH), F32),
        ],
        scratch_shapes=[pltpu.VMEM((2 * DIFF_WIDTH, D), BF16), pltpu.VMEM((D, DIFF_WIDTH + POOL_WIDTH), BF16)],
        compiler_params=pltpu.CompilerParams(
            dimension_semantics=("arbitrary", "arbitrary"), vmem_limit_bytes=VMEM_LIMIT_BYTES),
        name="in_proj",
    )(x, attn_norm, w_in)


def _rel_bucket(rel):
    nb = REL_BUCKETS // 2
    max_exact = nb // 2
    ret = (rel > 0).astype(jnp.int32) * nb
    n = jnp.abs(rel)
    nf = jnp.maximum(n, 1).astype(jnp.float32)
    large = max_exact + (jnp.log(nf / max_exact) / math.log(REL_MAX_DISTANCE / max_exact)
                         * (nb - max_exact)).astype(jnp.int32)
    large = jnp.minimum(large, nb - 1)
    return ret + jnp.where(n < max_exact, n, large)


def _bias_tile_kernel(band_ref, out_ref):
    n = ATTN_BLOCK
    kk = lax.broadcasted_iota(jnp.int32, (n, n), 0)
    qq = lax.broadcasted_iota(jnp.int32, (n, n), 1)
    for d in range(2):
        band = jnp.broadcast_to(band_ref[d], (n, 2 * n))
        tile = pltpu.roll(band, n + 1, axis=1, stride=1, stride_axis=0)[:, :n]
        if d == 0:
            tile = jnp.where(kk // CHUNK <= qq // CHUNK, tile, MASK_VALUE)
        out_ref[d] = tile
    out_ref[2] = jnp.full((n, n), MASK_VALUE, F32)


def _bias_tiles(rel_bias):
    n = ATTN_BLOCK
    m = jnp.arange(2 * n, dtype=jnp.int32)
    rel = jnp.stack([n - 1 - m, -1 - m])
    band = (rel_bias[_rel_bucket(rel)] - rel_bias[FAR_BUCKET]) * LOG2E
    band = band.transpose(2, 0, 1)[:, :, None, :]
    return pl.pallas_call(
        _bias_tile_kernel,
        grid=(N_DIFF_HEADS,),
        in_specs=[pl.BlockSpec((None, 2, 1, 2 * n), lambda h: (h, 0, 0, 0))],
        out_specs=pl.BlockSpec((None, 3, n, n), lambda h: (h, 0, 0, 0)),
        out_shape=jax.ShapeDtypeStruct((N_DIFF_HEADS, 3, n, n), F32),
        name="bias_tiles",
    )(band)


def _attn_kernel(far_ref, qta_ref, qtb_ref, k_ref, vt_ref, dt_ref, lam_ref, g_ref, oa_ref, ob_ref,
                 kaug, vaug, qaug, qref, m_s, acc_s, s_buf, p_buf):
    h = pl.program_id(1)
    step = pl.program_id(2)
    nkb = vt_ref.shape[0]
    QB = ATTN_BLOCK
    KB = ATTN_BLOCK
    q_block = (step, nkb - 1 - step)

    @pl.when(step == 0)
    def _():
        lane = lax.broadcasted_iota(jnp.int32, (KB, LANES), 1)
        pad = jnp.where(lane == 0, far_ref[h, 0],
                        jnp.where(lane == 1, far_ref[h, 1],
                                  jnp.where(lane == 2, far_ref[h, 2],
                                            jnp.where(lane < 6, 1.0, 0.0)))).astype(BF16)
        vrow = lax.broadcasted_iota(jnp.int32, (V_AUG_ROWS - DIFF_DV, KB), 0)
        ones_row = jnp.where(vrow == 0, 1.0, 0.0).astype(BF16)

        def fill(c, carry):
            r = pl.multiple_of(c * KB, KB)
            kaug[pl.ds(r, KB), :LANES] = k_ref[pl.ds(r, KB), :]
            kaug[pl.ds(r, KB), LANES:] = pad
            vaug[c, :DIFF_DV] = vt_ref[c]
            vaug[c, DIFF_DV:] = ones_row
            return carry

        lax.fori_loop(0, nkb, fill, 0)

    zeros = jnp.zeros((DIFF_DK, QB), BF16)
    row = lax.broadcasted_iota(jnp.int32, (LANES, QB), 0)
    ones_blk = jnp.where(row < 3, 1.0, 0.0)

    def fill_query_rows(buf, w, q):
        buf[w, 0, :DIFF_DK] = q[:DIFF_DK]
        buf[w, 0, DIFF_DK:2 * DIFF_DK] = zeros
        buf[w, 1, :DIFF_DK] = zeros
        buf[w, 1, DIFF_DK:2 * DIFF_DK] = q[DIFF_DK:]

    for w, qt_ref in enumerate((qta_ref, qtb_ref)):
        q = qt_ref[...]
        fill_query_rows(qref, w, q)
        rd = pl.multiple_of(q_block[w] * KB, KB)
        k_own_t = kaug[pl.ds(rd, KB), :LANES].astype(F32).T
        qk = q.astype(F32) * k_own_t
        for half in range(2):
            m = jnp.sum(qk[half * DIFF_DK:(half + 1) * DIFF_DK], axis=0, keepdims=True) + far_ref[h, 3]
            m_s[w, half] = m
            nm = -m
            hi = nm.astype(BF16).astype(F32)
            mid = (nm - hi).astype(BF16).astype(F32)
            lo = (nm - hi - mid).astype(BF16).astype(F32)
            qref[w, half, 2 * DIFF_DK:] = jnp.where(
                row == 3, hi, jnp.where(row == 4, mid, jnp.where(row == 5, lo, ones_blk))).astype(BF16)
    acc_s[...] = jnp.zeros(acc_s.shape, F32)

    i_a, i_b = q_block
    near_a = (0, jnp.maximum(i_a - 1, 0))
    biased = ((0, i_a), near_a, (1, i_b), (1, i_b - 1))
    bias_tile = (0, jnp.where(i_a >= 1, 1, 2), 0, 1)
    n_far_a = jnp.maximum(i_a - 1, 0)
    n_far = n_far_a + i_b - 1

    def far_block(f):
        w = (f >= n_far_a).astype(jnp.int32)
        return w, f - w * n_far_a

    def probabilities(par, blk, bias):
        w, j = blk
        r = pl.multiple_of(j * KB, KB)
        kblk = kaug[pl.ds(r, KB), :]
        for half in range(2):
            s = _dot(kblk, qref[w, half])
            if bias is not None:
                s = s + dt_ref[bias]
            p_buf[par, half] = jnp.exp2(s.astype(BF16))

    def weighted_values(par, blk):
        w, j = blk
        vblk = vaug[j]
        for half in range(2):
            acc_s[w, half] += _dot(vblk, p_buf[par, half])

    probabilities(0, biased[0], bias_tile[0])
    for u in range(1, 4):
        weighted_values((u - 1) % 2, biased[u - 1])
        probabilities(u % 2, biased[u], bias_tile[u])

    def far_step(par, f):
        w_prev, j_prev = far_block(f - 1)
        first = f == 0
        weighted_values(1 - par, (jnp.where(first, biased[3][0], w_prev), jnp.where(first, biased[3][1], j_prev)))
        probabilities(par, far_block(f), None)

    def far_pair(g, carry):
        far_step(0, 2 * g)
        far_step(1, 2 * g + 1)
        return carry

    lax.fori_loop(0, n_far // 2, far_pair, 0)

    @pl.when(n_far % 2 == 1)
    def _():
        far_step(0, n_far - 1)

    w_last, j_last = far_block(n_far - 1)
    last = (jnp.where(n_far == 0, biased[3][0], w_last), jnp.where(n_far == 0, biased[3][1], j_last))
    for par in range(2):
        @pl.when((n_far + 1) % 2 == par)
        def _():
            weighted_values(par, last)

    lam = (jnp.exp(jnp.sum(lam_ref[0:1] * lam_ref[1:2], axis=-1, keepdims=True))
           - jnp.exp(jnp.sum(lam_ref[2:3] * lam_ref[3:4], axis=-1, keepdims=True)) + LAMBDA_INIT)

    def finalize(w, o_ref):
        l0 = acc_s[w, 0, DIFF_DV:DIFF_DV + 1]
        l1 = acc_s[w, 1, DIFF_DV:DIFF_DV + 1]
        ot = acc_s[w, 0, :DIFF_DV] / l0 - lam * (acc_s[w, 1, :DIFF_DV] / l1)
        inv = lax.rsqrt(jnp.mean(ot * ot, axis=0, keepdims=True) + EPS)
        o_ref[...] = (((ot * inv) * g_ref[...]) * (1.0 - LAMBDA_INIT)).T.astype(BF16)
        bad = jnp.where(jnp.abs(ot) <= M_FINITE, 0.0, 1.0) + jnp.where(jnp.maximum(l0, l1) <= M_FINITE, 0.0, 1.0)
        return jnp.max(bad) > 0.0

    for w, (qt_ref, o_ref) in enumerate(((qta_ref, oa_ref), (qtb_ref, ob_ref))):
        overflowed = finalize(w, o_ref)

        @pl.when(overflowed)
        def _():
            i_q = q_block[w]
            fill_query_rows(qaug, 0, qt_ref[...])
            qaug[0, 0, 2 * DIFF_DK:] = ones_blk.astype(BF16)
            qaug[0, 1, 2 * DIFF_DK:] = ones_blk.astype(BF16)
            acc_s[w] = jnp.zeros(acc_s.shape[1:], F32)

            def online_step(j, kind):
                r = pl.multiple_of(j * KB, KB)
                kblk = kaug[pl.ds(r, KB), :]
                vblk = vaug[j]
                for half in range(2):
                    s_buf[half] = _dot(kblk, qaug[0, half])
                    if kind is not None:
                        s_buf[half] += dt_ref[kind]
                    m_old = m_s[w, half]
                    m_new = jnp.maximum(m_old, jnp.max(s_buf[half], axis=0, keepdims=True))
                    p = jnp.exp2((s_buf[half] - m_new).astype(BF16))
                    acc_s[w, half] = jnp.exp2(m_old - m_new) * acc_s[w, half] + _dot(vblk, p)
                    m_s[w, half] = m_new

            online_step(i_q, 0)

            @pl.when(i_q >= 1)
            def _():
                online_step(i_q - 1, 1)

            def far_online(j, carry):
                online_step(j, None)
                return carry

            lax.fori_loop(0, jnp.maximum(i_q - 1, 0), far_online, 0)
            finalize(w, o_ref)


def _diff_attn(far_split, qt, k, vt, dtiles, lam_params, subln_cols):
    B, S, _ = k.shape
    nq = S // ATTN_BLOCK
    half_blocks = nq // 2
    out_half = jax.ShapeDtypeStruct((B, S // 2, DIFF_WIDTH), BF16)
    return pl.pallas_call(
        _attn_kernel,
        grid=(B, N_DIFF_HEADS, half_blocks),
        in_specs=[
            pl.BlockSpec(memory_space=pltpu.SMEM),
            pl.BlockSpec((None, 2 * DIFF_DK, ATTN_BLOCK), lambda b, h, s: (b, h, s)),
            pl.BlockSpec((None, 2 * DIFF_DK, ATTN_BLOCK), lambda b, h, s: (b, h, nq - 1 - s)),
            pl.BlockSpec((None, S, 2 * DIFF_DK), lambda b, h, s: (b, 0, h)),
            pl.BlockSpec((None, nq, DIFF_DV, ATTN_BLOCK), lambda b, h, s: (b, 0, h, 0)),
            pl.BlockSpec((None, 3, ATTN_BLOCK, ATTN_BLOCK), lambda b, h, s: (h, 0, 0, 0)),
            pl.BlockSpec(lam_params.shape, lambda b, h, s: (0, 0)),
            pl.BlockSpec((DIFF_DV, ATTN_BLOCK), lambda b, h, s: (0, 0)),
        ],
        out_specs=[
            pl.BlockSpec((None, ATTN_BLOCK, DIFF_DV), lambda b, h, s: (b, s, h)),
            pl.BlockSpec((None, ATTN_BLOCK, DIFF_DV), lambda b, h, s: (b, half_blocks - 1 - s, h)),
        ],
        out_shape=[out_half, out_half],
        scratch_shapes=[
            pltpu.VMEM((S, 2 * LANES), BF16),
            pltpu.VMEM((nq, V_AUG_ROWS, ATTN_BLOCK), BF16),
            pltpu.VMEM((1, 2, 2 * LANES, ATTN_BLOCK), BF16),
            pltpu.VMEM((2, 2, 2 * LANES, ATTN_BLOCK), BF16),
            pltpu.VMEM((2, 2, 1, ATTN_BLOCK), F32),
            pltpu.VMEM((2, 2, V_AUG_ROWS, ATTN_BLOCK), F32),
            pltpu.VMEM((2, ATTN_BLOCK, ATTN_BLOCK), F32),
            pltpu.VMEM((2, 2, ATTN_BLOCK, ATTN_BLOCK), BF16),
        ],
        compiler_params=pltpu.CompilerParams(
            dimension_semantics=("arbitrary", "arbitrary", "arbitrary"), vmem_limit_bytes=VMEM_LIMIT_BYTES),
        name="diff_attn",
    )(far_split, qt, qt, k, vt, dtiles, lam_params, subln_cols)


def _mem_kv_kernel(mem_ref, g_ref, wk_t_ref, wv_ref, kt_ref, v_ref):
    mn = _rms(mem_ref[...], g_ref[...]).astype(BF16)
    kt_ref[...] = _dot_nt(wk_t_ref[...], mn).astype(BF16)
    v_ref[...] = _dot(mn, wv_ref[...]).astype(BF16)


def _mem_kv(mem, mem_norm, wk_t, wv):
    B, M, D = mem.shape
    return pl.pallas_call(
        _mem_kv_kernel,
        grid=(B,),
        in_specs=[
            pl.BlockSpec((None, M, D), lambda b: (b, 0, 0)),
            pl.BlockSpec((1, D), lambda b: (0, 0)),
            pl.BlockSpec((D, D), lambda b: (0, 0)),
            pl.BlockSpec((D, D), lambda b: (0, 0)),
        ],
        out_specs=[
            pl.BlockSpec((None, D, M), lambda b: (b, 0, 0)),
            pl.BlockSpec((None, M, D), lambda b: (b, 0, 0)),
        ],
        out_shape=[jax.ShapeDtypeStruct((B, D, M), BF16), jax.ShapeDtypeStruct((B, M, D), BF16)],
        compiler_params=pltpu.CompilerParams(
            dimension_semantics=("arbitrary",), vmem_limit_bytes=VMEM_LIMIT_BYTES),
        name="mem_kv",
    )(mem, mem_norm, wk_t, wv)


def _first_argmax_rows(v, vmax):
    row = lax.broadcasted_iota(jnp.int32, v.shape, 0)
    return jnp.min(jnp.where(v == vmax, row, v.shape[0]), axis=0, keepdims=True)


def _mix_cross_kernel(alo_ref, ahi_ref, u_ref, uprev_ref, x_ref, wout_ref, pw_ref, ps_ref, cn_ref, wq_ref, kt_ref, v_ref,
                      wo_ref, fn_ref, wrh_ref, wrl_ref, rb_ref, h2c_ref):
    i = pl.program_id(1)
    tm = u_ref.shape[0]
    D = x_ref.shape[1]
    hd = kt_ref.shape[0] // N_CROSS_HEADS

    u = u_ref[...]
    prev = jnp.where(i > 0, uprev_ref[...], 0.0)
    pos1 = (i * tm + 1 + lax.broadcasted_iota(jnp.int32, (tm, 1), 0)).astype(F32)
    ys = []
    for g, w in enumerate(POOL_WINDOWS):
        sl = slice(g * POOL_DIM, (g + 1) * POOL_DIM)
        acc = jnp.concatenate([prev[:, sl], u[:, sl]], axis=0)
        span = 1
        while span < w:
            acc = acc[span:] + acc[:-span]
            span *= 2
        first = MAX_WINDOW - (w - 1)
        mean = acc[first:first + tm] / jnp.minimum(pos1, float(w))
        d = (mean - u[:, sl]).astype(BF16)
        ys.append(_dot(d, pw_ref[g]) * ps_ref[:, sl])
    p = jnp.concatenate(ys, axis=-1).astype(BF16)

    a = jnp.where(i < pl.num_programs(1) // 2, alo_ref[...], ahi_ref[...])
    h1 = x_ref[...] + _dot(jnp.concatenate([a, p], axis=-1), wout_ref[...])

    q = _dot(_rms(h1, cn_ref[...]).astype(BF16), wq_ref[...]).astype(BF16)
    outs = []
    for c in range(N_CROSS_HEADS):
        sl = slice(c * hd, (c + 1) * hd)
        s = _dot(q[:, sl], kt_ref[sl, :]) * (hd ** -0.5)
        e = jnp.exp(s - jnp.max(s, axis=-1, keepdims=True))
        pr = (e / jnp.sum(e, axis=-1, keepdims=True)).astype(BF16)
        outs.append(_dot(pr, v_ref[:, sl]))
    o = jnp.concatenate(outs, axis=-1).astype(BF16)
    h2 = h1 + _dot(o, wo_ref[...])
    h2c_ref[:, :D] = h2

    hf = _rms(h2, fn_ref[...])
    hf_hi = hf.astype(BF16)
    hf_lo = (hf - hf_hi.astype(F32)).astype(BF16)
    lg = (_dot_nt(wrh_ref[...], hf_hi) + _dot_nt(wrl_ref[...], hf_hi) + _dot_nt(wrh_ref[...], hf_lo)
          + rb_ref[...])
    R = GROUP_ROW_STRIDE
    gl = lg[:R]
    gmax = jnp.max(gl, axis=0, keepdims=True)
    gsel = _first_argmax_rows(gl, gmax)
    g_w = 1.0 / jnp.sum(jnp.exp(gl - gmax), axis=0, keepdims=True)
    el = lg[R:2 * R]
    for g in range(1, N_EXPERT_GROUPS):
        el = jnp.where(gsel == g, lg[R * (g + 1):R * (g + 2)], el)
    row = lax.broadcasted_iota(jnp.int32, el.shape, 0)
    v0 = jnp.max(el, axis=0, keepdims=True)
    i0 = _first_argmax_rows(el, v0)
    el1 = jnp.where(row == i0, -jnp.inf, el)
    v1 = jnp.max(el1, axis=0, keepdims=True)
    i1 = _first_argmax_rows(el1, v1)
    t = jnp.exp(v1 - v0)
    w0 = 1.0 / (1.0 + t)
    w1 = t / (1.0 + t)
    inner = jnp.where(row == i0, w0, 0.0) + jnp.where(row == i1, w1, 0.0)
    blocks = [g_w * inner, jnp.where(row == 0, gsel.astype(F32), 0.0),
              jnp.zeros((LANES - 2 * R, tm), F32)]
    h2c_ref[:, D:] = jnp.concatenate(blocks, axis=0).T


def _mix_cross(a_lo, a_hi, u, x, wout, pool_w, pool_scale, cross_norm, wq, kt, v, wo, ffn_norm, wr_hi, wr_lo, rbias, tm):
    B, S, D = x.shape
    M = v.shape[1]
    prev_blocks = tm // MAX_WINDOW
    half_tiles = S // tm // 2
    const2 = lambda b, i: (0, 0)
    return pl.pallas_call(
        _mix_cross_kernel,
        grid=(B, S // tm),
        in_specs=[
            pl.BlockSpec((None, tm, DIFF_WIDTH), lambda b, i: (b, jnp.minimum(i, half_tiles - 1), 0)),
            pl.BlockSpec((None, tm, DIFF_WIDTH), lambda b, i: (b, jnp.maximum(i - half_tiles, 0), 0)),
            pl.BlockSpec((None, tm, POOL_WIDTH), lambda b, i: (b, i, 0)),
            pl.BlockSpec((None, MAX_WINDOW, POOL_WIDTH), lambda b, i: (b, jnp.maximum(i * prev_blocks - 1, 0), 0)),
            pl.BlockSpec((None, tm, D), lambda b, i: (b, i, 0)),
            pl.BlockSpec(wout.shape, const2),
            pl.BlockSpec(pool_w.shape, lambda b, i: (0, 0, 0)),
            pl.BlockSpec((1, POOL_WIDTH), const2),
            pl.BlockSpec((1, D), const2),
            pl.BlockSpec(wq.shape, const2),
            pl.BlockSpec((None, D, M), lambda b, i: (b, 0, 0)),
            pl.BlockSpec((None, M, D), lambda b, i: (b, 0, 0)),
            pl.BlockSpec(wo.shape, const2),
            pl.BlockSpec((1, D), const2),
            pl.BlockSpec(wr_hi.shape, const2),
            pl.BlockSpec(wr_lo.shape, const2),
            pl.BlockSpec(rbias.shape, const2),
        ],
        out_specs=pl.BlockSpec((None, tm, D + LANES), lambda b, i: (b, i, 0)),
        out_shape=jax.ShapeDtypeStruct((B, S, D + LANES), F32),
        compiler_params=pltpu.CompilerParams(
            dimension_semantics=("arbitrary", "arbitrary"), vmem_limit_bytes=VMEM_LIMIT_BYTES),
        name="mix_cross",
    )(a_lo, a_hi, u, u, x, wout, pool_w, pool_scale, cross_norm, wq, kt, v, wo, ffn_norm, wr_hi, wr_lo, rbias)


def _moe_kernel(glo_ref, ghi_ref, slot_ref, h2c_hbm, wg_hbm, wu_hbm, wd_hbm, fn_ref, wg_ref, wu_ref, wd_ref, gn_ref,
                y_hbm, src_ref, xbuf, ybuf, hf_s, acc_s, wg_x, wu_x, wd_x, gsem, ssem, wsem, fence_sem):
    c = pl.program_id(0)
    ntiles = pl.num_programs(0)
    tm, D = hf_s.shape
    slot = c % 2

    @pl.when(c == 0)
    def _():
        def invert(t, carry):
            src_ref[slot_ref[t]] = t
            return carry
        lax.fori_loop(0, slot_ref.shape[0], invert, 0, unroll=8)

    def row_gather(tile, r, s):
        t = src_ref[tile * tm + r]
        return pltpu.make_async_copy(h2c_hbm.at[pl.ds(t, 1)], xbuf.at[s, pl.ds(r, 1)], gsem.at[s])

    def row_scatter(tile, r, s):
        t = src_ref[tile * tm + r]
        return pltpu.make_async_copy(ybuf.at[s, pl.ds(r, 1)], y_hbm.at[pl.ds(t, 1)], ssem.at[s])

    def start_rows(make, tile):
        def body(r, carry):
            make(tile, r, tile % 2).start()
            return carry
        lax.fori_loop(0, tm, body, 0, unroll=8)

    def wait_gather(s):
        pltpu.make_async_copy(h2c_hbm.at[pl.ds(0, tm)], xbuf.at[s], gsem.at[s]).wait()

    def wait_scatter(s):
        pltpu.make_async_copy(ybuf.at[s], y_hbm.at[pl.ds(0, tm)], ssem.at[s]).wait()

    @pl.when(c == 0)
    def _():
        start_rows(row_gather, 0)

    wait_gather(slot)

    x = xbuf[slot]
    h2 = x[:, :D]
    hf_s[...] = _rms(h2, fn_ref[...]).astype(BF16)
    group = x[:, D + ROUTE_GROUP_LANE:D + ROUTE_GROUP_LANE + 1]

    def expert_pass(g, wg, wu, wd, first, neighbour_slot=None):
        hf = hf_s[...]
        in_group = group == g.astype(F32)
        n_segments = EXPERTS_PER_GROUP
        rows_per_segment = tm // n_segments
        experts_per_segment = EXPERTS_PER_GROUP // n_segments
        for e in range(EXPERTS_PER_GROUP):
            seg, first_in_seg = divmod(e, experts_per_segment)
            if neighbour_slot is not None and first_in_seg == 0:
                half_segments = n_segments // 2
                first_row = (seg % half_segments) * 2 * rows_per_segment
                for r in range(first_row, first_row + 2 * rows_per_segment):
                    if seg < half_segments:
                        row_gather(c + 1, r, neighbour_slot).start(priority=1)
                    else:
                        row_scatter(c - 1, r, neighbour_slot).start(priority=0)
            hg = _dot(hf, wg[e].astype(BF16))
            hu = _dot(hf, wu[e].astype(BF16))
            cw = jnp.where(in_group, x[:, D + e:D + e + 1], 0.0)
            act = (hg * (1.0 / (1.0 + jnp.exp(-hg))) * hu * cw).astype(BF16)
            part = _dot(act, wd[e].astype(BF16))
            if first and e == 0:
                acc_s[...] = part
            else:
                acc_s[...] += part
            if neighbour_slot is not None and first_in_seg == experts_per_segment - 1:
                pl.semaphore_signal(fence_sem, 1)
                pl.semaphore_wait(fence_sem, 1)

    interior = (c >= 1) & (c + 1 < ntiles)
    for par in range(2):
        @pl.when(interior & (slot == par))
        def _():
            expert_pass(glo_ref[c], wg_ref, wu_ref, wd_ref, True, neighbour_slot=1 - par)

    @pl.when(jnp.logical_not(interior))
    def _():
        @pl.when(c + 1 < ntiles)
        def _():
            start_rows(row_gather, c + 1)

        @pl.when(c >= 1)
        def _():
            start_rows(row_scatter, c - 1)

        expert_pass(glo_ref[c], wg_ref, wu_ref, wd_ref, True)

    def extra_pass(g, carry):
        copies = [pltpu.make_async_copy(w_hbm.at[pl.ds(g * EXPERTS_PER_GROUP, EXPERTS_PER_GROUP)], w_x, wsem.at[n])
                  for n, (w_hbm, w_x) in enumerate(((wg_hbm, wg_x), (wu_hbm, wu_x), (wd_hbm, wd_x)))]
        for cp in copies:
            cp.start()
        for cp in copies:
            cp.wait()
        expert_pass(g, wg_x, wu_x, wd_x, False)
        return carry

    lax.fori_loop(glo_ref[c] + 1, ghi_ref[c] + 1, extra_pass, 0)

    @pl.when(c >= 2)
    def _():
        wait_scatter(slot)

    ybuf[slot] = _rms(h2 + acc_s[...], gn_ref[...])

    @pl.when(c == ntiles - 1)
    def _():
        start_rows(row_scatter, c)
        wait_scatter(slot)

        @pl.when(ntiles >= 2)
        def _():
            wait_scatter(1 - slot)


def _moe(h2c, tile_glo, tile_ghi, slot, ffn_norm, wg, wu, wd, final_norm, tm):
    T, DL = h2c.shape
    D = DL - LANES
    FF = wg.shape[-1]
    E = EXPERTS_PER_GROUP
    any_spec = pl.BlockSpec(memory_space=pl.ANY)
    grid_spec = pltpu.PrefetchScalarGridSpec(
        num_scalar_prefetch=3,
        grid=(T // tm,),
        in_specs=[
            any_spec, any_spec, any_spec, any_spec,
            pl.BlockSpec((1, D), lambda c, glo, ghi, src: (0, 0)),
            pl.BlockSpec((E, D, FF), lambda c, glo, ghi, src: (glo[c], 0, 0)),
            pl.BlockSpec((E, D, FF), lambda c, glo, ghi, src: (glo[c], 0, 0)),
            pl.BlockSpec((E, FF, D), lambda c, glo, ghi, src: (glo[c], 0, 0)),
            pl.BlockSpec((1, D), lambda c, glo, ghi, src: (0, 0)),
        ],
        out_specs=any_spec,
        scratch_shapes=[
            pltpu.SMEM((T,), jnp.int32),
            pltpu.VMEM((2, tm, DL), F32),
            pltpu.VMEM((2, tm, D), F32),
            pltpu.VMEM((tm, D), BF16),
            pltpu.VMEM((tm, D), F32),
            pltpu.VMEM((E, D, FF), wg.dtype),
            pltpu.VMEM((E, D, FF), wu.dtype),
            pltpu.VMEM((E, FF, D), wd.dtype),
            pltpu.SemaphoreType.DMA((2,)),
            pltpu.SemaphoreType.DMA((2,)),
            pltpu.SemaphoreType.DMA((3,)),
            pltpu.SemaphoreType.REGULAR,
        ],
    )
    return pl.pallas_call(
        _moe_kernel,
        grid_spec=grid_spec,
        out_shape=jax.ShapeDtypeStruct((T, D), F32),
        compiler_params=pltpu.CompilerParams(
            dimension_semantics=("arbitrary",), vmem_limit_bytes=VMEM_LIMIT_BYTES),
        name="moe",
    )(tile_glo, tile_ghi, slot, h2c, wg, wu, wd, ffn_norm, wg, wu, wd, final_norm)


def _group_sort_plan(group, tm):
    T = group.shape[0]
    onehot = (group[:, None] == jnp.arange(N_EXPERT_GROUPS, dtype=jnp.int32)[None, :]).astype(jnp.int32)
    ranks = jnp.cumsum(onehot, axis=0) - onehot
    counts = jnp.sum(onehot, axis=0)
    starts = jnp.cumsum(counts) - counts
    slot = jnp.sum(onehot * (ranks + starts[None, :]), axis=1)
    first = jnp.arange(0, T, tm, dtype=jnp.int32)
    group_at = lambda pos: jnp.sum((starts[None, 1:] <= pos[:, None]).astype(jnp.int32), axis=1)
    return slot, group_at(first), group_at(first + (tm - 1))


def _split3_bf16(c):
    hi = c.astype(BF16).astype(F32)
    mid = (c - hi).astype(BF16).astype(F32)
    lo = (c - hi - mid).astype(BF16).astype(F32)
    return jnp.stack([hi, mid, lo], axis=-1)


def _router_operands(router_group, router_group_bias, router_expert, router_expert_bias):
    D = router_group.shape[0]
    R = GROUP_ROW_STRIDE
    w = jnp.zeros((ROUTER_ROWS, D), F32)
    bias = jnp.zeros((ROUTER_ROWS,), F32)
    slab_pad = jnp.full((R - N_EXPERT_GROUPS,), MASK_VALUE, F32)
    w = w.at[:N_EXPERT_GROUPS].set(router_group.T)
    bias = bias.at[:R].set(jnp.concatenate([router_group_bias, slab_pad]))
    for g in range(N_EXPERT_GROUPS):
        w = w.at[R * (g + 1):R * (g + 1) + EXPERTS_PER_GROUP].set(router_expert[g].T)
        bias = bias.at[R * (g + 1):R * (g + 2)].set(jnp.concatenate([router_expert_bias[g], slab_pad]))
    w_hi = w.astype(BF16)
    w_lo = (w - w_hi.astype(F32)).astype(BF16)
    return w_hi, w_lo, bias[:, None]


def kernel(x, mem, rel_bias, attn_norm, w_in, lambda_q1, lambda_k1, lambda_q2, lambda_k2, diff_subln, pool_w,
           pool_scale, w_out, cross_norm, mem_norm, wq_cross, wkv_cross, wo_cross, ffn_norm, router_group,
           router_group_bias, router_expert, router_expert_bias, w_gate, w_up, w_down, final_norm):
    B, S, D = x.shape
    layer = 0
    qt, vt, k, u = _in_proj(x, attn_norm[layer][None], w_in[layer], tm=1024)

    dtiles = _bias_tiles(rel_bias)
    far_split = jnp.concatenate([_split3_bf16(rel_bias[FAR_BUCKET] * LOG2E), rel_bias[0][:, None] * LOG2E],
                                axis=1)
    lam_params = jnp.stack([lambda_q1[layer], lambda_k1[layer], lambda_q2[layer], lambda_k2[layer]])
    subln_cols = jnp.broadcast_to(diff_subln[layer][:, None], (DIFF_DV, ATTN_BLOCK))
    a_lo, a_hi = _diff_attn(far_split, qt, k, vt, dtiles, lam_params, subln_cols)

    wkv = wkv_cross[layer]
    kt, v = _mem_kv(mem, mem_norm[layer][None], wkv[:, :D].T.astype(BF16), wkv[:, D:].astype(BF16))

    wr_hi, wr_lo, rbias = _router_operands(router_group[layer], router_group_bias[layer],
                                           router_expert[layer], router_expert_bias[layer])
    h2c = _mix_cross(a_lo, a_hi, u, x, w_out[layer].astype(BF16), pool_w[layer].astype(BF16), pool_scale[layer][None],
                     cross_norm[layer][None], wq_cross[layer].astype(BF16), kt, v,
                     wo_cross[layer].astype(BF16), ffn_norm[layer][None], wr_hi, wr_lo, rbias, tm=1024)
    h2c = h2c.reshape(B * S, D + LANES)

    moe_tm = 512
    slot, tile_glo, tile_ghi = _group_sort_plan(h2c[:, D + ROUTE_GROUP_LANE].astype(jnp.int32), moe_tm)
    y = _moe(h2c, tile_glo, tile_ghi, slot, ffn_norm[layer][None], w_gate[layer], w_up[layer], w_down[layer],
             final_norm[None], moe_tm)
    return y.reshape(B, S, D)
```
